```python
import math
import jax, jax.numpy as jnp
from jax import lax
import numpy as np

D_MODEL = 1024
BATCH = 8
SEQ = 4096
DEPTH = 4

GRID_W = 64
CTX_LEN = 256
HEAD_DIM = 64
N_BRANCH = 4
BRANCH_W = D_MODEL // N_BRANCH
N_HEADS = BRANCH_W // HEAD_DIM
BLOCK = 128
ROPE_THETA = 10000.0
EPS = 1e-6
H_A = N_HEADS
DK_A = HEAD_DIM
DV_A = HEAD_DIM
CONV_K = 5
CHUNK = 64
H_B = N_HEADS
DH_B = HEAD_DIM // 2
DV_B = 2 * DH_B
HQ_C = N_HEADS
HKV_C = N_HEADS // 2
HQ_D = N_HEADS
HKV_D = N_HEADS // 2
WINDOW = 128
N_EXPERTS = 16
D_EXPERT = D_MODEL
CAPACITY_FACTOR = 2
IN_SPLITS = (H_A * DK_A, H_A * DK_A, H_A * DV_A, H_A, H_A, H_A, H_A, H_A * DV_A,
             2 * H_B * DH_B, 2 * H_B * DH_B, H_B * DV_B,
             HQ_C * HEAD_DIM, HKV_C * HEAD_DIM, HKV_C * HEAD_DIM,
             HQ_D * HEAD_DIM, HKV_D * HEAD_DIM, HKV_D * HEAD_DIM,
             N_BRANCH * D_MODEL)
D_IN = sum(IN_SPLITS)

kernel_name = 'hybrid_diffusion_gdn_diffattn_gqa_swa_ecmoe'

F32 = jnp.float32


def layer_norm(x, g, b):
    xf = x.astype(F32)
    mu = jnp.mean(xf, -1, keepdims=True)
    var = jnp.mean(jnp.square(xf - mu), -1, keepdims=True)
    return ((xf - mu) * lax.rsqrt(var + EPS) * g + b).astype(x.dtype)


def rms_norm(x, g):
    xf = x.astype(F32)
    return (xf * lax.rsqrt(jnp.mean(xf * xf, -1, keepdims=True) + EPS) * g).astype(x.dtype)


def l2norm(x):
    xf = x.astype(F32)
    return (xf * lax.rsqrt(jnp.sum(xf * xf, -1, keepdims=True) + EPS)).astype(x.dtype)


def split_cols(p):
    out, off = [], 0
    for w in IN_SPLITS:
        out.append(p[..., off:off + w])
        off += w
    return out


def to_heads(p, h):
    b_, n, _ = p.shape
    return p.reshape(b_, n, h, -1).transpose(0, 2, 1, 3)


def from_heads(o):
    b_, h, n, d = o.shape
    return o.transpose(0, 2, 1, 3).reshape(b_, n, h * d)


def to_groups(p, hk, g):
    b_, n, _ = p.shape
    return p.reshape(b_, n, hk, g, -1).transpose(0, 2, 3, 1, 4)


def from_groups(o):
    b_, hk, g, n, d = o.shape
    return o.transpose(0, 3, 1, 2, 4).reshape(b_, n, hk * g * d)


def axial_rope_tables(n, dim):
    rows = n // GRID_W
    row = jnp.repeat(jnp.arange(rows), GRID_W).astype(F32)
    col = jnp.tile(jnp.arange(GRID_W), rows).astype(F32)
    nf = dim // 4
    inv = ROPE_THETA ** (-jnp.arange(nf, dtype=F32) / nf)
    ar = row[:, None] * inv
    ac = col[:, None] * inv
    return (jnp.cos(ar), jnp.sin(ar), jnp.cos(ac), jnp.sin(ac))


def apply_axial_rope(x, tabs):
    cr, sr, cc, sc = tabs
    x1, x2, x3, x4 = jnp.split(x, 4, axis=-1)
    return jnp.concatenate([x1 * cr - x2 * sr, x2 * cr + x1 * sr,
                            x3 * cc - x4 * sc, x4 * cc + x3 * sc], axis=-1).astype(x.dtype)


def sweep_query_blocks(fn, q):
    n, d = q.shape[-2], q.shape[-1]
    nb = n // BLOCK
    qb = jnp.moveaxis(q.reshape(q.shape[:-2] + (nb, BLOCK, d)), -3, 0)
    o = jnp.moveaxis(lax.map(fn, qb), 0, -3)
    return o.reshape(o.shape[:-3] + (n, o.shape[-1]))


def short_conv(u, w):
    half = CONV_K // 2
    y = lax.conv_general_dilated(u, w[:, None, :].astype(u.dtype), window_strides=(1,),
                                 padding=[(half, half)], dimension_numbers=('NWC', 'WIO', 'NWC'),
                                 feature_group_count=u.shape[-1])
    return jax.nn.silu(y)


def gated_delta_chunked(q, k, v, g, beta):
    out_dtype = v.dtype
    b_, h, t, dk = q.shape
    dv = v.shape[-1]
    nc = t // CHUNK
    chunks = lambda a: a.astype(F32).reshape(b_, h, nc, CHUNK, -1)
    q, k, v = chunks(q), chunks(k), chunks(v)
    g = g.astype(F32).reshape(b_, h, nc, CHUNK)
    beta = beta.astype(F32).reshape(b_, h, nc, CHUNK)
    G = jnp.cumsum(g, axis=-1)
    incl = jnp.tril(jnp.ones((CHUNK, CHUNK), bool))
    strict = jnp.tril(jnp.ones((CHUNK, CHUNK), bool), -1)
    decay = jnp.exp(jnp.where(incl, G[..., :, None] - G[..., None, :], -jnp.inf))
    a_mat = jnp.where(strict, beta[..., :, None] * jnp.einsum('bhncd,bhnsd->bhncs', k, k) * decay, 0.0)
    l_mat = a_mat + jnp.eye(CHUNK, dtype=F32)
    rhs = jnp.concatenate([(beta * jnp.exp(G))[..., None] * k, beta[..., None] * v], axis=-1)
    sol = lax.linalg.triangular_solve(l_mat, rhs, left_side=True, lower=True, unit_diagonal=True)
    w_c, u_c = sol[..., :dk], sol[..., dk:]
    p_c = jnp.einsum('bhncd,bhnsd->bhncs', q, k) * decay
    qg = q * jnp.exp(G)[..., None]
    kd = k * jnp.exp(G[..., -1:] - G)[..., None]
    gl = jnp.exp(G[..., -1])

    def step(S, xs):
        w_i, u_i, p_i, qg_i, kd_i, gl_i = xs
        u = u_i - jnp.einsum('bhcd,bhde->bhce', w_i, S)
        o = jnp.einsum('bhcd,bhde->bhce', qg_i, S) + jnp.einsum('bhcs,bhse->bhce', p_i, u)
        S = gl_i[..., None, None] * S + jnp.einsum('bhcd,bhce->bhde', kd_i, u)
        return S, o

    xs = tuple(jnp.moveaxis(a, 2, 0) for a in (w_c, u_c, p_c, qg, kd, gl))
    _, o = lax.scan(step, jnp.zeros((b_, h, dk, dv), F32), xs)
    return jnp.moveaxis(o, 0, 2).reshape(b_, h, t, dv).astype(out_dtype)


def gdn_branch(p_lat, p_ctx, conv_w, a_log, dt_bias, norm_g, with_ctx):
    def prep(p):
        q, k, v, a_f, b_f, a_b, b_b, gate = p
        qkv = short_conv(jnp.concatenate([q, k, v], axis=-1), conv_w)
        q, k, v = jnp.split(qkv, 3, axis=-1)
        q = l2norm(to_heads(q, H_A)) * (DK_A ** -0.5)
        k = l2norm(to_heads(k, H_A))
        v = to_heads(v, H_A)

        def decay_gate(a, b, d):
            lg = -jnp.exp(a_log[d].astype(F32)) * jax.nn.softplus(a.astype(F32) + dt_bias[d])
            return jnp.swapaxes(lg, 1, 2), jnp.swapaxes(jax.nn.sigmoid(b.astype(F32)), 1, 2)
        return q, k, v, decay_gate(a_f, b_f, 0), decay_gate(a_b, b_b, 1), gate

    qc, kc, vc, fwd_c, bwd_c, gate_c = prep(p_ctx)
    ql, kl, vl, fwd_l, bwd_l, gate_l = prep(p_lat)
    lc = qc.shape[2]
    cat = lambda a, b: jnp.concatenate([a, b], axis=2)
    rev = lambda a: jnp.flip(a, axis=2)
    o_f = gated_delta_chunked(cat(qc, ql), cat(kc, kl), cat(vc, vl),
                              cat(fwd_c[0], fwd_l[0]), cat(fwd_c[1], fwd_l[1]))
    o_b = gated_delta_chunked(cat(rev(qc), rev(ql)), cat(rev(kc), rev(kl)), cat(rev(vc), rev(vl)),
                              cat(rev(bwd_c[0]), rev(bwd_l[0])), cat(rev(bwd_c[1]), rev(bwd_l[1])))

    def finish(o, gate):
        b_, h, n, d = o.shape
        o = rms_norm(jnp.swapaxes(o, 1, 2), norm_g) * jax.nn.silu(gate.reshape(b_, n, h, d))
        return o.reshape(b_, n, h * d)

    o_lat = finish(o_f[:, :, lc:] + rev(o_b[:, :, lc:]), gate_l)
    o_ctx = finish(o_f[:, :, :lc] + rev(o_b[:, :, :lc]), gate_c) if with_ctx else None
    return o_lat, o_ctx


def diff_core(q, k, v, lam):
    s = jnp.einsum('bhmqd,bhmkd->bhmqk', q, k, preferred_element_type=F32) * (q.shape[-1] ** -0.5)
    p = jax.nn.softmax(s, axis=-1)
    p = p[:, :, 0] - lam * p[:, :, 1]
    return jnp.einsum('bhqk,bhkd->bhqd', p.astype(v.dtype), v)


def diff_branch(p_lat, p_ctx, lam_vecs, lam_init, norm_g, rope, with_ctx):
    lv = lam_vecs.astype(F32)
    lam = jnp.exp(jnp.sum(lv[0] * lv[1])) - jnp.exp(jnp.sum(lv[2] * lv[3])) + lam_init

    def qkv(p):
        q, k, v = p
        b_, n, _ = q.shape
        maps = lambda t: t.reshape(b_, n, H_B, 2, DH_B).transpose(0, 2, 3, 1, 4)
        return maps(q), maps(k), to_heads(v, H_B)

    ql, kl, vl = qkv(p_lat)
    ql, kl = apply_axial_rope(ql, rope), apply_axial_rope(kl, rope)
    qc, kc, vc = qkv(p_ctx)
    k_all = jnp.concatenate([kc, kl], axis=3)
    v_all = jnp.concatenate([vc, vl], axis=2)
    finish = lambda o: from_heads(rms_norm(o, norm_g) * (1.0 - lam_init))
    o_lat = finish(sweep_query_blocks(lambda qb: diff_core(qb, k_all, v_all, lam), ql))
    o_ctx = finish(diff_core(qc, kc, vc, lam)) if with_ctx else None
    return o_lat, o_ctx


def gqa_core(q, k, v):
    s = jnp.einsum('bhgqd,bhkd->bhgqk', q, k, preferred_element_type=F32) * (q.shape[-1] ** -0.5)
    p = jax.nn.softmax(s, axis=-1).astype(v.dtype)
    return jnp.einsum('bhgqk,bhkd->bhgqd', p, v)


def gqa_branch(p_lat, p_ctx, qk_g, rope, with_ctx):
    grp = HQ_C // HKV_C

    def qkv(p):
        q, k, v = p
        b_, n, _ = q.shape
        q = rms_norm(q.reshape(b_, n, HQ_C, HEAD_DIM), qk_g[0]).reshape(b_, n, HKV_C, grp, HEAD_DIM)
        k = rms_norm(k.reshape(b_, n, HKV_C, HEAD_DIM), qk_g[1])
        return q.transpose(0, 2, 3, 1, 4), k.transpose(0, 2, 1, 3), to_heads(v, HKV_C)

    ql, kl, vl = qkv(p_lat)
    ql, kl = apply_axial_rope(ql, rope), apply_axial_rope(kl, rope)
    qc, kc, vc = qkv(p_ctx)
    k_all = jnp.concatenate([kc, kl], axis=2)
    v_all = jnp.concatenate([vc, vl], axis=2)
    o_lat = from_groups(sweep_query_blocks(lambda qb: gqa_core(qb, k_all, v_all), ql))
    o_ctx = from_groups(gqa_core(qc, kc, vc)) if with_ctx else None
    return o_lat, o_ctx


def sink_attention(q, k, v, sink):
    s = jnp.einsum('bhgqd,bhkd->bhgqk', q, k, preferred_element_type=F32) * (q.shape[-1] ** -0.5)
    s_sink = jnp.broadcast_to(sink.astype(F32)[None, :, :, None, None], s.shape[:-1] + (1,))
    p = jax.nn.softmax(jnp.concatenate([s, s_sink], axis=-1), axis=-1)[..., :-1]
    return jnp.einsum('bhgqk,bhkd->bhgqd', p.astype(v.dtype), v)


def window_attention_latent(q, k_lat, v_lat, k_ctx, v_ctx, sink):
    b_, hk, grp, n, d = q.shape
    nb = n // BLOCK
    lc = k_ctx.shape[2]

    def band(t):
        tp = jnp.pad(t, ((0, 0), (0, 0), (BLOCK, BLOCK), (0, 0))).reshape(b_, hk, nb + 2, BLOCK, d)
        return jnp.concatenate([tp[:, :, :-2], tp[:, :, 1:-1], tp[:, :, 2:]], axis=3)

    kb, vb = band(k_lat), band(v_lat)
    qb = q.reshape(b_, hk, grp, nb, BLOCK, d)
    scale = d ** -0.5
    s_loc = jnp.einsum('bhgnqd,bhnkd->bhgnqk', qb, kb, preferred_element_type=F32) * scale
    r = jnp.arange(BLOCK)[:, None]
    j = jnp.arange(3 * BLOCK)[None, :]
    s_abs = (jnp.arange(nb)[:, None, None] - 1) * BLOCK + j
    valid = (jnp.abs(BLOCK + r - j) <= WINDOW) & (s_abs >= 0) & (s_abs < n)
    s_loc = jnp.where(valid, s_loc, -jnp.inf)
    s_ctx = jnp.einsum('bhgnqd,bhkd->bhgnqk', qb, k_ctx, preferred_element_type=F32) * scale
    s_sink = jnp.broadcast_to(sink.astype(F32)[None, :, :, None, None, None], s_ctx.shape[:-1] + (1,))
    p = jax.nn.softmax(jnp.concatenate([s_ctx, s_loc, s_sink], axis=-1), axis=-1).astype(v_lat.dtype)
    o = (jnp.einsum('bhgnqk,bhkd->bhgnqd', p[..., :lc], v_ctx)
         + jnp.einsum('bhgnqk,bhnkd->bhgnqd', p[..., lc:lc + 3 * BLOCK], vb))
    return o.reshape(b_, hk, grp, n, d)


def window_branch(p_lat, p_ctx, sink, rope, with_ctx):
    grp = HQ_D // HKV_D
    sink = sink.reshape(HKV_D, grp)
    qkv = lambda p: (to_groups(p[0], HKV_D, grp), to_heads(p[1], HKV_D), to_heads(p[2], HKV_D))
    ql, kl, vl = qkv(p_lat)
    ql, kl = apply_axial_rope(ql, rope), apply_axial_rope(kl, rope)
    qc, kc, vc = qkv(p_ctx)
    o_lat = from_groups(window_attention_latent(ql, kl, vl, kc, vc, sink))
    o_ctx = from_groups(sink_attention(qc, kc, vc, sink)) if with_ctx else None
    return o_lat, o_ctx


def merge_branches(outs, gate_logits, w_br, w_out):
    b_, n, _ = gate_logits.shape
    gl = gate_logits.reshape(b_, n, N_BRANCH, D_MODEL)
    m = sum(jax.nn.sigmoid(gl[:, :, i]) * (outs[i] @ w_br[i]) for i in range(N_BRANCH))
    return m @ w_out


def token_mixers(h_lat, h_ctx, w_in, conv_w, a_log, dt_bias, gdn_g, lam_vecs, lam_init, diff_g,
                 qk_g, sink, w_br, w_out, rope_hd, rope_b, with_ctx):
    pl = split_cols(h_lat @ w_in)
    pc = split_cols(h_ctx @ w_in)
    a_lat, a_ctx = gdn_branch(pl[0:8], pc[0:8], conv_w, a_log, dt_bias, gdn_g, with_ctx)
    b_lat, b_ctx = diff_branch(pl[8:11], pc[8:11], lam_vecs, lam_init, diff_g, rope_b, with_ctx)
    g_lat, g_ctx = gqa_branch(pl[11:14], pc[11:14], qk_g, rope_hd, with_ctx)
    w_lat, w_ctx = window_branch(pl[14:17], pc[14:17], sink, rope_hd, with_ctx)
    y_lat = merge_branches((a_lat, b_lat, g_lat, w_lat), pl[17], w_br, w_out)
    y_ctx = merge_branches((a_ctx, b_ctx, g_ctx, w_ctx), pc[17], w_br, w_out) if with_ctx else None
    return y_lat, y_ctx


def expert_choice_ffn(h, w_router, w_gate, w_up, w_down):
    b_, n, dm = h.shape
    cap = max(1, CAPACITY_FACTOR * n // N_EXPERTS)
    aff = jax.nn.softmax(jnp.einsum('bnd,de->bne', h, w_router, preferred_element_type=F32), axis=-1)
    gval, idx = lax.top_k(jnp.swapaxes(aff, 1, 2), cap)
    xg = jax.vmap(lambda hb, ib: hb[ib])(h, idx)
    a = jnp.einsum('becd,edf->becf', xg, w_gate)
    u = jnp.einsum('becd,edf->becf', xg, w_up)
    y = jnp.einsum('becf,efd->becd', jax.nn.silu(a) * u, w_down)
    y = y * gval[..., None].astype(y.dtype)
    return jax.vmap(lambda yb, ib: jnp.zeros((n, dm), yb.dtype).at[ib.reshape(-1)].add(yb.reshape(-1, dm)))(y, idx)


def setup_inputs(seed: int = 0) -> dict:
    key = jax.random.key(seed)
    ks = jax.random.split(key, 25)
    D = D_MODEL
    beta_dn = (8.0 * DEPTH) ** -0.25
    nrm = lambda k, shape, s: jax.random.normal(k, shape, F32) * s
    gain = lambda k, shape: 1.0 + 0.02 * jax.random.normal(k, shape, F32)
    dt = jnp.exp(jax.random.uniform(ks[9], (DEPTH, 2, H_A), F32, math.log(1e-3), math.log(1e-1)))
    return {
        'x': nrm(ks[0], (BATCH, SEQ, D), 1.0),
        'c': nrm(ks[1], (BATCH, D), 1.0),
        'ctx': nrm(ks[2], (BATCH, CTX_LEN, D), 1.0),
        'c_ctx': nrm(ks[3], (D,), 1.0),
        'w_mod': nrm(ks[4], (DEPTH, D, 6 * D), 0.5 * D ** -0.5),
        'b_mod': nrm(ks[5], (DEPTH, 6 * D), 0.02),
        'w_in': nrm(ks[6], (DEPTH, D, D_IN), D ** -0.5),
        'conv_a': nrm(ks[7], (DEPTH, CONV_K, 3 * H_A * DK_A), CONV_K ** -0.5),
        'a_log': jnp.log(jax.random.uniform(ks[8], (DEPTH, 2, H_A), F32, 1.0, 16.0)),
        'dt_bias': dt + jnp.log(-jnp.expm1(-dt)),
        'gdn_norm': gain(ks[10], (DEPTH, DV_A)),
        'diff_lambda': nrm(ks[11], (DEPTH, 4, DH_B), 0.1),
        'diff_norm': gain(ks[12], (DEPTH, DV_B)),
        'qk_norm_c': gain(ks[13], (DEPTH, 2, HEAD_DIM)),
        'sink_d': nrm(ks[14], (DEPTH, HQ_D), 0.5),
        'w_br': nrm(ks[15], (DEPTH, N_BRANCH, BRANCH_W, D), beta_dn * BRANCH_W ** -0.5),
        'w_out': nrm(ks[16], (DEPTH, D, D), beta_dn * D ** -0.5),
        'ln1_g': gain(ks[17], (DEPTH, D)),
        'ln1_b': nrm(ks[18], (DEPTH, D), 0.02),
        'w_router': nrm(ks[19], (DEPTH, D, N_EXPERTS), D ** -0.5),
        'w_gate_e': nrm(ks[20], (DEPTH, N_EXPERTS, D, D_EXPERT), D ** -0.5),
        'w_up_e': nrm(ks[21], (DEPTH, N_EXPERTS, D, D_EXPERT), D ** -0.5),
        'w_down_e': nrm(ks[22], (DEPTH, N_EXPERTS, D_EXPERT, D), beta_dn * D_EXPERT ** -0.5),
        'ln2_g': gain(ks[23], (DEPTH, D)),
        'ln2_b': nrm(ks[24], (DEPTH, D), 0.02),
    }


def reference(x, c, ctx, c_ctx, w_mod, b_mod, w_in, conv_a, a_log, dt_bias, gdn_norm,
              diff_lambda, diff_norm, qk_norm_c, sink_d, w_br, w_out, ln1_g, ln1_b,
              w_router, w_gate_e, w_up_e, w_down_e, ln2_g, ln2_b):
    alpha = (2.0 * DEPTH) ** 0.25
    n = x.shape[1]
    rope_hd = axial_rope_tables(n, HEAD_DIM)
    rope_b = axial_rope_tables(n, DH_B)
    x_lat, x_ctx = x, ctx
    for l in range(DEPTH):
        last = l == DEPTH - 1
        lam_init = 0.8 - 0.6 * math.exp(-0.3 * l)
        mod = jax.nn.silu(c) @ w_mod[l] + b_mod[l]
        mod_c = jax.nn.silu(c_ctx) @ w_mod[l] + b_mod[l]
        sh1, sc1, g1, sh2, sc2, g2 = jnp.split(mod[:, None, :], 6, axis=-1)
        csh1, csc1, cg1, csh2, csc2, cg2 = jnp.split(mod_c, 6)
        y_lat, y_ctx = token_mixers(x_lat * (1 + sc1) + sh1, x_ctx * (1 + csc1) + csh1,
                                    w_in[l], conv_a[l], a_log[l], dt_bias[l], gdn_norm[l],
                                    diff_lambda[l], lam_init, diff_norm[l], qk_norm_c[l], sink_d[l],
                                    w_br[l], w_out[l], rope_hd, rope_b, not last)
        x_lat = layer_norm(alpha * x_lat + g1 * y_lat, ln1_g[l], ln1_b[l])
        f_lat = expert_choice_ffn(x_lat * (1 + sc2) + sh2, w_router[l], w_gate_e[l], w_up_e[l], w_down_e[l])
        x_lat = layer_norm(alpha * x_lat + g2 * f_lat, ln2_g[l], ln2_b[l])
        if not last:
            x_ctx = layer_norm(alpha * x_ctx + cg1 * y_ctx, ln1_g[l], ln1_b[l])
            f_ctx = expert_choice_ffn(x_ctx * (1 + csc2) + csh2, w_router[l], w_gate_e[l], w_up_e[l], w_down_e[l])
            x_ctx = layer_norm(alpha * x_ctx + cg2 * f_ctx, ln2_g[l], ln2_b[l])
    return x_lat
```

```python
import functools
import math

import jax
import jax.numpy as jnp
from jax import lax
from jax.experimental import pallas as pl
from jax.experimental.pallas import tpu as pltpu

F32 = jnp.float32
BF16 = jnp.bfloat16

D_MODEL = 1024
HEAD_DIM = 64
N_HEADS = 4
BRANCH_W = 256
N_BRANCH = 4
GRID_W = 64
ROPE_THETA = 10000.0
EPS = 1e-6
CONV_K = 5
CHUNK = 64
DH_B = 32
WINDOW = 128
N_EXPERTS = 16
CAPACITY_FACTOR = 2
LOG2E = 1.4426950408889634

TOKEN_TILE = 256
LANES = 128
VMEM_LIMIT = 56 * 1024 * 1024

GATE_W = N_BRANCH * D_MODEL
C_GATES = 0
C_AQKV = C_GATES + GATE_W
C_AAB = C_AQKV + 3 * BRANCH_W
C_AGATE = C_AAB + LANES
C_B = C_AGATE + BRANCH_W
C_C = C_B + 3 * BRANCH_W
C_D = C_C + 2 * BRANCH_W
W_ALL = C_D + 2 * BRANCH_W


def _cparams(sem, vmem=VMEM_LIMIT):
    return pltpu.CompilerParams(dimension_semantics=sem, vmem_limit_bytes=vmem)


def _bdot(a, b):
    return jnp.dot(a.astype(BF16), b.astype(BF16), preferred_element_type=F32)


def _bdot_nt(a, b):
    return lax.dot_general(a.astype(BF16), b.astype(BF16), (((1,), (1,)), ((), ())),
                           preferred_element_type=F32)


def _silu(x):
    return x * jax.nn.sigmoid(x)


def _block_diag_ones(width, group):
    r = lax.broadcasted_iota(jnp.int32, (width, width), 0) // group
    c = lax.broadcasted_iota(jnp.int32, (width, width), 1) // group
    return jnp.where(r == c, 1.0, 0.0).astype(BF16)


def _group_sum(x, ones_bd):
    hi = x.astype(BF16)
    lo = (x - hi.astype(F32)).astype(BF16)
    return (jnp.dot(hi, ones_bd, preferred_element_type=F32)
            + jnp.dot(lo, ones_bd, preferred_element_type=F32))


def _pair_swap(x, q):
    w = x.shape[-1]
    lane = lax.broadcasted_iota(jnp.int32, x.shape, x.ndim - 1)
    fwd = pltpu.roll(x, w - q, x.ndim - 1)
    bwd = pltpu.roll(x, q, x.ndim - 1)
    return jnp.where((lane % (2 * q)) < q, fwd, bwd)


def _layer_norm(x, g, b):
    mu = jnp.mean(x, axis=-1, keepdims=True)
    xc = x - mu
    var = jnp.mean(xc * xc, axis=-1, keepdims=True)
    return xc * lax.rsqrt(var + EPS) * g + b


def _mod_kernel(c_ref, w_ref, b_ref, o_ref):
    o_ref[0] = _bdot(_silu(c_ref[...]), w_ref[0]) + b_ref[0]


def _mod_call(cc, w_mod, b_mod):
    depth, d, n6 = w_mod.shape
    rows = cc.shape[0]
    nb = 1536
    return pl.pallas_call(
        _mod_kernel,
        grid=(depth, n6 // nb),
        in_specs=[pl.BlockSpec((rows, d), lambda l, j: (0, 0)),
                  pl.BlockSpec((1, d, nb), lambda l, j: (l, 0, j)),
                  pl.BlockSpec((1, 1, nb), lambda l, j: (l, 0, j))],
        out_specs=pl.BlockSpec((1, rows, nb), lambda l, j: (l, 0, j)),
        out_shape=jax.ShapeDtypeStruct((depth, rows, n6), F32),
        compiler_params=_cparams(("arbitrary", "arbitrary")),
        name="mod_vectors",
    )(cc, w_mod, b_mod.reshape(depth, 1, n6))


def _rope(x, cos, sin, quarter):
    return x * cos + _pair_swap(x, quarter) * sin


def _store_heads(ref, x, n_heads):
    for h in range(n_heads):
        ref[0, h] = x[:, h * HEAD_DIM:(h + 1) * HEAD_DIM].astype(ref.dtype)


def _inproj_kernel(x_ref, mod_ref, w_ref, cos_hd_ref, sin_hd_ref, cos_b_ref, sin_b_ref, gq_ref, gk_ref,
                   gates_ref, aqkv_ref, aab_ref, agate_ref,
                   qb_ref, kb_ref, vb_ref, qc_ref, kc_ref, vc_ref, qd_ref, kd_ref, vd_ref):
    d = D_MODEL
    x = x_ref[0]
    sh1 = mod_ref[0, 0, :, 0:d]
    sc1 = mod_ref[0, 0, :, d:2 * d]
    h = (x * (1.0 + sc1) + sh1).astype(BF16)

    def proj(a, width):
        return jnp.dot(h, w_ref[:, a:a + width], preferred_element_type=F32)

    gates_ref[0] = proj(C_GATES, GATE_W).astype(gates_ref.dtype)
    aqkv_ref[0] = proj(C_AQKV, 3 * BRANCH_W)
    aab_ref[0] = proj(C_AAB, LANES)
    agate_ref[0] = proj(C_AGATE, BRANCH_W).astype(agate_ref.dtype)

    cos_hd, sin_hd = cos_hd_ref[...], sin_hd_ref[...]
    cos_b, sin_b = cos_b_ref[...], sin_b_ref[...]
    kv_w = BRANCH_W // 2

    pb = proj(C_B, 3 * BRANCH_W)
    qb = _rope(pb[:, 0:BRANCH_W], cos_b, sin_b, DH_B // 4) * (DH_B ** -0.5 * LOG2E)
    kb = _rope(pb[:, BRANCH_W:2 * BRANCH_W], cos_b, sin_b, DH_B // 4)
    _store_heads(qb_ref, qb, N_HEADS)
    _store_heads(kb_ref, kb, N_HEADS)
    _store_heads(vb_ref, pb[:, 2 * BRANCH_W:3 * BRANCH_W], N_HEADS)

    pc = proj(C_C, 2 * BRANCH_W)
    ones_q = _block_diag_ones(BRANCH_W, HEAD_DIM)
    ones_k = _block_diag_ones(kv_w, HEAD_DIM)
    qc = pc[:, 0:BRANCH_W]
    kc = pc[:, BRANCH_W:BRANCH_W + kv_w]
    qc = qc * lax.rsqrt(_group_sum(qc * qc, ones_q) * (1.0 / HEAD_DIM) + EPS) * gq_ref[...]
    kc = kc * lax.rsqrt(_group_sum(kc * kc, ones_k) * (1.0 / HEAD_DIM) + EPS) * gk_ref[...]
    qc = _rope(qc, cos_hd, sin_hd, HEAD_DIM // 4) * (HEAD_DIM ** -0.5 * LOG2E)
    kc = _rope(kc, cos_hd[:, 0:kv_w], sin_hd[:, 0:kv_w], HEAD_DIM // 4)
    _store_heads(qc_ref, qc, N_HEADS)
    _store_heads(kc_ref, kc, N_HEADS // 2)
    _store_heads(vc_ref, pc[:, BRANCH_W + kv_w:2 * BRANCH_W], N_HEADS // 2)

    pd = proj(C_D, 2 * BRANCH_W)
    qd = _rope(pd[:, 0:BRANCH_W], cos_hd, sin_hd, HEAD_DIM // 4) * (HEAD_DIM ** -0.5 * LOG2E)
    kd = _rope(pd[:, BRANCH_W:BRANCH_W + kv_w], cos_hd[:, 0:kv_w], sin_hd[:, 0:kv_w], HEAD_DIM // 4)
    _store_heads(qd_ref, qd, N_HEADS)
    _store_heads(kd_ref, kd, N_HEADS // 2)
    _store_heads(vd_ref, pd[:, BRANCH_W + kv_w:2 * BRANCH_W], N_HEADS // 2)


def _inproj_call(x_all, modv, w_all, tabs, gq, gk, n_lat_tiles):
    b_, s, d = x_all.shape
    ts = TOKEN_TILE
    nt = s // ts
    tok = lambda w: pl.BlockSpec((1, ts, w), lambda b, j: (b, j, 0))
    tab = pl.BlockSpec((ts, BRANCH_W), lambda b, j: (j, 0))
    head = lambda nh: pl.BlockSpec((1, nh, ts, HEAD_DIM), lambda b, j: (b, 0, j, 0))
    hshape = lambda nh: jax.ShapeDtypeStruct((b_, nh, s, HEAD_DIM), BF16)
    kvh = N_HEADS // 2
    return pl.pallas_call(
        _inproj_kernel,
        grid=(b_, nt),
        in_specs=[tok(d),
                  pl.BlockSpec((1, 1, 1, 6 * d), lambda b, j: (b, j // n_lat_tiles, 0, 0)),
                  pl.BlockSpec((d, W_ALL), lambda b, j: (0, 0), pipeline_mode=pl.Buffered(1)),
                  tab, tab, tab, tab,
                  pl.BlockSpec((1, BRANCH_W), lambda b, j: (0, 0)),
                  pl.BlockSpec((1, BRANCH_W // 2), lambda b, j: (0, 0))],
        out_specs=[tok(GATE_W), tok(3 * BRANCH_W), tok(LANES), tok(BRANCH_W),
                   head(N_HEADS), head(N_HEADS), head(N_HEADS),
                   head(N_HEADS), head(kvh), head(kvh),
                   head(N_HEADS), head(kvh), head(kvh)],
        out_shape=[jax.ShapeDtypeStruct((b_, s, GATE_W), BF16),
                   jax.ShapeDtypeStruct((b_, s, 3 * BRANCH_W), F32),
                   jax.ShapeDtypeStruct((b_, s, LANES), F32),
                   jax.ShapeDtypeStruct((b_, s, BRANCH_W), BF16),
                   hshape(N_HEADS), hshape(N_HEADS), hshape(N_HEADS),
                   hshape(N_HEADS), hshape(kvh), hshape(kvh),
                   hshape(N_HEADS), hshape(kvh), hshape(kvh)],
        compiler_params=_cparams(("parallel", "arbitrary")),
        name="inproj",
    )(x_all, modv, w_all, *tabs, gq, gk)


NEG_BIG = -1e30


def _softmax_pv(q2, k_ref, v_ref, s_ref, chunks, mask_fn=None, extra_logit=None):
    m = None
    for ks, n, col in chunks:
        s = _bdot_nt(q2, k_ref[pl.ds(ks, n), :])
        if mask_fn is not None:
            s = mask_fn(s, ks, n, col)
        s_ref[:, col:col + n] = s
        cm = jnp.max(s, axis=1, keepdims=True)
        m = cm if m is None else jnp.maximum(m, cm)
    if extra_logit is not None:
        m = jnp.maximum(m, extra_logit)
    l = None
    acc = None
    for ks, n, col in chunks:
        p = jnp.exp2(s_ref[:, col:col + n] - m)
        ps = jnp.sum(p, axis=1, keepdims=True)
        pv = jnp.dot(p.astype(BF16), v_ref[pl.ds(ks, n), :], preferred_element_type=F32)
        l = ps if l is None else l + ps
        acc = pv if acc is None else acc + pv
    if extra_logit is not None:
        l = l + jnp.exp2(extra_logit - m)
    return acc, l


def _key_chunks(start, total, size):
    out, a = [], 0
    while a < total:
        n = min(size, total - a)
        out.append((start + a, n, a))
        a += n
    return out


KEY_CHUNK = 512


def _diff_attn_kernel(lam_ref, q_ref, k_ref, v_ref, g_ref, o_ref, s_ref, *, n_lat, n_ctx, lam_init):
    tq = q_ref.shape[2]
    i = pl.program_id(2)
    lv = lam_ref[...]
    lam = (jnp.exp(jnp.sum(lv[0:1] * lv[1:2], axis=1, keepdims=True))
           - jnp.exp(jnp.sum(lv[2:3] * lv[3:4], axis=1, keepdims=True)) + lam_init)
    lane = lax.broadcasted_iota(jnp.int32, (tq, HEAD_DIM), 1)

    def run(chunks):
        for hh in range(2):
            q = q_ref[0, hh]
            zero = jnp.zeros_like(q)
            q2 = jnp.concatenate([jnp.where(lane < DH_B, q, zero), jnp.where(lane < DH_B, zero, q)], axis=0)
            acc, l = _softmax_pv(q2, k_ref.at[0, hh], v_ref.at[0, hh], s_ref, chunks)
            o = acc / l
            o = o[0:tq] - lam * o[tq:2 * tq]
            o = o * lax.rsqrt(jnp.mean(o * o, axis=1, keepdims=True) + EPS) * g_ref[...] * (1.0 - lam_init)
            o_ref[0, :, hh * HEAD_DIM:(hh + 1) * HEAD_DIM] = o.astype(o_ref.dtype)

    @pl.when(i * tq < n_lat)
    def _():
        run(_key_chunks(0, n_lat + n_ctx, KEY_CHUNK))

    @pl.when(i * tq >= n_lat)
    def _():
        run(_key_chunks(n_lat, n_ctx, KEY_CHUNK))


def _gqa_kernel(q_ref, k_ref, v_ref, o_ref, s_ref, *, n_lat, n_ctx):
    tq = q_ref.shape[2]
    i = pl.program_id(2)

    def run(chunks):
        q2 = jnp.concatenate([q_ref[0, 0], q_ref[0, 1]], axis=0)
        acc, l = _softmax_pv(q2, k_ref.at[0, 0], v_ref.at[0, 0], s_ref, chunks)
        o = acc / l
        o_ref[0, :, 0:HEAD_DIM] = o[0:tq].astype(o_ref.dtype)
        o_ref[0, :, HEAD_DIM:2 * HEAD_DIM] = o[tq:2 * tq].astype(o_ref.dtype)

    @pl.when(i * tq < n_lat)
    def _():
        run(_key_chunks(0, n_lat + n_ctx, KEY_CHUNK))

    @pl.when(i * tq >= n_lat)
    def _():
        run(_key_chunks(n_lat, n_ctx, KEY_CHUNK))


def _window_kernel(sink_ref, q_ref, k_ref, v_ref, o_ref, s_ref, *, n_lat, n_ctx, win_keys):
    tq = q_ref.shape[2]
    g = pl.program_id(1)
    i = pl.program_id(2)
    row = lax.broadcasted_iota(jnp.int32, (2 * tq, 1), 0)
    sink = jnp.where(row < tq, sink_ref[2 * g], sink_ref[2 * g + 1]) * LOG2E

    def finish(acc, l):
        o = acc / l
        o_ref[0, :, 0:HEAD_DIM] = o[0:tq].astype(o_ref.dtype)
        o_ref[0, :, HEAD_DIM:2 * HEAD_DIM] = o[tq:2 * tq].astype(o_ref.dtype)

    q2 = jnp.concatenate([q_ref[0, 0], q_ref[0, 1]], axis=0)

    @pl.when(i * tq < n_lat)
    def _():
        start = jnp.clip(i * tq - WINDOW, 0, n_lat - win_keys)
        start = pl.multiple_of(start, WINDOW)

        def mask(s, ks, n, col):
            if col < n_ctx:
                return s
            t = i * tq + lax.broadcasted_iota(jnp.int32, s.shape, 0) % tq
            sp = start + lax.broadcasted_iota(jnp.int32, s.shape, 1)
            return jnp.where(jnp.abs(t - sp) <= WINDOW, s, NEG_BIG)

        chunks = [(n_lat, n_ctx, 0), (start, win_keys, n_ctx)]
        acc, l = _softmax_pv(q2, k_ref.at[0, 0], v_ref.at[0, 0], s_ref, chunks, mask_fn=mask, extra_logit=sink)
        finish(acc, l)

    @pl.when(i * tq >= n_lat)
    def _():
        acc, l = _softmax_pv(q2, k_ref.at[0, 0], v_ref.at[0, 0], s_ref, [(n_lat, n_ctx, 0)], extra_logit=sink)
        finish(acc, l)


def _attn_specs(b_, s, tq, q_heads_per_step, kv_heads_per_step):
    q_spec = pl.BlockSpec((1, q_heads_per_step, tq, HEAD_DIM), lambda b, g, i: (b, g, i, 0))
    kv_spec = pl.BlockSpec((1, kv_heads_per_step, s, HEAD_DIM), lambda b, g, i: (b, g, 0, 0))
    o_spec = pl.BlockSpec((1, tq, 2 * HEAD_DIM), lambda b, g, i: (b, i, g))
    return q_spec, kv_spec, o_spec


def _diff_attn_call(q, k, v, lam_vecs, norm_g, lam_init, n_lat, n_ctx, n_q_tiles):
    b_, _, s, _ = q.shape
    tq = TOKEN_TILE
    q_spec, kv_spec, o_spec = _attn_specs(b_, s, tq, 2, 2)
    return pl.pallas_call(
        functools.partial(_diff_attn_kernel, n_lat=n_lat, n_ctx=n_ctx, lam_init=lam_init),
        grid=(b_, N_HEADS // 2, n_q_tiles),
        in_specs=[pl.BlockSpec((4, DH_B), lambda b, g, i: (0, 0)), q_spec, kv_spec, kv_spec,
                  pl.BlockSpec((1, HEAD_DIM), lambda b, g, i: (0, 0))],
        out_specs=o_spec,
        out_shape=jax.ShapeDtypeStruct((b_, n_q_tiles * tq, BRANCH_W), BF16),
        scratch_shapes=[pltpu.VMEM((2 * tq, s), F32)],
        compiler_params=_cparams(("parallel", "arbitrary", "arbitrary")),
        name="diff_attn",
    )(lam_vecs, q, k, v, norm_g)


def _gqa_call(q, k, v, n_lat, n_ctx, n_q_tiles):
    b_, _, s, _ = q.shape
    tq = TOKEN_TILE
    q_spec, kv_spec, o_spec = _attn_specs(b_, s, tq, 2, 1)
    return pl.pallas_call(
        functools.partial(_gqa_kernel, n_lat=n_lat, n_ctx=n_ctx),
        grid=(b_, N_HEADS // 2, n_q_tiles),
        in_specs=[q_spec, kv_spec, kv_spec],
        out_specs=o_spec,
        out_shape=jax.ShapeDtypeStruct((b_, n_q_tiles * tq, BRANCH_W), BF16),
        scratch_shapes=[pltpu.VMEM((2 * tq, s), F32)],
        compiler_params=_cparams(("parallel", "arbitrary", "arbitrary")),
        name="gqa_attn",
    )(q, k, v)


def _window_call(q, k, v, sink, n_lat, n_ctx, n_q_tiles):
    b_, _, s, _ = q.shape
    tq = TOKEN_TILE
    win_keys = min(tq + 2 * WINDOW, n_lat)
    q_spec, kv_spec, o_spec = _attn_specs(b_, s, tq, 2, 1)
    return pl.pallas_call(
        functools.partial(_window_kernel, n_lat=n_lat, n_ctx=n_ctx, win_keys=win_keys),
        grid=(b_, N_HEADS // 2, n_q_tiles),
        in_specs=[pl.BlockSpec(memory_space=pltpu.SMEM), q_spec, kv_spec, kv_spec],
        out_specs=o_spec,
        out_shape=jax.ShapeDtypeStruct((b_, n_q_tiles * tq, BRANCH_W), BF16),
        scratch_shapes=[pltpu.VMEM((2 * tq, n_ctx + win_keys), F32)],
        compiler_params=_cparams(("parallel", "arbitrary", "arbitrary")),
        name="window_attn",
    )(sink, q, k, v)


R_W, R_QG, R_UC, R_P, R_KDT, R_GL = 0, 64, 128, 192, 256, 320
R_TOTAL = 328


def _row_iota(shape):
    return lax.broadcasted_iota(jnp.int32, shape, 0)


def _chunk_scan(x, pos, forward):
    n = x.shape[0]
    d = 1
    while d < CHUNK:
        if forward:
            x = x + jnp.where(pos >= d, pltpu.roll(x, d, 0), 0.0)
        else:
            x = x + jnp.where(pos < CHUNK - d, pltpu.roll(x, n - d, 0), 0.0)
        d *= 2
    return x


def _gdn_prep_kernel(prev_ref, cur_ref, next_ref, ab_ref, convw_ref, alog_ref, dtb_ref,
                     q_ref, k_ref, v_ref, gcol_ref, grow_ref, *, n_lat_tiles):
    ts = cur_ref.shape[1]
    j = pl.program_id(1)
    has_prev = jnp.logical_and(j != 0, j != n_lat_tiles)
    has_next = jnp.logical_and(j != n_lat_tiles - 1, j < n_lat_tiles)
    cur = cur_ref[0]
    prev = jnp.where(has_prev, prev_ref[0], 0.0)
    nxt = jnp.where(has_next, next_ref[0], 0.0)
    rows = _row_iota(cur.shape)
    half = CONV_K // 2
    y = cur * convw_ref[half:half + 1, :]
    for d in range(1, half + 1):
        back = jnp.where(rows < d, pltpu.roll(prev, d, 0), pltpu.roll(cur, d, 0))
        fwd = jnp.where(rows >= ts - d, pltpu.roll(nxt, ts - d, 0), pltpu.roll(cur, ts - d, 0))
        y = y + back * convw_ref[half - d:half - d + 1, :] + fwd * convw_ref[half + d:half + d + 1, :]
    y = _silu(y)
    ones_bd = _block_diag_ones(BRANCH_W, HEAD_DIM)
    q = y[:, 0:BRANCH_W]
    k = y[:, BRANCH_W:2 * BRANCH_W]
    q = q * lax.rsqrt(_group_sum(q * q, ones_bd) + EPS) * (HEAD_DIM ** -0.5)
    k = k * lax.rsqrt(_group_sum(k * k, ones_bd) + EPS)
    _store_heads(q_ref, q, N_HEADS)
    _store_heads(k_ref, k, N_HEADS)
    _store_heads(v_ref, y[:, 2 * BRANCH_W:3 * BRANCH_W], N_HEADS)

    ab = ab_ref[0]
    lane = lax.broadcasted_iota(jnp.int32, ab.shape, 1)
    is_a = (lane % 8) < N_HEADS
    z = ab + dtb_ref[...]
    softplus = jnp.maximum(z, 0.0) + jnp.log(1.0 + jnp.exp(-jnp.abs(z)))
    lg = jnp.where(is_a, -jnp.exp(alog_ref[...]) * softplus, 0.0)
    pos = _row_iota(ab.shape) % CHUNK
    cum = jnp.where(lane < 8, _chunk_scan(lg, pos, True), _chunk_scan(lg, pos, False))
    gates = jnp.where(is_a, cum, jax.nn.sigmoid(ab))
    gcol_ref[0] = gates
    gt = jnp.transpose(gates)
    for c in range(ts // CHUNK):
        grow_ref[0, c] = gt[0:16, c * CHUNK:(c + 1) * CHUNK]


def _gdn_prep_call(aqkv, aab, conv_w, alog_vec, dtb_vec, n_lat_tiles):
    b_, s, _ = aqkv.shape
    ts = TOKEN_TILE
    nt = s // ts
    w3 = 3 * BRANCH_W
    head = pl.BlockSpec((1, N_HEADS, ts, HEAD_DIM), lambda b, j: (b, 0, j, 0))
    hshape = jax.ShapeDtypeStruct((b_, N_HEADS, s, HEAD_DIM), F32)
    cpt = ts // CHUNK
    return pl.pallas_call(
        functools.partial(_gdn_prep_kernel, n_lat_tiles=n_lat_tiles),
        grid=(b_, nt),
        in_specs=[pl.BlockSpec((1, ts, w3), lambda b, j: (b, jnp.maximum(j - 1, 0), 0)),
                  pl.BlockSpec((1, ts, w3), lambda b, j: (b, j, 0)),
                  pl.BlockSpec((1, ts, w3), lambda b, j: (b, jnp.minimum(j + 1, nt - 1), 0)),
                  pl.BlockSpec((1, ts, LANES), lambda b, j: (b, j, 0)),
                  pl.BlockSpec((8, w3), lambda b, j: (0, 0)),
                  pl.BlockSpec((1, LANES), lambda b, j: (0, 0)),
                  pl.BlockSpec((1, LANES), lambda b, j: (0, 0))],
        out_specs=[head, head, head,
                   pl.BlockSpec((1, ts, LANES), lambda b, j: (b, j, 0)),
                   pl.BlockSpec((1, cpt, 16, CHUNK), lambda b, j: (b, j, 0, 0))],
        out_shape=[hshape, hshape, hshape,
                   jax.ShapeDtypeStruct((b_, s, LANES), F32),
                   jax.ShapeDtypeStruct((b_, s // CHUNK, 16, CHUNK), F32)],
        compiler_params=_cparams(("parallel", "arbitrary")),
        name="gdn_prep",
    )(aqkv, aqkv, aqkv, aab, conv_w, alog_vec, dtb_vec)


def _gdn_local_kernel(q_ref, k_ref, v_ref, gcol_ref, grow_ref, fwd_ref, bwd_ref):
    c = CHUNK
    r = lax.broadcasted_iota(jnp.int32, (c, c), 0)
    s_ = lax.broadcasted_iota(jnp.int32, (c, c), 1)
    eye = jnp.where(r == s_, 1.0, 0.0)
    gcol = gcol_ref[0]
    grow = grow_ref[0, 0]
    for h in range(N_HEADS):
        q = q_ref[0, h]
        k = k_ref[0, h]
        v = v_ref[0, h]
        kk = _bdot_nt(k, k)
        qk = _bdot_nt(q, k)
        kt = _bdot_nt(eye, k)
        for direction, out_ref in ((0, fwd_ref), (1, bwd_ref)):
            base = 8 * direction
            g_c = gcol[:, base + h:base + h + 1]
            beta_c = gcol[:, base + 4 + h:base + 4 + h + 1]
            g_r = grow[base + h:base + h + 1, :]
            if direction == 0:
                incl, strict = r >= s_, r > s_
                g_end = g_c[c - 1:c, :]
            else:
                incl, strict = r <= s_, r < s_
                g_end = g_c[0:1, :]
            decay = jnp.exp(jnp.where(incl, g_c - g_r, -jnp.inf))
            a_mat = jnp.where(strict, beta_c * kk * decay, 0.0)
            pw = -a_mat
            t_inv = eye + pw
            for _ in range(5):
                pw = _bdot(pw, pw)
                t_inv = t_inv + _bdot(t_inv, pw)
            eg = jnp.exp(g_c)
            rhs = jnp.concatenate([beta_c * eg * k, beta_c * v], axis=1)
            sol = _bdot(t_inv, rhs)
            out_ref[0, h, 0, R_W:R_W + c, :] = sol[:, 0:c]
            out_ref[0, h, 0, R_UC:R_UC + c, :] = sol[:, c:2 * c]
            out_ref[0, h, 0, R_QG:R_QG + c, :] = q * eg
            out_ref[0, h, 0, R_P:R_P + c, :] = qk * decay
            out_ref[0, h, 0, R_KDT:R_KDT + c, :] = kt * jnp.exp(g_end - g_r)
            out_ref[0, h, 0, R_GL:R_GL + 8, :] = jnp.broadcast_to(jnp.exp(g_end), (8, c))


def _gdn_local_call(qn, kn, vn, gcol, grow):
    b_, nh, s, _ = qn.shape
    nch = s // CHUNK
    head = pl.BlockSpec((1, nh, CHUNK, HEAD_DIM), lambda b, c: (b, 0, c, 0))
    packed = pl.BlockSpec((1, nh, 1, R_TOTAL, CHUNK), lambda b, c: (b, 0, c, 0, 0))
    pshape = jax.ShapeDtypeStruct((b_, nh, nch, R_TOTAL, CHUNK), F32)
    return pl.pallas_call(
        _gdn_local_kernel,
        grid=(b_, nch),
        in_specs=[head, head, head,
                  pl.BlockSpec((1, CHUNK, LANES), lambda b, c: (b, c, 0)),
                  pl.BlockSpec((1, 1, 16, CHUNK), lambda b, c: (b, c, 0, 0))],
        out_specs=[packed, packed],
        out_shape=[pshape, pshape],
        compiler_params=_cparams(("parallel", "arbitrary")),
        name="gdn_local",
    )(qn, kn, vn, gcol, grow)


def _gdn_scan_kernel(fwd_ref, bwd_ref, of_ref, ob_ref, state_ref):
    c = CHUNK

    @pl.when(pl.program_id(1) == 0)
    def _():
        state_ref[...] = jnp.zeros_like(state_ref)

    for chain in range(2 * N_HEADS):
        src, dst = (fwd_ref, of_ref) if chain < N_HEADS else (bwd_ref, ob_ref)
        h = chain % N_HEADS
        st = state_ref[chain]
        ws = _bdot(src[0, h, 0, R_W:R_W + 2 * c, :], st)
        u = src[0, h, 0, R_UC:R_UC + c, :] - ws[0:c]
        dst[0, h] = ws[c:2 * c] + _bdot(src[0, h, 0, R_P:R_P + c, :], u)
        state_ref[chain] = (src[0, h, 0, R_GL:R_GL + 1, :] * st
                            + _bdot(src[0, h, 0, R_KDT:R_KDT + c, :], u))


def _gdn_scan_call(fwd, bwd, n_lat_chunks, n_ctx_chunks):
    b_, nh, nch, _, _ = fwd.shape
    s = nch * CHUNK
    nlc, ncc = n_lat_chunks, n_ctx_chunks

    def order_f(i):
        return jnp.where(i < ncc, nlc + i, i - ncc)

    def order_b(i):
        return jnp.where(i < ncc, nlc + ncc - 1 - i, nlc - 1 - (i - ncc))

    pk = lambda order: pl.BlockSpec((1, nh, 1, R_TOTAL, CHUNK), lambda b, i: (b, 0, order(i), 0, 0))
    oo = lambda order: pl.BlockSpec((1, nh, CHUNK, HEAD_DIM), lambda b, i: (b, 0, order(i), 0))
    oshape = jax.ShapeDtypeStruct((b_, nh, s, HEAD_DIM), F32)
    return pl.pallas_call(
        _gdn_scan_kernel,
        grid=(b_, nch),
        in_specs=[pk(order_f), pk(order_b)],
        out_specs=[oo(order_f), oo(order_b)],
        out_shape=[oshape, oshape],
        scratch_shapes=[pltpu.VMEM((2 * nh, HEAD_DIM, HEAD_DIM), F32)],
        compiler_params=_cparams(("parallel", "arbitrary")),
        name="gdn_scan",
    )(fwd, bwd)


def _gdn_finish_kernel(of_ref, ob_ref, gate_ref, g_ref, o_ref):
    gate = gate_ref[0].astype(F32)
    for h in range(N_HEADS):
        o = of_ref[0, h] + ob_ref[0, h]
        o = o * lax.rsqrt(jnp.mean(o * o, axis=1, keepdims=True) + EPS) * g_ref[...]
        o = o * _silu(gate[:, h * HEAD_DIM:(h + 1) * HEAD_DIM])
        o_ref[0, :, h * HEAD_DIM:(h + 1) * HEAD_DIM] = o.astype(o_ref.dtype)


def _gdn_finish_call(o_f, o_b, agate, norm_g):
    b_, nh, s, _ = o_f.shape
    ts = TOKEN_TILE
    head = pl.BlockSpec((1, nh, ts, HEAD_DIM), lambda b, j: (b, 0, j, 0))
    tok = pl.BlockSpec((1, ts, BRANCH_W), lambda b, j: (b, j, 0))
    return pl.pallas_call(
        _gdn_finish_kernel,
        grid=(b_, s // ts),
        in_specs=[head, head, tok, pl.BlockSpec((1, HEAD_DIM), lambda b, j: (0, 0))],
        out_specs=tok,
        out_shape=jax.ShapeDtypeStruct((b_, s, BRANCH_W), BF16),
        compiler_params=_cparams(("parallel", "arbitrary")),
        name="gdn_finish",
    )(o_f, o_b, agate, norm_g)


def _merge_kernel(oa_ref, ob_ref, oc_ref, od_ref, gates_ref, x_ref, mod_ref, wbr_ref, wout_ref,
                  lng_ref, lnb_ref, wr_ref, x1_ref, h2_ref, aff_ref, *, alpha):
    d = D_MODEL
    m = None
    for i, o_ref in enumerate((oa_ref, ob_ref, oc_ref, od_ref)):
        gate = jax.nn.sigmoid(gates_ref[0, :, i * d:(i + 1) * d].astype(F32))
        term = gate * jnp.dot(o_ref[0], wbr_ref[i], preferred_element_type=F32)
        m = term if m is None else m + term
    y = _bdot(m, wout_ref[...])
    g1 = mod_ref[0, 0, :, 2 * d:3 * d]
    sh2 = mod_ref[0, 0, :, 3 * d:4 * d]
    sc2 = mod_ref[0, 0, :, 4 * d:5 * d]
    x1 = _layer_norm(alpha * x_ref[0] + g1 * y, lng_ref[...], lnb_ref[...])
    x1_ref[0] = x1
    h2 = x1 * (1.0 + sc2) + sh2
    h2_ref[0] = h2
    logits = _bdot(h2, wr_ref[...])
    lane = lax.broadcasted_iota(jnp.int32, logits.shape, 1)
    logits = jnp.where(lane < N_EXPERTS, logits, NEG_BIG)
    e = jnp.exp(logits - jnp.max(logits, axis=1, keepdims=True))
    aff_ref[0] = e / jnp.sum(e, axis=1, keepdims=True)


def _merge_call(outs, gates, x_all, modv, w_br, w_out, ln_g, ln_b, w_router, alpha, n_lat_tiles, n_tiles):
    b_, s, d = x_all.shape
    ts = TOKEN_TILE
    tok = lambda w: pl.BlockSpec((1, ts, w), lambda b, j: (b, j, 0))
    const2 = lambda shape: pl.BlockSpec(shape, lambda b, j: (0,) * len(shape))
    return pl.pallas_call(
        functools.partial(_merge_kernel, alpha=alpha),
        grid=(b_, n_tiles),
        in_specs=[tok(BRANCH_W)] * 4 + [
            tok(GATE_W), tok(d),
            pl.BlockSpec((1, 1, 1, 6 * d), lambda b, j: (b, j // n_lat_tiles, 0, 0)),
            const2((N_BRANCH, BRANCH_W, d)), const2((d, d)), const2((1, d)), const2((1, d)),
            const2((d, LANES))],
        out_specs=[tok(d), tok(d), tok(LANES)],
        out_shape=[jax.ShapeDtypeStruct((b_, n_tiles * ts, d), F32),
                   jax.ShapeDtypeStruct((b_, n_tiles * ts, d), F32),
                   jax.ShapeDtypeStruct((b_, n_tiles * ts, LANES), F32)],
        compiler_params=_cparams(("parallel", "arbitrary")),
        name="merge_ln1_router",
    )(*outs, gates, x_all, modv, w_br, w_out, ln_g, ln_b, w_router)


def _moe_kernel(idx_ref, gval_ref, h_ref, wg_ref, wu_ref, wd_ref, out_hbm, xg_ref, y_ref, acc_ref, sem,
                *, cap, slots):
    e = pl.program_id(1)
    nb = h_ref.shape[0]

    @pl.when(e == 0)
    def _():
        acc_ref[...] = jnp.zeros_like(acc_ref)

    def gather(j, carry):
        xg_ref[pl.ds(j, 1), :] = h_ref[j // cap, pl.ds(idx_ref[0, 0, 0, j], 1), :]
        return carry

    lax.fori_loop(0, slots, gather, 0)
    xg = xg_ref[...].astype(BF16)
    a = jnp.dot(xg, wg_ref[0], preferred_element_type=F32)
    u = jnp.dot(xg, wu_ref[0], preferred_element_type=F32)
    y_ref[...] = _bdot(_silu(a) * u, wd_ref[0])

    def scatter(j, carry):
        bb = j // cap
        r = idx_ref[0, 0, 0, j]
        acc_ref[bb, pl.ds(r, 1), :] = (acc_ref[bb, pl.ds(r, 1), :]
                                       + gval_ref[0, 0, 0, j] * y_ref[pl.ds(j, 1), :])
        return carry

    lax.fori_loop(0, slots, scatter, 0)

    @pl.when(e == pl.num_programs(1) - 1)
    def _():
        cp = pltpu.make_async_copy(acc_ref, out_hbm.at[pl.ds(pl.program_id(0) * nb, nb)], sem)
        cp.start()
        cp.wait()


def _moe_call(idx, gval, h2, wg, wu, wd, row_block, n_rows, group):
    g_, n_e, _, slots = idx.shape
    b_, s, d = h2.shape
    f = wg.shape[2]
    cap = slots // group
    smem = lambda: pl.BlockSpec((1, 1, 1, slots), lambda g, e: (g, e, 0, 0), memory_space=pltpu.SMEM)
    return pl.pallas_call(
        functools.partial(_moe_kernel, cap=cap, slots=slots),
        grid=(g_, n_e),
        in_specs=[smem(), smem(),
                  pl.BlockSpec((group, n_rows, d), lambda g, e: (g, row_block, 0), pipeline_mode=pl.Buffered(1)),
                  pl.BlockSpec((1, d, f), lambda g, e: (e, 0, 0)),
                  pl.BlockSpec((1, d, f), lambda g, e: (e, 0, 0)),
                  pl.BlockSpec((1, f, d), lambda g, e: (e, 0, 0))],
        out_specs=pl.BlockSpec(memory_space=pl.ANY),
        out_shape=jax.ShapeDtypeStruct((b_, n_rows, d), F32),
        scratch_shapes=[pltpu.VMEM((slots, d), F32), pltpu.VMEM((slots, d), F32),
                        pltpu.VMEM((group, n_rows, d), F32), pltpu.SemaphoreType.DMA(())],
        compiler_params=_cparams(("arbitrary", "arbitrary"), vmem=60 * 1024 * 1024),
        name="moe_experts",
    )(idx, gval, h2, wg, wu, wd)


def _route(aff, n_lat, n_ctx, with_ctx):
    b_ = aff.shape[0]

    def pick(a, n):
        cap = max(1, CAPACITY_FACTOR * n // N_EXPERTS)
        gval, idx = lax.top_k(jnp.swapaxes(a, 1, 2), cap)
        return gval, idx.astype(jnp.int32), cap

    gl, il, cap_l = pick(aff[:, :n_lat, :N_EXPERTS], n_lat)
    lat = (il.reshape(b_, N_EXPERTS, 1, cap_l), gl.reshape(b_, N_EXPERTS, 1, cap_l))
    if not with_ctx:
        return lat, None
    gc, ic, cap_c = pick(aff[:, n_lat:, :N_EXPERTS], n_ctx)
    flat = lambda t: jnp.transpose(t, (1, 0, 2)).reshape(1, N_EXPERTS, 1, b_ * cap_c)
    return lat, (flat(ic), flat(gc))


def _ln2_kernel(x1_ref, fl_ref, fc_ref, mod_ref, g_ref, b_ref, o_ref, *, alpha, n_lat_tiles):
    d = D_MODEL
    j = pl.program_id(1)
    f = jnp.where(j < n_lat_tiles, fl_ref[0], fc_ref[0])
    g2 = mod_ref[0, 0, :, 5 * d:6 * d]
    o_ref[0] = _layer_norm(alpha * x1_ref[0] + g2 * f, g_ref[...], b_ref[...])


def _ln2_call(x1, f_lat, f_ctx, modv, ln_g, ln_b, alpha, n_lat_tiles, n_tiles):
    b_, s, d = x1.shape
    ts = TOKEN_TILE
    return pl.pallas_call(
        functools.partial(_ln2_kernel, alpha=alpha, n_lat_tiles=n_lat_tiles),
        grid=(b_, n_tiles),
        in_specs=[pl.BlockSpec((1, ts, d), lambda b, j: (b, j, 0)),
                  pl.BlockSpec((1, ts, d), lambda b, j: (b, jnp.minimum(j, n_lat_tiles - 1), 0)),
                  pl.BlockSpec((1, ts, d), lambda b, j: (b, jnp.maximum(j - n_lat_tiles, 0), 0)),
                  pl.BlockSpec((1, 1, 1, 6 * d), lambda b, j: (b, j // n_lat_tiles, 0, 0)),
                  pl.BlockSpec((1, d), lambda b, j: (0, 0)),
                  pl.BlockSpec((1, d), lambda b, j: (0, 0))],
        out_specs=pl.BlockSpec((1, ts, d), lambda b, j: (b, j, 0)),
        out_shape=jax.ShapeDtypeStruct((b_, n_tiles * ts, d), F32),
        compiler_params=_cparams(("parallel", "arbitrary")),
        name="ln2",
    )(x1, f_lat, f_ctx, modv, ln_g, ln_b)


def _rope_tables(n_lat, n_ctx, dim):
    nf = dim // 4
    t = jnp.arange(n_lat)
    row = (t // GRID_W).astype(F32)
    col = (t % GRID_W).astype(F32)
    inv = ROPE_THETA ** (-jnp.arange(nf, dtype=F32) / nf)
    ar = row[:, None] * inv
    ac = col[:, None] * inv
    cos = jnp.concatenate([jnp.cos(ar), jnp.cos(ar), jnp.cos(ac), jnp.cos(ac)], axis=1)
    sin = jnp.concatenate([-jnp.sin(ar), jnp.sin(ar), -jnp.sin(ac), jnp.sin(ac)], axis=1)
    reps = BRANCH_W // dim
    cos = jnp.concatenate([jnp.tile(cos, (1, reps)), jnp.ones((n_ctx, BRANCH_W), F32)], axis=0)
    sin = jnp.concatenate([jnp.tile(sin, (1, reps)), jnp.zeros((n_ctx, BRANCH_W), F32)], axis=0)
    return cos, sin


def _reorder_w_in(w_in):
    o_ab = 3 * BRANCH_W
    o_gate = o_ab + 4 * N_HEADS
    o_b = o_gate + BRANCH_W
    o_gates = o_b + 3 * BRANCH_W + 4 * BRANCH_W
    pad = jnp.zeros(w_in.shape[:2] + (LANES - 4 * N_HEADS,), w_in.dtype)
    return jnp.concatenate([w_in[..., o_gates:], w_in[..., :o_ab], w_in[..., o_ab:o_gate], pad,
                            w_in[..., o_gate:o_b], w_in[..., o_b:o_gates]], axis=-1).astype(BF16)


def _lane_vec(p):
    v = jnp.zeros((LANES,), F32)
    v = v.at[0:N_HEADS].set(p[0]).at[8:8 + N_HEADS].set(p[1])
    return v.reshape(1, LANES)


def kernel(x, c, ctx, c_ctx, w_mod, b_mod, w_in, conv_a, a_log, dt_bias, gdn_norm, diff_lambda, diff_norm,
           qk_norm_c, sink_d, w_br, w_out, ln1_g, ln1_b, w_router, w_gate_e, w_up_e, w_down_e, ln2_g, ln2_b):
    b_, n_lat, d = x.shape
    n_ctx = ctx.shape[1]
    depth = w_mod.shape[0]
    ts = TOKEN_TILE
    assert d == D_MODEL and n_lat % ts == 0 and n_ctx == ts and n_lat % GRID_W == 0
    assert n_lat >= ts + 2 * WINDOW or n_lat == ts
    n_lat_tiles = n_lat // ts
    n_tiles = n_lat_tiles + n_ctx // ts
    alpha = (2.0 * depth) ** 0.25

    rows = -(-(b_ + 1) // 8) * 8
    cc = jnp.zeros((rows, d), F32).at[:b_].set(c).at[b_].set(c_ctx)
    mod_all = _mod_call(cc, w_mod, b_mod)

    w_all = _reorder_w_in(w_in)
    w_br_b = w_br.astype(BF16)
    w_out_b = w_out.astype(BF16)
    w_r = jnp.concatenate([w_router, jnp.zeros((depth, d, LANES - N_EXPERTS), F32)], axis=-1).astype(BF16)
    wg_b, wu_b, wd_b = w_gate_e.astype(BF16), w_up_e.astype(BF16), w_down_e.astype(BF16)
    conv_w = jnp.concatenate([conv_a, jnp.zeros((depth, 8 - CONV_K, conv_a.shape[2]), F32)], axis=1)
    tabs = _rope_tables(n_lat, n_ctx, HEAD_DIM) + _rope_tables(n_lat, n_ctx, DH_B)

    x_all = jnp.concatenate([x, ctx], axis=1)
    for l in range(depth):
        last = l == depth - 1
        lam_init = 0.8 - 0.6 * math.exp(-0.3 * l)
        modv = jnp.stack([mod_all[l, :b_], jnp.broadcast_to(mod_all[l, b_], (b_, 6 * d))], axis=1)
        modv = modv.reshape(b_, 2, 1, 6 * d)
        gq = jnp.tile(qk_norm_c[l, 0], N_HEADS).reshape(1, BRANCH_W)
        gk = jnp.tile(qk_norm_c[l, 1], N_HEADS // 2).reshape(1, BRANCH_W // 2)

        (gates, aqkv, aab, agate, qb, kb, vb, qc, kc, vc, qd, kd, vd) = _inproj_call(
            x_all, modv, w_all[l], tabs, gq, gk, n_lat_tiles)

        qn, kn, vn, gcol, grow = _gdn_prep_call(aqkv, aab, conv_w[l], _lane_vec(a_log[l]), _lane_vec(dt_bias[l]),
                                                n_lat_tiles)
        pk_f, pk_b = _gdn_local_call(qn, kn, vn, gcol, grow)
        o_f, o_b = _gdn_scan_call(pk_f, pk_b, n_lat // CHUNK, n_ctx // CHUNK)
        out_a = _gdn_finish_call(o_f, o_b, agate, gdn_norm[l].reshape(1, HEAD_DIM))

        nq = n_lat_tiles if last else n_tiles
        out_b = _diff_attn_call(qb, kb, vb, diff_lambda[l], diff_norm[l].reshape(1, HEAD_DIM), lam_init,
                                n_lat, n_ctx, nq)
        out_c = _gqa_call(qc, kc, vc, n_lat, n_ctx, nq)
        out_d = _window_call(qd, kd, vd, sink_d[l], n_lat, n_ctx, nq)

        x1, h2, aff = _merge_call((out_a, out_b, out_c, out_d), gates, x_all, modv, w_br_b[l], w_out_b[l],
                                  ln1_g[l].reshape(1, d), ln1_b[l].reshape(1, d), w_r[l], alpha,
                                  n_lat_tiles, nq)

        route_lat, route_ctx = _route(aff, n_lat, n_ctx, not last)
        f_lat = _moe_call(route_lat[0], route_lat[1], h2, wg_b[l], wu_b[l], wd_b[l], 0, n_lat, 1)
        if last:
            f_ctx = f_lat
        else:
            f_ctx = _moe_call(route_ctx[0], route_ctx[1], h2, wg_b[l], wu_b[l], wd_b[l],
                              n_lat // n_ctx, n_ctx, b_)
        x_all = _ln2_call(x1, f_lat, f_ctx, modv, ln2_g[l].reshape(1, d), ln2_b[l].reshape(1, d), alpha,
                          n_lat_tiles, nq)
    return x_all[:, :n_lat]
```

```python
import functools
import math

import jax
import jax.numpy as jnp
from jax import lax
from jax.experimental import pallas as pl
from jax.experimental.pallas import tpu as pltpu

F32 = jnp.float32
BF16 = jnp.bfloat16

D_MODEL = 1024
HEAD_DIM = 64
N_HEADS = 4
BRANCH_W = 256
N_BRANCH = 4
GRID_W = 64
ROPE_THETA = 10000.0
EPS = 1e-6
CONV_K = 5
CHUNK = 64
DH_B = 32
WINDOW = 128
N_EXPERTS = 16
CAPACITY_FACTOR = 2
LOG2E = 1.4426950408889634

TOKEN_TILE = 256
LANES = 128
VMEM_LIMIT = 56 * 1024 * 1024

GATE_W = N_BRANCH * D_MODEL
C_GATES = 0
C_AQKV = C_GATES + GATE_W
C_AAB = C_AQKV + 3 * BRANCH_W
C_AGATE = C_AAB + LANES
C_B = C_AGATE + BRANCH_W
C_C = C_B + 3 * BRANCH_W
C_D = C_C + 2 * BRANCH_W
W_ALL = C_D + 2 * BRANCH_W


def _cparams(sem, vmem=VMEM_LIMIT):
    return pltpu.CompilerParams(dimension_semantics=sem, vmem_limit_bytes=vmem)


def _bdot(a, b):
    return jnp.dot(a.astype(BF16), b.astype(BF16), preferred_element_type=F32)


def _bdot_nt(a, b):
    return lax.dot_general(a.astype(BF16), b.astype(BF16), (((1,), (1,)), ((), ())),
                           preferred_element_type=F32)


def _silu(x):
    return x * jax.nn.sigmoid(x)


def _block_diag_ones(width, group):
    r = lax.broadcasted_iota(jnp.int32, (width, width), 0) // group
    c = lax.broadcasted_iota(jnp.int32, (width, width), 1) // group
    return jnp.where(r == c, 1.0, 0.0).astype(BF16)


def _group_sum(x, ones_bd):
    hi = x.astype(BF16)
    lo = (x - hi.astype(F32)).astype(BF16)
    return (jnp.dot(hi, ones_bd, preferred_element_type=F32)
            + jnp.dot(lo, ones_bd, preferred_element_type=F32))


def _pair_swap(x, q):
    w = x.shape[-1]
    lane = lax.broadcasted_iota(jnp.int32, x.shape, x.ndim - 1)
    fwd = pltpu.roll(x, w - q, x.ndim - 1)
    bwd = pltpu.roll(x, q, x.ndim - 1)
    return jnp.where((lane % (2 * q)) < q, fwd, bwd)


def _layer_norm(x, g, b):
    mu = jnp.mean(x, axis=-1, keepdims=True)
    xc = x - mu
    var = jnp.mean(xc * xc, axis=-1, keepdims=True)
    return xc * lax.rsqrt(var + EPS) * g + b


def _mod_kernel(c_ref, w_ref, b_ref, o_ref):
    o_ref[0] = _bdot(_silu(c_ref[...]), w_ref[0]) + b_ref[0]


def _mod_call(cc, w_mod, b_mod):
    depth, d, n6 = w_mod.shape
    rows = cc.shape[0]
    nb = 1536
    return pl.pallas_call(
        _mod_kernel,
        grid=(depth, n6 // nb),
        in_specs=[pl.BlockSpec((rows, d), lambda l, j: (0, 0)),
                  pl.BlockSpec((1, d, nb), lambda l, j: (l, 0, j)),
                  pl.BlockSpec((1, 1, nb), lambda l, j: (l, 0, j))],
        out_specs=pl.BlockSpec((1, rows, nb), lambda l, j: (l, 0, j)),
        out_shape=jax.ShapeDtypeStruct((depth, rows, n6), F32),
        compiler_params=_cparams(("arbitrary", "arbitrary")),
        name="mod_vectors",
    )(cc, w_mod, b_mod.reshape(depth, 1, n6))


def _rope(x, cos, sin, quarter):
    return x * cos + _pair_swap(x, quarter) * sin


def _store_heads(ref, x, n_heads):
    for h in range(n_heads):
        ref[0, h] = x[:, h * HEAD_DIM:(h + 1) * HEAD_DIM].astype(ref.dtype)


def _inproj_kernel(x_ref, mod_ref, w_ref, cos_hd_ref, sin_hd_ref, cos_b_ref, sin_b_ref, gq_ref, gk_ref,
                   gates_ref, aqkv_ref, aab_ref, agate_ref,
                   qb_ref, kb_ref, vb_ref, qc_ref, kc_ref, vc_ref, qd_ref, kd_ref, vd_ref):
    d = D_MODEL
    x = x_ref[0]
    sh1 = mod_ref[0, 0, :, 0:d]
    sc1 = mod_ref[0, 0, :, d:2 * d]
    h = (x * (1.0 + sc1) + sh1).astype(BF16)

    def proj(a, width):
        return jnp.dot(h, w_ref[:, a:a + width], preferred_element_type=F32)

    gates_ref[0] = proj(C_GATES, GATE_W).astype(gates_ref.dtype)
    aqkv_ref[0] = proj(C_AQKV, 3 * BRANCH_W)
    aab_ref[0] = proj(C_AAB, LANES)
    agate_ref[0] = proj(C_AGATE, BRANCH_W).astype(agate_ref.dtype)

    cos_hd, sin_hd = cos_hd_ref[...], sin_hd_ref[...]
    cos_b, sin_b = cos_b_ref[...], sin_b_ref[...]
    kv_w = BRANCH_W // 2

    pb = proj(C_B, 3 * BRANCH_W)
    qb = _rope(pb[:, 0:BRANCH_W], cos_b, sin_b, DH_B // 4) * (DH_B ** -0.5 * LOG2E)
    kb = _rope(pb[:, BRANCH_W:2 * BRANCH_W], cos_b, sin_b, DH_B // 4)
    _store_heads(qb_ref, qb, N_HEADS)
    _store_heads(kb_ref, kb, N_HEADS)
    _store_heads(vb_ref, pb[:, 2 * BRANCH_W:3 * BRANCH_W], N_HEADS)

    pc = proj(C_C, 2 * BRANCH_W)
    ones_q = _block_diag_ones(BRANCH_W, HEAD_DIM)
    ones_k = _block_diag_ones(kv_w, HEAD_DIM)
    qc = pc[:, 0:BRANCH_W]
    kc = pc[:, BRANCH_W:BRANCH_W + kv_w]
    qc = qc * lax.rsqrt(_group_sum(qc * qc, ones_q) * (1.0 / HEAD_DIM) + EPS) * gq_ref[...]
    kc = kc * lax.rsqrt(_group_sum(kc * kc, ones_k) * (1.0 / HEAD_DIM) + EPS) * gk_ref[...]
    qc = _rope(qc, cos_hd, sin_hd, HEAD_DIM // 4) * (HEAD_DIM ** -0.5 * LOG2E)
    kc = _rope(kc, cos_hd[:, 0:kv_w], sin_hd[:, 0:kv_w], HEAD_DIM // 4)
    _store_heads(qc_ref, qc, N_HEADS)
    _store_heads(kc_ref, kc, N_HEADS // 2)
    _store_heads(vc_ref, pc[:, BRANCH_W + kv_w:2 * BRANCH_W], N_HEADS // 2)

    pd = proj(C_D, 2 * BRANCH_W)
    qd = _rope(pd[:, 0:BRANCH_W], cos_hd, sin_hd, HEAD_DIM // 4) * (HEAD_DIM ** -0.5 * LOG2E)
    kd = _rope(pd[:, BRANCH_W:BRANCH_W + kv_w], cos_hd[:, 0:kv_w], sin_hd[:, 0:kv_w], HEAD_DIM // 4)
    _store_heads(qd_ref, qd, N_HEADS)
    _store_heads(kd_ref, kd, N_HEADS // 2)
    _store_heads(vd_ref, pd[:, BRANCH_W + kv_w:2 * BRANCH_W], N_HEADS // 2)


def _inproj_call(x_all, modv, w_all, tabs, gq, gk, n_lat_tiles):
    b_, s, d = x_all.shape
    ts = TOKEN_TILE
    nt = s // ts
    tok = lambda w: pl.BlockSpec((1, ts, w), lambda b, j: (b, j, 0))
    tab = pl.BlockSpec((ts, BRANCH_W), lambda b, j: (j, 0))
    head = lambda nh: pl.BlockSpec((1, nh, ts, HEAD_DIM), lambda b, j: (b, 0, j, 0))
    hshape = lambda nh: jax.ShapeDtypeStruct((b_, nh, s, HEAD_DIM), BF16)
    kvh = N_HEADS // 2
    return pl.pallas_call(
        _inproj_kernel,
        grid=(b_, nt),
        in_specs=[tok(d),
                  pl.BlockSpec((1, 1, 1, 6 * d), lambda b, j: (b, j // n_lat_tiles, 0, 0)),
                  pl.BlockSpec((d, W_ALL), lambda b, j: (0, 0), pipeline_mode=pl.Buffered(1)),
                  tab, tab, tab, tab,
                  pl.BlockSpec((1, BRANCH_W), lambda b, j: (0, 0)),
                  pl.BlockSpec((1, BRANCH_W // 2), lambda b, j: (0, 0))],
        out_specs=[tok(GATE_W), tok(3 * BRANCH_W), tok(LANES), tok(BRANCH_W),
                   head(N_HEADS), head(N_HEADS), head(N_HEADS),
                   head(N_HEADS), head(kvh), head(kvh),
                   head(N_HEADS), head(kvh), head(kvh)],
        out_shape=[jax.ShapeDtypeStruct((b_, s, GATE_W), BF16),
                   jax.ShapeDtypeStruct((b_, s, 3 * BRANCH_W), F32),
                   jax.ShapeDtypeStruct((b_, s, LANES), F32),
                   jax.ShapeDtypeStruct((b_, s, BRANCH_W), BF16),
                   hshape(N_HEADS), hshape(N_HEADS), hshape(N_HEADS),
                   hshape(N_HEADS), hshape(kvh), hshape(kvh),
                   hshape(N_HEADS), hshape(kvh), hshape(kvh)],
        compiler_params=_cparams(("parallel", "arbitrary")),
        name="inproj",
    )(x_all, modv, w_all, *tabs, gq, gk)


NEG_BIG = -1e30


def _softmax_pv(q2, k_ref, v_ref, s_ref, chunks, mask_fn=None, extra_logit=None):
    m = None
    for ks, n, col in chunks:
        s = _bdot_nt(q2, k_ref[pl.ds(ks, n), :])
        if mask_fn is not None:
            s = mask_fn(s, ks, n, col)
        s_ref[:, col:col + n] = s
        cm = jnp.max(s, axis=1, keepdims=True)
        m = cm if m is None else jnp.maximum(m, cm)
    if extra_logit is not None:
        m = jnp.maximum(m, extra_logit)
    l = None
    acc = None
    for ks, n, col in chunks:
        p = jnp.exp2(s_ref[:, col:col + n] - m)
        ps = jnp.sum(p, axis=1, keepdims=True)
        pv = jnp.dot(p.astype(BF16), v_ref[pl.ds(ks, n), :], preferred_element_type=F32)
        l = ps if l is None else l + ps
        acc = pv if acc is None else acc + pv
    if extra_logit is not None:
        l = l + jnp.exp2(extra_logit - m)
    return acc, l


def _key_chunks(start, total, size):
    out, a = [], 0
    while a < total:
        n = min(size, total - a)
        out.append((start + a, n, a))
        a += n
    return out


KEY_CHUNK = 512


def _diff_attn_kernel(lam_ref, q_ref, k_ref, v_ref, g_ref, o_ref, s_ref, *, n_lat, n_ctx, lam_init):
    tq = q_ref.shape[2]
    i = pl.program_id(2)
    lv = lam_ref[...]
    lam = (jnp.exp(jnp.sum(lv[0:1] * lv[1:2], axis=1, keepdims=True))
           - jnp.exp(jnp.sum(lv[2:3] * lv[3:4], axis=1, keepdims=True)) + lam_init)
    lane = lax.broadcasted_iota(jnp.int32, (tq, HEAD_DIM), 1)

    def run(chunks):
        for hh in range(2):
            q = q_ref[0, hh]
            zero = jnp.zeros_like(q)
            q2 = jnp.concatenate([jnp.where(lane < DH_B, q, zero), jnp.where(lane < DH_B, zero, q)], axis=0)
            acc, l = _softmax_pv(q2, k_ref.at[0, hh], v_ref.at[0, hh], s_ref.at[hh], chunks)
            o = acc / l
            o = o[0:tq] - lam * o[tq:2 * tq]
            o = o * lax.rsqrt(jnp.mean(o * o, axis=1, keepdims=True) + EPS) * g_ref[...] * (1.0 - lam_init)
            o_ref[0, :, hh * HEAD_DIM:(hh + 1) * HEAD_DIM] = o.astype(o_ref.dtype)

    @pl.when(i * tq < n_lat)
    def _():
        run(_key_chunks(0, n_lat + n_ctx, KEY_CHUNK))

    @pl.when(i * tq >= n_lat)
    def _():
        run(_key_chunks(n_lat, n_ctx, KEY_CHUNK))


def _gqa_kernel(q_ref, k_ref, v_ref, o_ref, s_ref, *, n_lat, n_ctx):
    tq = q_ref.shape[2]
    i = pl.program_id(2)

    def run(chunks):
        for g in range(N_HEADS // 2):
            q2 = jnp.concatenate([q_ref[0, 2 * g], q_ref[0, 2 * g + 1]], axis=0)
            acc, l = _softmax_pv(q2, k_ref.at[0, g], v_ref.at[0, g], s_ref.at[g], chunks)
            o = acc / l
            for j in range(2):
                c0 = (2 * g + j) * HEAD_DIM
                o_ref[0, :, c0:c0 + HEAD_DIM] = o[j * tq:(j + 1) * tq].astype(o_ref.dtype)

    @pl.when(i * tq < n_lat)
    def _():
        run(_key_chunks(0, n_lat + n_ctx, KEY_CHUNK))

    @pl.when(i * tq >= n_lat)
    def _():
        run(_key_chunks(n_lat, n_ctx, KEY_CHUNK))


def _window_kernel(sink_ref, q_ref, k_ref, v_ref, o_ref, s_ref, *, n_lat, n_ctx, win_keys):
    tq = q_ref.shape[2]
    i = pl.program_id(2)
    row = lax.broadcasted_iota(jnp.int32, (2 * tq, 1), 0)

    def head_pair(g, chunks, mask):
        sink = jnp.where(row < tq, sink_ref[2 * g], sink_ref[2 * g + 1]) * LOG2E
        q2 = jnp.concatenate([q_ref[0, 2 * g], q_ref[0, 2 * g + 1]], axis=0)
        acc, l = _softmax_pv(q2, k_ref.at[0, g], v_ref.at[0, g], s_ref.at[g], chunks, mask_fn=mask,
                             extra_logit=sink)
        o = acc / l
        for j in range(2):
            c0 = (2 * g + j) * HEAD_DIM
            o_ref[0, :, c0:c0 + HEAD_DIM] = o[j * tq:(j + 1) * tq].astype(o_ref.dtype)

    @pl.when(i * tq < n_lat)
    def _():
        start = jnp.clip(i * tq - WINDOW, 0, n_lat - win_keys)
        start = pl.multiple_of(start, WINDOW)

        def mask(s, ks, n, col):
            if col < n_ctx:
                return s
            t = i * tq + lax.broadcasted_iota(jnp.int32, s.shape, 0) % tq
            sp = start + lax.broadcasted_iota(jnp.int32, s.shape, 1)
            return jnp.where(jnp.abs(t - sp) <= WINDOW, s, NEG_BIG)

        for g in range(N_HEADS // 2):
            head_pair(g, [(n_lat, n_ctx, 0), (start, win_keys, n_ctx)], mask)

    @pl.when(i * tq >= n_lat)
    def _():
        for g in range(N_HEADS // 2):
            head_pair(g, [(n_lat, n_ctx, 0)], None)


def _attn_specs(s, tq, q_heads, kv_heads):
    q_spec = pl.BlockSpec((1, q_heads, tq, HEAD_DIM), lambda b, g, i: (b, g, i, 0))
    kv_spec = pl.BlockSpec((1, kv_heads, s, HEAD_DIM), lambda b, g, i: (b, g, 0, 0))
    return q_spec, kv_spec


def _diff_attn_call(q, k, v, lam_vecs, norm_g, lam_init, n_lat, n_ctx, n_q_tiles):
    b_, _, s, _ = q.shape
    tq = TOKEN_TILE
    q_spec, kv_spec = _attn_specs(s, tq, 2, 2)
    return pl.pallas_call(
        functools.partial(_diff_attn_kernel, n_lat=n_lat, n_ctx=n_ctx, lam_init=lam_init),
        grid=(b_, N_HEADS // 2, n_q_tiles),
        in_specs=[pl.BlockSpec((4, DH_B), lambda b, g, i: (0, 0)), q_spec, kv_spec, kv_spec,
                  pl.BlockSpec((1, HEAD_DIM), lambda b, g, i: (0, 0))],
        out_specs=pl.BlockSpec((1, tq, 2 * HEAD_DIM), lambda b, g, i: (b, i, g)),
        out_shape=jax.ShapeDtypeStruct((b_, n_q_tiles * tq, BRANCH_W), BF16),
        scratch_shapes=[pltpu.VMEM((2, 2 * tq, s), F32)],
        compiler_params=_cparams(("parallel", "arbitrary", "arbitrary")),
        name="diff_attn",
    )(lam_vecs, q, k, v, norm_g)


def _gqa_call(q, k, v, n_lat, n_ctx, n_q_tiles):
    b_, _, s, _ = q.shape
    tq = TOKEN_TILE
    q_spec, kv_spec = _attn_specs(s, tq, N_HEADS, N_HEADS // 2)
    return pl.pallas_call(
        functools.partial(_gqa_kernel, n_lat=n_lat, n_ctx=n_ctx),
        grid=(b_, 1, n_q_tiles),
        in_specs=[q_spec, kv_spec, kv_spec],
        out_specs=pl.BlockSpec((1, tq, BRANCH_W), lambda b, g, i: (b, i, 0)),
        out_shape=jax.ShapeDtypeStruct((b_, n_q_tiles * tq, BRANCH_W), BF16),
        scratch_shapes=[pltpu.VMEM((N_HEADS // 2, 2 * tq, s), F32)],
        compiler_params=_cparams(("parallel", "arbitrary", "arbitrary")),
        name="gqa_attn",
    )(q, k, v)


def _window_call(q, k, v, sink, n_lat, n_ctx, n_q_tiles):
    b_, _, s, _ = q.shape
    tq = TOKEN_TILE
    win_keys = min(tq + 2 * WINDOW, n_lat)
    q_spec, kv_spec = _attn_specs(s, tq, N_HEADS, N_HEADS // 2)
    return pl.pallas_call(
        functools.partial(_window_kernel, n_lat=n_lat, n_ctx=n_ctx, win_keys=win_keys),
        grid=(b_, 1, n_q_tiles),
        in_specs=[pl.BlockSpec(memory_space=pltpu.SMEM), q_spec, kv_spec, kv_spec],
        out_specs=pl.BlockSpec((1, tq, BRANCH_W), lambda b, g, i: (b, i, 0)),
        out_shape=jax.ShapeDtypeStruct((b_, n_q_tiles * tq, BRANCH_W), BF16),
        scratch_shapes=[pltpu.VMEM((N_HEADS // 2, 2 * tq, n_ctx + win_keys), F32)],
        compiler_params=_cparams(("parallel", "arbitrary", "arbitrary")),
        name="window_attn",
    )(sink, q, k, v)


def _row_iota(shape):
    return lax.broadcasted_iota(jnp.int32, shape, 0)


def _chunk_scan(x, pos, forward):
    n = x.shape[0]
    d = 1
    while d < CHUNK:
        if forward:
            x = x + jnp.where(pos >= d, pltpu.roll(x, d, 0), 0.0)
        else:
            x = x + jnp.where(pos < CHUNK - d, pltpu.roll(x, n - d, 0), 0.0)
        d *= 2
    return x


def _gdn_prep_kernel(prev_ref, cur_ref, next_ref, ab_ref, convw_ref, alog_ref, dtb_ref,
                     q_ref, k_ref, v_ref, gcol_ref, grow_ref, *, n_lat_tiles):
    ts = cur_ref.shape[1]
    j = pl.program_id(1)
    has_prev = jnp.logical_and(j != 0, j != n_lat_tiles)
    has_next = jnp.logical_and(j != n_lat_tiles - 1, j < n_lat_tiles)
    cur = cur_ref[0]
    prev = jnp.where(has_prev, prev_ref[0], 0.0)
    nxt = jnp.where(has_next, next_ref[0], 0.0)
    rows = _row_iota(cur.shape)
    half = CONV_K // 2
    y = cur * convw_ref[half:half + 1, :]
    for d in range(1, half + 1):
        back = jnp.where(rows < d, pltpu.roll(prev, d, 0), pltpu.roll(cur, d, 0))
        fwd = jnp.where(rows >= ts - d, pltpu.roll(nxt, ts - d, 0), pltpu.roll(cur, ts - d, 0))
        y = y + back * convw_ref[half - d:half - d + 1, :] + fwd * convw_ref[half + d:half + d + 1, :]
    y = _silu(y)
    ones_bd = _block_diag_ones(BRANCH_W, HEAD_DIM)
    q = y[:, 0:BRANCH_W]
    k = y[:, BRANCH_W:2 * BRANCH_W]
    q = q * lax.rsqrt(_group_sum(q * q, ones_bd) + EPS) * (HEAD_DIM ** -0.5)
    k = k * lax.rsqrt(_group_sum(k * k, ones_bd) + EPS)
    _store_heads(q_ref, q, N_HEADS)
    _store_heads(k_ref, k, N_HEADS)
    _store_heads(v_ref, y[:, 2 * BRANCH_W:3 * BRANCH_W], N_HEADS)

    ab = ab_ref[0]
    lane = lax.broadcasted_iota(jnp.int32, ab.shape, 1)
    is_a = (lane % 8) < N_HEADS
    z = ab + dtb_ref[...]
    softplus = jnp.maximum(z, 0.0) + jnp.log(1.0 + jnp.exp(-jnp.abs(z)))
    lg = jnp.where(is_a, -jnp.exp(alog_ref[...]) * softplus, 0.0)
    pos = _row_iota(ab.shape) % CHUNK
    cum = jnp.where(lane < 8, _chunk_scan(lg, pos, True), _chunk_scan(lg, pos, False))
    gates = jnp.where(is_a, cum, jax.nn.sigmoid(ab))
    gcol_ref[0] = gates
    gt = jnp.transpose(gates)
    for c in range(ts // CHUNK):
        grow_ref[0, c] = gt[0:16, c * CHUNK:(c + 1) * CHUNK]


def _gdn_prep_call(aqkv, aab, conv_w, alog_vec, dtb_vec, n_lat_tiles):
    b_, s, _ = aqkv.shape
    ts = TOKEN_TILE
    nt = s // ts
    w3 = 3 * BRANCH_W
    head = pl.BlockSpec((1, N_HEADS, ts, HEAD_DIM), lambda b, j: (b, 0, j, 0))
    hshape = jax.ShapeDtypeStruct((b_, N_HEADS, s, HEAD_DIM), F32)
    cpt = ts // CHUNK
    return pl.pallas_call(
        functools.partial(_gdn_prep_kernel, n_lat_tiles=n_lat_tiles),
        grid=(b_, nt),
        in_specs=[pl.BlockSpec((1, ts, w3), lambda b, j: (b, jnp.maximum(j - 1, 0), 0)),
                  pl.BlockSpec((1, ts, w3), lambda b, j: (b, j, 0)),
                  pl.BlockSpec((1, ts, w3), lambda b, j: (b, jnp.minimum(j + 1, nt - 1), 0)),
                  pl.BlockSpec((1, ts, LANES), lambda b, j: (b, j, 0)),
                  pl.BlockSpec((8, w3), lambda b, j: (0, 0)),
                  pl.BlockSpec((1, LANES), lambda b, j: (0, 0)),
                  pl.BlockSpec((1, LANES), lambda b, j: (0, 0))],
        out_specs=[head, head, head,
                   pl.BlockSpec((1, ts, LANES), lambda b, j: (b, j, 0)),
                   pl.BlockSpec((1, cpt, 16, CHUNK), lambda b, j: (b, j, 0, 0))],
        out_shape=[hshape, hshape, hshape,
                   jax.ShapeDtypeStruct((b_, s, LANES), F32),
                   jax.ShapeDtypeStruct((b_, s // CHUNK, 16, CHUNK), F32)],
        compiler_params=_cparams(("parallel", "arbitrary")),
        name="gdn_prep",
    )(aqkv, aqkv, aqkv, aab, conv_w, alog_vec, dtb_vec)


def _gdn_local_kernel(q_ref, k_ref, v_ref, gcol_ref, grow_ref,
                      wq_f, p_f, kdt_f, ug_f, wq_b, p_b, kdt_b, ug_b):
    c = CHUNK
    n = N_HEADS * c
    r = lax.broadcasted_iota(jnp.int32, (n, n), 0)
    s_ = lax.broadcasted_iota(jnp.int32, (n, n), 1)
    same_head = (r // c) == (s_ // c)
    eye = jnp.where(r == s_, 1.0, 0.0)
    eye_c = eye[0:c, 0:c]
    tile = jnp.where(lax.broadcasted_iota(jnp.int32, (c, n), 0) == lax.broadcasted_iota(jnp.int32, (c, n), 1) % c,
                     1.0, 0.0)
    outs = ((wq_f, p_f, kdt_f, ug_f), (wq_b, p_b, kdt_b, ug_b))
    stack_cols = lambda a, lane0: jnp.concatenate([a[:, lane0 + h:lane0 + h + 1] for h in range(N_HEADS)], axis=0)
    for cc in range(q_ref.shape[2] // c):
        rows = slice(cc * c, (cc + 1) * c)
        q = q_ref[0, :, rows, :].reshape(n, HEAD_DIM)
        k = k_ref[0, :, rows, :].reshape(n, HEAD_DIM)
        v = v_ref[0, :, rows, :].reshape(n, HEAD_DIM)
        gcol = gcol_ref[0, rows, :]
        grow = grow_ref[0, cc]
        kk = _bdot_nt(k, k)
        qk = _bdot_nt(q, k)
        kt = _bdot_nt(eye_c, k)
        for direction in range(2):
            wq_ref, p_ref, kdt_ref, ug_ref = outs[direction]
            base = 8 * direction
            g_c = stack_cols(gcol, base)
            beta_c = stack_cols(gcol, base + N_HEADS)
            g_r = jnp.concatenate([grow[base + h:base + h + 1, :] for h in range(N_HEADS)], axis=1)
            end = c - 1 if direction == 0 else 0
            g_end_r = jnp.concatenate([jnp.broadcast_to(grow[base + h:base + h + 1, end:end + 1], (1, c))
                                       for h in range(N_HEADS)], axis=1)
            g_end_c = jnp.concatenate([jnp.broadcast_to(gcol[end:end + 1, base + h:base + h + 1], (c, 1))
                                       for h in range(N_HEADS)], axis=0)
            if direction == 0:
                incl, strict = same_head & (r >= s_), same_head & (r > s_)
            else:
                incl, strict = same_head & (r <= s_), same_head & (r < s_)
            decay = jnp.exp(jnp.where(incl, g_c - g_r, -jnp.inf))
            a_mat = jnp.where(strict, beta_c * kk * decay, 0.0)
            pw = -a_mat
            t_inv = eye + pw
            for _ in range(5):
                pw = _bdot(pw, pw)
                t_inv = t_inv + _bdot(t_inv, pw)
            eg = jnp.exp(g_c)
            rhs = jnp.concatenate([beta_c * eg * k, beta_c * v], axis=1)
            sol = _bdot(t_inv, rhs)
            wq = jnp.concatenate([sol[:, 0:c], q * eg], axis=0)
            wq_tiled = _bdot(wq, tile)
            wq_ref[0, cc, 0:n, :] = jnp.where(same_head, wq_tiled[0:n], 0.0).astype(BF16)
            wq_ref[0, cc, n:2 * n, :] = jnp.where(same_head, wq_tiled[n:2 * n], 0.0).astype(BF16)
            p_ref[0, cc] = (qk * decay).astype(BF16)
            kdt = kt * jnp.exp(g_end_r - g_r)
            kdt_ref[0, cc] = jnp.where(same_head, jnp.concatenate([kdt] * N_HEADS, axis=0), 0.0).astype(BF16)
            ug_ref[0, cc, 0] = sol[:, c:2 * c]
            ug_ref[0, cc, 1] = jnp.broadcast_to(jnp.exp(g_end_c), (n, HEAD_DIM))


GDN_CHUNKS_PER_STEP = 2


def _gdn_local_call(qn, kn, vn, gcol, grow):
    b_, nh, s, _ = qn.shape
    nch = s // CHUNK
    cps = GDN_CHUNKS_PER_STEP
    n = nh * CHUNK
    head = pl.BlockSpec((1, nh, cps * CHUNK, HEAD_DIM), lambda b, c: (b, 0, c, 0))
    spec = lambda *shape: pl.BlockSpec((1, cps) + shape, lambda b, c: (b, c) + (0,) * len(shape))
    out_specs = [spec(2 * n, n), spec(n, n), spec(n, n), spec(2, n, HEAD_DIM)]
    out_shape = [jax.ShapeDtypeStruct((b_, nch, 2 * n, n), BF16), jax.ShapeDtypeStruct((b_, nch, n, n), BF16),
                 jax.ShapeDtypeStruct((b_, nch, n, n), BF16), jax.ShapeDtypeStruct((b_, nch, 2, n, HEAD_DIM), F32)]
    return pl.pallas_call(
        _gdn_local_kernel,
        grid=(b_, nch // cps),
        in_specs=[head, head, head,
                  pl.BlockSpec((1, cps * CHUNK, LANES), lambda b, c: (b, c, 0)),
                  pl.BlockSpec((1, cps, 16, CHUNK), lambda b, c: (b, c, 0, 0))],
        out_specs=out_specs * 2,
        out_shape=out_shape * 2,
        compiler_params=_cparams(("parallel", "arbitrary")),
        name="gdn_local",
    )(qn, kn, vn, gcol, grow)


def _gdn_scan_kernel(wq_f, p_f, kdt_f, ug_f, wq_b, p_b, kdt_b, ug_b, of_ref, ob_ref, state_ref):
    n = N_HEADS * CHUNK

    @pl.when(pl.program_id(1) == 0)
    def _():
        state_ref[...] = jnp.zeros_like(state_ref)

    for d, (wq, p, kdt, ug, dst) in enumerate(((wq_f, p_f, kdt_f, ug_f, of_ref), (wq_b, p_b, kdt_b, ug_b, ob_ref))):
        st = state_ref[d]
        ws = jnp.dot(wq[0, 0], st.astype(BF16), preferred_element_type=F32)
        u = ug[0, 0, 0] - ws[0:n]
        u_b = u.astype(BF16)
        o = ws[n:2 * n] + jnp.dot(p[0, 0], u_b, preferred_element_type=F32)
        for h in range(N_HEADS):
            dst[0, h] = o[h * CHUNK:(h + 1) * CHUNK]
        state_ref[d] = ug[0, 0, 1] * st + jnp.dot(kdt[0, 0], u_b, preferred_element_type=F32)


def _gdn_scan_call(local_out, n_lat_chunks, n_ctx_chunks):
    wq = local_out[0]
    b_, nch = wq.shape[0], wq.shape[1]
    nh = N_HEADS
    n = nh * CHUNK
    s = nch * CHUNK
    nlc, ncc = n_lat_chunks, n_ctx_chunks

    def order_f(i):
        return jnp.where(i < ncc, nlc + i, i - ncc)

    def order_b(i):
        return jnp.where(i < ncc, nlc + ncc - 1 - i, nlc - 1 - (i - ncc))

    def specs(order):
        sp = lambda *shape: pl.BlockSpec((1, 1) + shape, lambda b, i: (b, order(i)) + (0,) * len(shape))
        return [sp(2 * n, n), sp(n, n), sp(n, n), sp(2, n, HEAD_DIM)]

    oo = lambda order: pl.BlockSpec((1, nh, CHUNK, HEAD_DIM), lambda b, i: (b, 0, order(i), 0))
    oshape = jax.ShapeDtypeStruct((b_, nh, s, HEAD_DIM), F32)
    return pl.pallas_call(
        _gdn_scan_kernel,
        grid=(b_, nch),
        in_specs=specs(order_f) + specs(order_b),
        out_specs=[oo(order_f), oo(order_b)],
        out_shape=[oshape, oshape],
        scratch_shapes=[pltpu.VMEM((2, n, HEAD_DIM), F32)],
        compiler_params=_cparams(("parallel", "arbitrary")),
        name="gdn_scan",
    )(*local_out)


def _gdn_finish_kernel(of_ref, ob_ref, gate_ref, g_ref, o_ref):
    gate = gate_ref[0].astype(F32)
    for h in range(N_HEADS):
        o = of_ref[0, h] + ob_ref[0, h]
        o = o * lax.rsqrt(jnp.mean(o * o, axis=1, keepdims=True) + EPS) * g_ref[...]
        o = o * _silu(gate[:, h * HEAD_DIM:(h + 1) * HEAD_DIM])
        o_ref[0, :, h * HEAD_DIM:(h + 1) * HEAD_DIM] = o.astype(o_ref.dtype)


def _gdn_finish_call(o_f, o_b, agate, norm_g):
    b_, nh, s, _ = o_f.shape
    ts = TOKEN_TILE
    head = pl.BlockSpec((1, nh, ts, HEAD_DIM), lambda b, j: (b, 0, j, 0))
    tok = pl.BlockSpec((1, ts, BRANCH_W), lambda b, j: (b, j, 0))
    return pl.pallas_call(
        _gdn_finish_kernel,
        grid=(b_, s // ts),
        in_specs=[head, head, tok, pl.BlockSpec((1, HEAD_DIM), lambda b, j: (0, 0))],
        out_specs=tok,
        out_shape=jax.ShapeDtypeStruct((b_, s, BRANCH_W), BF16),
        compiler_params=_cparams(("parallel", "arbitrary")),
        name="gdn_finish",
    )(o_f, o_b, agate, norm_g)


def _merge_kernel(oa_ref, ob_ref, oc_ref, od_ref, gates_ref, x_ref, mod_ref, wbr_ref, wout_ref,
                  lng_ref, lnb_ref, wr_ref, x1_ref, h2_ref, aff_ref, *, alpha):
    d = D_MODEL
    m = None
    for i, o_ref in enumerate((oa_ref, ob_ref, oc_ref, od_ref)):
        gate = jax.nn.sigmoid(gates_ref[0, :, i * d:(i + 1) * d].astype(F32))
        term = gate * jnp.dot(o_ref[0], wbr_ref[i], preferred_element_type=F32)
        m = term if m is None else m + term
    y = _bdot(m, wout_ref[...])
    g1 = mod_ref[0, 0, :, 2 * d:3 * d]
    sh2 = mod_ref[0, 0, :, 3 * d:4 * d]
    sc2 = mod_ref[0, 0, :, 4 * d:5 * d]
    x1 = _layer_norm(alpha * x_ref[0] + g1 * y, lng_ref[...], lnb_ref[...])
    x1_ref[0] = x1
    h2 = x1 * (1.0 + sc2) + sh2
    h2_ref[0] = h2
    logits = _bdot(h2, wr_ref[...])
    lane = lax.broadcasted_iota(jnp.int32, logits.shape, 1)
    logits = jnp.where(lane < N_EXPERTS, logits, NEG_BIG)
    e = jnp.exp(logits - jnp.max(logits, axis=1, keepdims=True))
    aff_ref[0] = e / jnp.sum(e, axis=1, keepdims=True)


def _merge_call(outs, gates, x_all, modv, w_br, w_out, ln_g, ln_b, w_router, alpha, n_lat_tiles, n_tiles):
    b_, s, d = x_all.shape
    ts = TOKEN_TILE
    tok = lambda w: pl.BlockSpec((1, ts, w), lambda b, j: (b, j, 0))
    const2 = lambda shape: pl.BlockSpec(shape, lambda b, j: (0,) * len(shape))
    return pl.pallas_call(
        functools.partial(_merge_kernel, alpha=alpha),
        grid=(b_, n_tiles),
        in_specs=[tok(BRANCH_W)] * 4 + [
            tok(GATE_W), tok(d),
            pl.BlockSpec((1, 1, 1, 6 * d), lambda b, j: (b, j // n_lat_tiles, 0, 0)),
            const2((N_BRANCH, BRANCH_W, d)), const2((d, d)), const2((1, d)), const2((1, d)),
            const2((d, LANES))],
        out_specs=[tok(d), tok(d), tok(LANES)],
        out_shape=[jax.ShapeDtypeStruct((b_, n_tiles * ts, d), F32),
                   jax.ShapeDtypeStruct((b_, n_tiles * ts, d), F32),
                   jax.ShapeDtypeStruct((b_, n_tiles * ts, LANES), F32)],
        compiler_params=_cparams(("parallel", "arbitrary")),
        name="merge_ln1_router",
    )(*outs, gates, x_all, modv, w_br, w_out, ln_g, ln_b, w_router)


ROW_LOOP_UNROLL = 8


def _moe_kernel(idx_ref, gval_ref, h_ref, wg_ref, wu_ref, wd_ref, out_hbm, xg_ref, y_ref, acc_ref, sem,
                *, cap, slots):
    e = pl.program_id(1)
    nb = h_ref.shape[0]

    @pl.when(e == 0)
    def _():
        acc_ref[...] = jnp.zeros_like(acc_ref)

    def gather(j, carry):
        xg_ref[pl.ds(j, 1), :] = h_ref[j // cap, pl.ds(idx_ref[0, 0, 0, j], 1), :]
        return carry

    lax.fori_loop(0, slots, gather, 0, unroll=ROW_LOOP_UNROLL)
    xg = xg_ref[...].astype(BF16)
    a = jnp.dot(xg, wg_ref[0], preferred_element_type=F32)
    u = jnp.dot(xg, wu_ref[0], preferred_element_type=F32)
    y_ref[...] = _bdot(_silu(a) * u, wd_ref[0])

    def scatter(j, carry):
        bb = j // cap
        r = idx_ref[0, 0, 0, j]
        acc_ref[bb, pl.ds(r, 1), :] = (acc_ref[bb, pl.ds(r, 1), :]
                                       + gval_ref[0, 0, 0, j] * y_ref[pl.ds(j, 1), :])
        return carry

    lax.fori_loop(0, slots, scatter, 0, unroll=ROW_LOOP_UNROLL)

    @pl.when(e == pl.num_programs(1) - 1)
    def _():
        cp = pltpu.make_async_copy(acc_ref, out_hbm.at[pl.ds(pl.program_id(0) * nb, nb)], sem)
        cp.start()
        cp.wait()


def _moe_call(idx, gval, h2, wg, wu, wd, row_block, n_rows, group):
    g_, n_e, _, slots = idx.shape
    b_, s, d = h2.shape
    f = wg.shape[2]
    cap = slots // group
    smem = lambda: pl.BlockSpec((1, 1, 1, slots), lambda g, e: (g, e, 0, 0), memory_space=pltpu.SMEM)
    return pl.pallas_call(
        functools.partial(_moe_kernel, cap=cap, slots=slots),
        grid=(g_, n_e),
        in_specs=[smem(), smem(),
                  pl.BlockSpec((group, n_rows, d), lambda g, e: (g, row_block, 0), pipeline_mode=pl.Buffered(1)),
                  pl.BlockSpec((1, d, f), lambda g, e: (e, 0, 0)),
                  pl.BlockSpec((1, d, f), lambda g, e: (e, 0, 0)),
                  pl.BlockSpec((1, f, d), lambda g, e: (e, 0, 0))],
        out_specs=pl.BlockSpec(memory_space=pl.ANY),
        out_shape=jax.ShapeDtypeStruct((b_, n_rows, d), F32),
        scratch_shapes=[pltpu.VMEM((slots, d), F32), pltpu.VMEM((slots, d), F32),
                        pltpu.VMEM((group, n_rows, d), F32), pltpu.SemaphoreType.DMA(())],
        compiler_params=_cparams(("arbitrary", "arbitrary"), vmem=60 * 1024 * 1024),
        name="moe_experts",
    )(idx, gval, h2, wg, wu, wd)


def _route(aff, n_lat, n_ctx, with_ctx):
    b_ = aff.shape[0]

    def pick(a, n):
        cap = max(1, CAPACITY_FACTOR * n // N_EXPERTS)
        gval, idx = lax.top_k(jnp.swapaxes(a, 1, 2), cap)
        return gval, idx.astype(jnp.int32), cap

    gl, il, cap_l = pick(aff[:, :n_lat, :N_EXPERTS], n_lat)
    lat = (il.reshape(b_, N_EXPERTS, 1, cap_l), gl.reshape(b_, N_EXPERTS, 1, cap_l))
    if not with_ctx:
        return lat, None
    gc, ic, cap_c = pick(aff[:, n_lat:, :N_EXPERTS], n_ctx)
    flat = lambda t: jnp.transpose(t, (1, 0, 2)).reshape(1, N_EXPERTS, 1, b_ * cap_c)
    return lat, (flat(ic), flat(gc))


def _ln2_kernel(x1_ref, fl_ref, fc_ref, mod_ref, g_ref, b_ref, o_ref, *, alpha, n_lat_tiles):
    d = D_MODEL
    j = pl.program_id(1)
    f = jnp.where(j < n_lat_tiles, fl_ref[0], fc_ref[0])
    g2 = mod_ref[0, 0, :, 5 * d:6 * d]
    o_ref[0] = _layer_norm(alpha * x1_ref[0] + g2 * f, g_ref[...], b_ref[...])


def _ln2_call(x1, f_lat, f_ctx, modv, ln_g, ln_b, alpha, n_lat_tiles, n_tiles):
    b_, s, d = x1.shape
    ts = TOKEN_TILE
    return pl.pallas_call(
        functools.partial(_ln2_kernel, alpha=alpha, n_lat_tiles=n_lat_tiles),
        grid=(b_, n_tiles),
        in_specs=[pl.BlockSpec((1, ts, d), lambda b, j: (b, j, 0)),
                  pl.BlockSpec((1, ts, d), lambda b, j: (b, jnp.minimum(j, n_lat_tiles - 1), 0)),
                  pl.BlockSpec((1, ts, d), lambda b, j: (b, jnp.maximum(j - n_lat_tiles, 0), 0)),
                  pl.BlockSpec((1, 1, 1, 6 * d), lambda b, j: (b, j // n_lat_tiles, 0, 0)),
                  pl.BlockSpec((1, d), lambda b, j: (0, 0)),
                  pl.BlockSpec((1, d), lambda b, j: (0, 0))],
        out_specs=pl.BlockSpec((1, ts, d), lambda b, j: (b, j, 0)),
        out_shape=jax.ShapeDtypeStruct((b_, n_tiles * ts, d), F32),
        compiler_params=_cparams(("parallel", "arbitrary")),
        name="ln2",
    )(x1, f_lat, f_ctx, modv, ln_g, ln_b)


def _rope_tables(n_lat, n_ctx, dim):
    nf = dim // 4
    t = jnp.arange(n_lat)
    row = (t // GRID_W).astype(F32)
    col = (t % GRID_W).astype(F32)
    inv = ROPE_THETA ** (-jnp.arange(nf, dtype=F32) / nf)
    ar = row[:, None] * inv
    ac = col[:, None] * inv
    cos = jnp.concatenate([jnp.cos(ar), jnp.cos(ar), jnp.cos(ac), jnp.cos(ac)], axis=1)
    sin = jnp.concatenate([-jnp.sin(ar), jnp.sin(ar), -jnp.sin(ac), jnp.sin(ac)], axis=1)
    reps = BRANCH_W // dim
    cos = jnp.concatenate([jnp.tile(cos, (1, reps)), jnp.ones((n_ctx, BRANCH_W), F32)], axis=0)
    sin = jnp.concatenate([jnp.tile(sin, (1, reps)), jnp.zeros((n_ctx, BRANCH_W), F32)], axis=0)
    return cos, sin


def _reorder_w_in(w_in):
    o_ab = 3 * BRANCH_W
    o_gate = o_ab + 4 * N_HEADS
    o_b = o_gate + BRANCH_W
    o_gates = o_b + 3 * BRANCH_W + 4 * BRANCH_W
    pad = jnp.zeros(w_in.shape[:2] + (LANES - 4 * N_HEADS,), w_in.dtype)
    return jnp.concatenate([w_in[..., o_gates:], w_in[..., :o_ab], w_in[..., o_ab:o_gate], pad,
                            w_in[..., o_gate:o_b], w_in[..., o_b:o_gates]], axis=-1).astype(BF16)


def _lane_vec(p):
    v = jnp.zeros((LANES,), F32)
    v = v.at[0:N_HEADS].set(p[0]).at[8:8 + N_HEADS].set(p[1])
    return v.reshape(1, LANES)


def kernel(x, c, ctx, c_ctx, w_mod, b_mod, w_in, conv_a, a_log, dt_bias, gdn_norm, diff_lambda, diff_norm,
           qk_norm_c, sink_d, w_br, w_out, ln1_g, ln1_b, w_router, w_gate_e, w_up_e, w_down_e, ln2_g, ln2_b):
    b_, n_lat, d = x.shape
    n_ctx = ctx.shape[1]
    depth = w_mod.shape[0]
    ts = TOKEN_TILE
    assert d == D_MODEL and n_lat % ts == 0 and n_ctx == ts and n_lat % GRID_W == 0
    assert n_lat >= ts + 2 * WINDOW or n_lat == ts
    n_lat_tiles = n_lat // ts
    n_tiles = n_lat_tiles + n_ctx // ts
    alpha = (2.0 * depth) ** 0.25

    rows = -(-(b_ + 1) // 8) * 8
    cc = jnp.zeros((rows, d), F32).at[:b_].set(c).at[b_].set(c_ctx)
    mod_all = _mod_call(cc, w_mod, b_mod)

    w_all = _reorder_w_in(w_in)
    w_br_b = w_br.astype(BF16)
    w_out_b = w_out.astype(BF16)
    w_r = jnp.concatenate([w_router, jnp.zeros((depth, d, LANES - N_EXPERTS), F32)], axis=-1).astype(BF16)
    wg_b, wu_b, wd_b = w_gate_e.astype(BF16), w_up_e.astype(BF16), w_down_e.astype(BF16)
    conv_w = jnp.concatenate([conv_a, jnp.zeros((depth, 8 - CONV_K, conv_a.shape[2]), F32)], axis=1)
    tabs = _rope_tables(n_lat, n_ctx, HEAD_DIM) + _rope_tables(n_lat, n_ctx, DH_B)

    x_all = jnp.concatenate([x, ctx], axis=1)
    for l in range(depth):
        last = l == depth - 1
        lam_init = 0.8 - 0.6 * math.exp(-0.3 * l)
        modv = jnp.stack([mod_all[l, :b_], jnp.broadcast_to(mod_all[l, b_], (b_, 6 * d))], axis=1)
        modv = modv.reshape(b_, 2, 1, 6 * d)
        gq = jnp.tile(qk_norm_c[l, 0], N_HEADS).reshape(1, BRANCH_W)
        gk = jnp.tile(qk_norm_c[l, 1], N_HEADS // 2).reshape(1, BRANCH_W // 2)

        (gates, aqkv, aab, agate, qb, kb, vb, qc, kc, vc, qd, kd, vd) = _inproj_call(
            x_all, modv, w_all[l], tabs, gq, gk, n_lat_tiles)

        qn, kn, vn, gcol, grow = _gdn_prep_call(aqkv, aab, conv_w[l], _lane_vec(a_log[l]), _lane_vec(dt_bias[l]),
                                                n_lat_tiles)
        local_out = _gdn_local_call(qn, kn, vn, gcol, grow)
        o_f, o_b = _gdn_scan_call(local_out, n_lat // CHUNK, n_ctx // CHUNK)
        out_a = _gdn_finish_call(o_f, o_b, agate, gdn_norm[l].reshape(1, HEAD_DIM))

        nq = n_lat_tiles if last else n_tiles
        out_b = _diff_attn_call(qb, kb, vb, diff_lambda[l], diff_norm[l].reshape(1, HEAD_DIM), lam_init,
                                n_lat, n_ctx, nq)
        out_c = _gqa_call(qc, kc, vc, n_lat, n_ctx, nq)
        out_d = _window_call(qd, kd, vd, sink_d[l], n_lat, n_ctx, nq)

        x1, h2, aff = _merge_call((out_a, out_b, out_c, out_d), gates, x_all, modv, w_br_b[l], w_out_b[l],
                                  ln1_g[l].reshape(1, d), ln1_b[l].reshape(1, d), w_r[l], alpha,
                                  n_lat_tiles, nq)

        route_lat, route_ctx = _route(aff, n_lat, n_ctx, not last)
        f_lat = _moe_call(route_lat[0], route_lat[1], h2, wg_b[l], wu_b[l], wd_b[l], 0, n_lat, 1)
        if last:
            f_ctx = f_lat
        else:
            f_ctx = _moe_call(route_ctx[0], route_ctx[1], h2, wg_b[l], wu_b[l], wd_b[l],
                              n_lat // n_ctx, n_ctx, b_)
        x_all = _ln2_call(x1, f_lat, f_ctx, modv, ln2_g[l].reshape(1, d), ln2_b[l].reshape(1, d), alpha,
                          n_lat_tiles, nq)
    return x_all[:, :n_lat]
```

```python
import functools
import math

import jax
import jax.numpy as jnp
from jax import lax
from jax.experimental import pallas as pl
from jax.experimental.pallas import tpu as pltpu

F32 = jnp.float32
BF16 = jnp.bfloat16

D_MODEL = 1024
HEAD_DIM = 64
N_HEADS = 4
BRANCH_W = 256
N_BRANCH = 4
GRID_W = 64
ROPE_THETA = 10000.0
EPS = 1e-6
CONV_K = 5
CHUNK = 64
DH_B = 32
WINDOW = 128
N_EXPERTS = 16
CAPACITY_FACTOR = 2
LOG2E = 1.4426950408889634

TOKEN_TILE = 256
LANES = 128
VMEM_LIMIT = 56 * 1024 * 1024

GATE_W = N_BRANCH * D_MODEL
C_GATES = 0
C_AQKV = C_GATES + GATE_W
C_AAB = C_AQKV + 3 * BRANCH_W
C_AGATE = C_AAB + LANES
C_B = C_AGATE + BRANCH_W
C_C = C_B + 3 * BRANCH_W
C_D = C_C + 2 * BRANCH_W
W_ALL = C_D + 2 * BRANCH_W


def _cparams(sem, vmem=VMEM_LIMIT):
    return pltpu.CompilerParams(dimension_semantics=sem, vmem_limit_bytes=vmem)


def _bdot(a, b):
    return jnp.dot(a.astype(BF16), b.astype(BF16), preferred_element_type=F32)


def _bdot_nt(a, b):
    return lax.dot_general(a.astype(BF16), b.astype(BF16), (((1,), (1,)), ((), ())),
                           preferred_element_type=F32)


def _silu(x):
    return x * jax.nn.sigmoid(x)


def _block_diag_ones(width, group):
    r = lax.broadcasted_iota(jnp.int32, (width, width), 0) // group
    c = lax.broadcasted_iota(jnp.int32, (width, width), 1) // group
    return jnp.where(r == c, 1.0, 0.0).astype(BF16)


def _group_sum(x, ones_bd):
    hi = x.astype(BF16)
    lo = (x - hi.astype(F32)).astype(BF16)
    return (jnp.dot(hi, ones_bd, preferred_element_type=F32)
            + jnp.dot(lo, ones_bd, preferred_element_type=F32))


def _pair_swap(x, q):
    w = x.shape[-1]
    lane = lax.broadcasted_iota(jnp.int32, x.shape, x.ndim - 1)
    fwd = pltpu.roll(x, w - q, x.ndim - 1)
    bwd = pltpu.roll(x, q, x.ndim - 1)
    return jnp.where((lane % (2 * q)) < q, fwd, bwd)


def _layer_norm(x, g, b):
    mu = jnp.mean(x, axis=-1, keepdims=True)
    xc = x - mu
    var = jnp.mean(xc * xc, axis=-1, keepdims=True)
    return xc * lax.rsqrt(var + EPS) * g + b


def _mod_kernel(c_ref, w_ref, b_ref, o_ref):
    o_ref[0] = _bdot(_silu(c_ref[...]), w_ref[0]) + b_ref[0]


def _mod_call(cc, w_mod, b_mod):
    depth, d, n6 = w_mod.shape
    rows = cc.shape[0]
    nb = 1536
    return pl.pallas_call(
        _mod_kernel,
        grid=(depth, n6 // nb),
        in_specs=[pl.BlockSpec((rows, d), lambda l, j: (0, 0)),
                  pl.BlockSpec((1, d, nb), lambda l, j: (l, 0, j)),
                  pl.BlockSpec((1, 1, nb), lambda l, j: (l, 0, j))],
        out_specs=pl.BlockSpec((1, rows, nb), lambda l, j: (l, 0, j)),
        out_shape=jax.ShapeDtypeStruct((depth, rows, n6), F32),
        compiler_params=_cparams(("arbitrary", "arbitrary")),
        name="mod_vectors",
    )(cc, w_mod, b_mod.reshape(depth, 1, n6))


def _rope(x, cos, sin, quarter):
    return x * cos + _pair_swap(x, quarter) * sin


def _store_heads(ref, x, n_heads):
    for h in range(n_heads):
        ref[0, h] = x[:, h * HEAD_DIM:(h + 1) * HEAD_DIM].astype(ref.dtype)


def _store_key_heads(ref, x, n_heads):
    xt = jnp.transpose(x)
    for h in range(n_heads):
        ref[0, h] = xt[h * HEAD_DIM:(h + 1) * HEAD_DIM, :].astype(ref.dtype)


def _store_value_heads(ref, x, n_heads):
    lane = lax.broadcasted_iota(jnp.int32, (x.shape[0], HEAD_DIM), 1)
    ones_col = jnp.where(lane == 0, 1.0, 0.0)
    for h in range(n_heads):
        ref[0, h] = jnp.concatenate([x[:, h * HEAD_DIM:(h + 1) * HEAD_DIM], ones_col], axis=1).astype(ref.dtype)


def _inproj_kernel(x_ref, mod_ref, w_ref, cos_hd_ref, sin_hd_ref, cos_b_ref, sin_b_ref, gq_ref, gk_ref,
                   gates_ref, aqkv_ref, aab_ref, agate_ref,
                   qb_ref, kb_ref, vb_ref, qc_ref, kc_ref, vc_ref, qd_ref, kd_ref, vd_ref):
    d = D_MODEL
    x = x_ref[0]
    sh1 = mod_ref[0, 0, :, 0:d]
    sc1 = mod_ref[0, 0, :, d:2 * d]
    h = (x * (1.0 + sc1) + sh1).astype(BF16)

    def proj(a, width):
        return jnp.dot(h, w_ref[:, a:a + width], preferred_element_type=F32)

    gates_ref[0] = proj(C_GATES, GATE_W).astype(gates_ref.dtype)
    aqkv_ref[0] = proj(C_AQKV, 3 * BRANCH_W)
    aab_ref[0] = proj(C_AAB, LANES)
    agate_ref[0] = proj(C_AGATE, BRANCH_W).astype(agate_ref.dtype)

    cos_hd, sin_hd = cos_hd_ref[...], sin_hd_ref[...]
    cos_b, sin_b = cos_b_ref[...], sin_b_ref[...]
    kv_w = BRANCH_W // 2

    pb = proj(C_B, 3 * BRANCH_W)
    qb = _rope(pb[:, 0:BRANCH_W], cos_b, sin_b, DH_B // 4) * (DH_B ** -0.5 * LOG2E)
    kb = _rope(pb[:, BRANCH_W:2 * BRANCH_W], cos_b, sin_b, DH_B // 4)
    _store_heads(qb_ref, qb, N_HEADS)
    _store_key_heads(kb_ref, kb, N_HEADS)
    _store_value_heads(vb_ref, pb[:, 2 * BRANCH_W:3 * BRANCH_W], N_HEADS)

    pc = proj(C_C, 2 * BRANCH_W)
    ones_q = _block_diag_ones(BRANCH_W, HEAD_DIM)
    ones_k = _block_diag_ones(kv_w, HEAD_DIM)
    qc = pc[:, 0:BRANCH_W]
    kc = pc[:, BRANCH_W:BRANCH_W + kv_w]
    qc = qc * lax.rsqrt(_group_sum(qc * qc, ones_q) * (1.0 / HEAD_DIM) + EPS) * gq_ref[...]
    kc = kc * lax.rsqrt(_group_sum(kc * kc, ones_k) * (1.0 / HEAD_DIM) + EPS) * gk_ref[...]
    qc = _rope(qc, cos_hd, sin_hd, HEAD_DIM // 4) * (HEAD_DIM ** -0.5 * LOG2E)
    kc = _rope(kc, cos_hd[:, 0:kv_w], sin_hd[:, 0:kv_w], HEAD_DIM // 4)
    _store_heads(qc_ref, qc, N_HEADS)
    _store_key_heads(kc_ref, kc, N_HEADS // 2)
    _store_value_heads(vc_ref, pc[:, BRANCH_W + kv_w:2 * BRANCH_W], N_HEADS // 2)

    pd = proj(C_D, 2 * BRANCH_W)
    qd = _rope(pd[:, 0:BRANCH_W], cos_hd, sin_hd, HEAD_DIM // 4) * (HEAD_DIM ** -0.5 * LOG2E)
    kd = _rope(pd[:, BRANCH_W:BRANCH_W + kv_w], cos_hd[:, 0:kv_w], sin_hd[:, 0:kv_w], HEAD_DIM // 4)
    _store_heads(qd_ref, qd, N_HEADS)
    _store_key_heads(kd_ref, kd, N_HEADS // 2)
    _store_value_heads(vd_ref, pd[:, BRANCH_W + kv_w:2 * BRANCH_W], N_HEADS // 2)


def _inproj_call(x_all, modv, w_all, tabs, gq, gk, n_lat_tiles):
    b_, s, d = x_all.shape
    ts = TOKEN_TILE
    nt = s // ts
    tok = lambda w: pl.BlockSpec((1, ts, w), lambda b, j: (b, j, 0))
    tab = pl.BlockSpec((ts, BRANCH_W), lambda b, j: (j, 0))
    head = lambda nh, w=HEAD_DIM: pl.BlockSpec((1, nh, ts, w), lambda b, j: (b, 0, j, 0))
    hshape = lambda nh, w=HEAD_DIM: jax.ShapeDtypeStruct((b_, nh, s, w), BF16)
    khead = lambda nh: pl.BlockSpec((1, nh, HEAD_DIM, ts), lambda b, j: (b, 0, 0, j))
    kshape = lambda nh: jax.ShapeDtypeStruct((b_, nh, HEAD_DIM, s), BF16)
    kvh = N_HEADS // 2
    return pl.pallas_call(
        _inproj_kernel,
        grid=(b_, nt),
        in_specs=[tok(d),
                  pl.BlockSpec((1, 1, 1, 6 * d), lambda b, j: (b, j // n_lat_tiles, 0, 0)),
                  pl.BlockSpec((d, W_ALL), lambda b, j: (0, 0), pipeline_mode=pl.Buffered(1)),
                  tab, tab, tab, tab,
                  pl.BlockSpec((1, BRANCH_W), lambda b, j: (0, 0)),
                  pl.BlockSpec((1, BRANCH_W // 2), lambda b, j: (0, 0))],
        out_specs=[tok(GATE_W), tok(3 * BRANCH_W), tok(LANES), tok(BRANCH_W),
                   head(N_HEADS), khead(N_HEADS), head(N_HEADS, LANES),
                   head(N_HEADS), khead(kvh), head(kvh, LANES),
                   head(N_HEADS), khead(kvh), head(kvh, LANES)],
        out_shape=[jax.ShapeDtypeStruct((b_, s, GATE_W), BF16),
                   jax.ShapeDtypeStruct((b_, s, 3 * BRANCH_W), F32),
                   jax.ShapeDtypeStruct((b_, s, LANES), F32),
                   jax.ShapeDtypeStruct((b_, s, BRANCH_W), BF16),
                   hshape(N_HEADS), kshape(N_HEADS), hshape(N_HEADS, LANES),
                   hshape(N_HEADS), kshape(kvh), hshape(kvh, LANES),
                   hshape(N_HEADS), kshape(kvh), hshape(kvh, LANES)],
        compiler_params=_cparams(("parallel", "arbitrary")),
        name="inproj",
    )(x_all, modv, w_all, *tabs, gq, gk)


NEG_BIG = -1e30


def _softmax_pv(streams, chunks, mask_fn=None):
    m = [st[3] for st in streams]
    acc = [None] * len(streams)
    for i, (q2, k_ref, v_ref, _) in enumerate(streams):
        for ks, n, tag in chunks:
            s = jnp.dot(q2, k_ref[:, pl.ds(ks, n)], preferred_element_type=F32)
            if mask_fn is not None:
                s = mask_fn(s, ks, n, tag)
            cm = jnp.max(s, axis=1, keepdims=True)
            m_new = cm if m[i] is None else jnp.maximum(m[i], cm)
            pv = jnp.dot(jnp.exp2(s - m_new).astype(BF16), v_ref[pl.ds(ks, n), :], preferred_element_type=F32)
            acc[i] = pv if acc[i] is None else acc[i] * jnp.exp2(m[i] - m_new) + pv
            m[i] = m_new
    out = []
    for i, st in enumerate(streams):
        l = acc[i][:, HEAD_DIM:HEAD_DIM + 1]
        if st[3] is not None:
            l = l + jnp.exp2(st[3] - m[i])
        out.append((acc[i][:, 0:HEAD_DIM], l))
    return out


def _key_chunks(start, total, size):
    out, a = [], 0
    while a < total:
        n = min(size, total - a)
        out.append((start + a, n, a))
        a += n
    return out


KEY_CHUNK = 512


def _diff_attn_kernel(lam_ref, q_ref, k_ref, v_ref, g_ref, o_ref, *, n_lat, n_ctx, lam_init):
    tq = q_ref.shape[2]
    i = pl.program_id(2)
    lv = lam_ref[...]
    lam = (jnp.exp(jnp.sum(lv[0:1] * lv[1:2], axis=1, keepdims=True))
           - jnp.exp(jnp.sum(lv[2:3] * lv[3:4], axis=1, keepdims=True)) + lam_init)
    lane = lax.broadcasted_iota(jnp.int32, (tq, HEAD_DIM), 1)

    def run(chunks):
        streams = []
        for hh in range(N_HEADS):
            q = q_ref[0, hh]
            zero = jnp.zeros_like(q)
            q2 = jnp.concatenate([jnp.where(lane < DH_B, q, zero), jnp.where(lane < DH_B, zero, q)], axis=0)
            streams.append((q2, k_ref.at[0, hh], v_ref.at[0, hh], None))
        for hh, (acc, l) in enumerate(_softmax_pv(streams, chunks)):
            o = acc / l
            o = o[0:tq] - lam * o[tq:2 * tq]
            o = o * lax.rsqrt(jnp.mean(o * o, axis=1, keepdims=True) + EPS) * g_ref[...] * (1.0 - lam_init)
            o_ref[0, :, hh * HEAD_DIM:(hh + 1) * HEAD_DIM] = o.astype(o_ref.dtype)

    @pl.when(i * tq < n_lat)
    def _():
        run(_key_chunks(0, n_lat + n_ctx, KEY_CHUNK))

    @pl.when(i * tq >= n_lat)
    def _():
        run(_key_chunks(n_lat, n_ctx, KEY_CHUNK))


def _gqa_kernel(q_ref, k_ref, v_ref, o_ref, *, n_lat, n_ctx):
    tq = q_ref.shape[2]
    i = pl.program_id(2)

    def run(chunks):
        streams = [(jnp.concatenate([q_ref[0, 2 * g], q_ref[0, 2 * g + 1]], axis=0), k_ref.at[0, g], v_ref.at[0, g],
                    None) for g in range(N_HEADS // 2)]
        for g, (acc, l) in enumerate(_softmax_pv(streams, chunks)):
            o = acc / l
            for j in range(2):
                c0 = (2 * g + j) * HEAD_DIM
                o_ref[0, :, c0:c0 + HEAD_DIM] = o[j * tq:(j + 1) * tq].astype(o_ref.dtype)

    @pl.when(i * tq < n_lat)
    def _():
        run(_key_chunks(0, n_lat + n_ctx, KEY_CHUNK))

    @pl.when(i * tq >= n_lat)
    def _():
        run(_key_chunks(n_lat, n_ctx, KEY_CHUNK))


def _window_kernel(sink_ref, q_ref, k_ref, v_ref, o_ref, *, n_lat, n_ctx, win_keys):
    tq = q_ref.shape[2]
    i = pl.program_id(2)
    row = lax.broadcasted_iota(jnp.int32, (2 * tq, 1), 0)

    def run(chunks, mask):
        streams = []
        for g in range(N_HEADS // 2):
            sink = jnp.where(row < tq, sink_ref[2 * g], sink_ref[2 * g + 1]) * LOG2E
            q2 = jnp.concatenate([q_ref[0, 2 * g], q_ref[0, 2 * g + 1]], axis=0)
            streams.append((q2, k_ref.at[0, g], v_ref.at[0, g], sink))
        for g, (acc, l) in enumerate(_softmax_pv(streams, chunks, mask_fn=mask)):
            o = acc / l
            for j in range(2):
                c0 = (2 * g + j) * HEAD_DIM
                o_ref[0, :, c0:c0 + HEAD_DIM] = o[j * tq:(j + 1) * tq].astype(o_ref.dtype)

    @pl.when(i * tq < n_lat)
    def _():
        start = jnp.clip(i * tq - WINDOW, 0, n_lat - win_keys)
        start = pl.multiple_of(start, WINDOW)

        def mask(s, ks, n, col):
            if col < n_ctx:
                return s
            t = i * tq + lax.broadcasted_iota(jnp.int32, s.shape, 0) % tq
            sp = start + lax.broadcasted_iota(jnp.int32, s.shape, 1)
            return jnp.where(jnp.abs(t - sp) <= WINDOW, s, NEG_BIG)

        run([(n_lat, n_ctx, 0), (start, win_keys, n_ctx)], mask)

    @pl.when(i * tq >= n_lat)
    def _():
        run([(n_lat, n_ctx, 0)], None)


def _attn_specs(s, tq, q_heads, kv_heads):
    q_spec = pl.BlockSpec((1, q_heads, tq, HEAD_DIM), lambda b, g, i: (b, 0, i, 0))
    k_spec = pl.BlockSpec((1, kv_heads, HEAD_DIM, s), lambda b, g, i: (b, 0, 0, 0))
    v_spec = pl.BlockSpec((1, kv_heads, s, LANES), lambda b, g, i: (b, 0, 0, 0))
    o_spec = pl.BlockSpec((1, tq, BRANCH_W), lambda b, g, i: (b, i, 0))
    return q_spec, k_spec, v_spec, o_spec


def _diff_attn_call(q, k, v, lam_vecs, norm_g, lam_init, n_lat, n_ctx, n_q_tiles):
    b_, _, s, _ = q.shape
    tq = TOKEN_TILE
    q_spec, k_spec, v_spec, o_spec = _attn_specs(s, tq, N_HEADS, N_HEADS)
    return pl.pallas_call(
        functools.partial(_diff_attn_kernel, n_lat=n_lat, n_ctx=n_ctx, lam_init=lam_init),
        grid=(b_, 1, n_q_tiles),
        in_specs=[pl.BlockSpec((4, DH_B), lambda b, g, i: (0, 0)), q_spec, k_spec, v_spec,
                  pl.BlockSpec((1, HEAD_DIM), lambda b, g, i: (0, 0))],
        out_specs=o_spec,
        out_shape=jax.ShapeDtypeStruct((b_, n_q_tiles * tq, BRANCH_W), BF16),
        compiler_params=_cparams(("parallel", "arbitrary", "arbitrary")),
        name="diff_attn",
    )(lam_vecs, q, k, v, norm_g)


def _gqa_call(q, k, v, n_lat, n_ctx, n_q_tiles):
    b_, _, s, _ = q.shape
    tq = TOKEN_TILE
    q_spec, k_spec, v_spec, o_spec = _attn_specs(s, tq, N_HEADS, N_HEADS // 2)
    return pl.pallas_call(
        functools.partial(_gqa_kernel, n_lat=n_lat, n_ctx=n_ctx),
        grid=(b_, 1, n_q_tiles),
        in_specs=[q_spec, k_spec, v_spec],
        out_specs=o_spec,
        out_shape=jax.ShapeDtypeStruct((b_, n_q_tiles * tq, BRANCH_W), BF16),
        compiler_params=_cparams(("parallel", "arbitrary", "arbitrary")),
        name="gqa_attn",
    )(q, k, v)


def _window_call(q, k, v, sink, n_lat, n_ctx, n_q_tiles):
    b_, _, s, _ = q.shape
    tq = TOKEN_TILE
    win_keys = min(tq + 2 * WINDOW, n_lat)
    q_spec, k_spec, v_spec, o_spec = _attn_specs(s, tq, N_HEADS, N_HEADS // 2)
    return pl.pallas_call(
        functools.partial(_window_kernel, n_lat=n_lat, n_ctx=n_ctx, win_keys=win_keys),
        grid=(b_, 1, n_q_tiles),
        in_specs=[pl.BlockSpec(memory_space=pltpu.SMEM), q_spec, k_spec, v_spec],
        out_specs=o_spec,
        out_shape=jax.ShapeDtypeStruct((b_, n_q_tiles * tq, BRANCH_W), BF16),
        compiler_params=_cparams(("parallel", "arbitrary", "arbitrary")),
        name="window_attn",
    )(sink, q, k, v)


def _row_iota(shape):
    return lax.broadcasted_iota(jnp.int32, shape, 0)


def _chunk_scan(x, pos, forward):
    n = x.shape[0]
    d = 1
    while d < CHUNK:
        if forward:
            x = x + jnp.where(pos >= d, pltpu.roll(x, d, 0), 0.0)
        else:
            x = x + jnp.where(pos < CHUNK - d, pltpu.roll(x, n - d, 0), 0.0)
        d *= 2
    return x


def _gdn_prep_kernel(prev_ref, cur_ref, next_ref, ab_ref, convw_ref, alog_ref, dtb_ref,
                     q_ref, k_ref, v_ref, gcol_ref, grow_ref, *, n_lat_tiles):
    ts = cur_ref.shape[1]
    j = pl.program_id(1)
    has_prev = jnp.logical_and(j != 0, j != n_lat_tiles)
    has_next = jnp.logical_and(j != n_lat_tiles - 1, j < n_lat_tiles)
    cur = cur_ref[0]
    prev = jnp.where(has_prev, prev_ref[0], 0.0)
    nxt = jnp.where(has_next, next_ref[0], 0.0)
    rows = _row_iota(cur.shape)
    half = CONV_K // 2
    y = cur * convw_ref[half:half + 1, :]
    for d in range(1, half + 1):
        back = jnp.where(rows < d, pltpu.roll(prev, d, 0), pltpu.roll(cur, d, 0))
        fwd = jnp.where(rows >= ts - d, pltpu.roll(nxt, ts - d, 0), pltpu.roll(cur, ts - d, 0))
        y = y + back * convw_ref[half - d:half - d + 1, :] + fwd * convw_ref[half + d:half + d + 1, :]
    y = _silu(y)
    ones_bd = _block_diag_ones(BRANCH_W, HEAD_DIM)
    q = y[:, 0:BRANCH_W]
    k = y[:, BRANCH_W:2 * BRANCH_W]
    q = q * lax.rsqrt(_group_sum(q * q, ones_bd) + EPS) * (HEAD_DIM ** -0.5)
    k = k * lax.rsqrt(_group_sum(k * k, ones_bd) + EPS)
    _store_heads(q_ref, q, N_HEADS)
    _store_heads(k_ref, k, N_HEADS)
    _store_heads(v_ref, y[:, 2 * BRANCH_W:3 * BRANCH_W], N_HEADS)

    ab = ab_ref[0]
    lane = lax.broadcasted_iota(jnp.int32, ab.shape, 1)
    is_a = (lane % 8) < N_HEADS
    z = ab + dtb_ref[...]
    softplus = jnp.maximum(z, 0.0) + jnp.log(1.0 + jnp.exp(-jnp.abs(z)))
    lg = jnp.where(is_a, -jnp.exp(alog_ref[...]) * softplus, 0.0)
    pos = _row_iota(ab.shape) % CHUNK
    cum = jnp.where(lane < 8, _chunk_scan(lg, pos, True), _chunk_scan(lg, pos, False))
    gates = jnp.where(is_a, cum, jax.nn.sigmoid(ab))
    gcol_ref[0] = gates
    gt = jnp.transpose(gates)
    for c in range(ts // CHUNK):
        grow_ref[0, c] = gt[0:16, c * CHUNK:(c + 1) * CHUNK]


def _gdn_prep_call(aqkv, aab, conv_w, alog_vec, dtb_vec, n_lat_tiles):
    b_, s, _ = aqkv.shape
    ts = TOKEN_TILE
    nt = s // ts
    w3 = 3 * BRANCH_W
    head = pl.BlockSpec((1, N_HEADS, ts, HEAD_DIM), lambda b, j: (b, 0, j, 0))
    hshape = jax.ShapeDtypeStruct((b_, N_HEADS, s, HEAD_DIM), F32)
    cpt = ts // CHUNK
    return pl.pallas_call(
        functools.partial(_gdn_prep_kernel, n_lat_tiles=n_lat_tiles),
        grid=(b_, nt),
        in_specs=[pl.BlockSpec((1, ts, w3), lambda b, j: (b, jnp.maximum(j - 1, 0), 0)),
                  pl.BlockSpec((1, ts, w3), lambda b, j: (b, j, 0)),
                  pl.BlockSpec((1, ts, w3), lambda b, j: (b, jnp.minimum(j + 1, nt - 1), 0)),
                  pl.BlockSpec((1, ts, LANES), lambda b, j: (b, j, 0)),
                  pl.BlockSpec((8, w3), lambda b, j: (0, 0)),
                  pl.BlockSpec((1, LANES), lambda b, j: (0, 0)),
                  pl.BlockSpec((1, LANES), lambda b, j: (0, 0))],
        out_specs=[head, head, head,
                   pl.BlockSpec((1, ts, LANES), lambda b, j: (b, j, 0)),
                   pl.BlockSpec((1, cpt, 16, CHUNK), lambda b, j: (b, j, 0, 0))],
        out_shape=[hshape, hshape, hshape,
                   jax.ShapeDtypeStruct((b_, s, LANES), F32),
                   jax.ShapeDtypeStruct((b_, s // CHUNK, 16, CHUNK), F32)],
        compiler_params=_cparams(("parallel", "arbitrary")),
        name="gdn_prep",
    )(aqkv, aqkv, aqkv, aab, conv_w, alog_vec, dtb_vec)


def _gdn_local_kernel(q_ref, k_ref, v_ref, gcol_ref, grow_ref,
                      wq_f, p_f, kdt_f, ug_f, wq_b, p_b, kdt_b, ug_b):
    c = CHUNK
    n = N_HEADS * c
    r = lax.broadcasted_iota(jnp.int32, (n, n), 0)
    s_ = lax.broadcasted_iota(jnp.int32, (n, n), 1)
    same_head = (r // c) == (s_ // c)
    eye = jnp.where(r == s_, 1.0, 0.0)
    eye_c = eye[0:c, 0:c]
    tile = jnp.where(lax.broadcasted_iota(jnp.int32, (c, n), 0) == lax.broadcasted_iota(jnp.int32, (c, n), 1) % c,
                     1.0, 0.0)
    outs = ((wq_f, p_f, kdt_f, ug_f), (wq_b, p_b, kdt_b, ug_b))
    stack_cols = lambda a, lane0: jnp.concatenate([a[:, lane0 + h:lane0 + h + 1] for h in range(N_HEADS)], axis=0)
    chains = []
    for cc in range(q_ref.shape[2] // c):
        rows = slice(cc * c, (cc + 1) * c)
        q = q_ref[0, :, rows, :].reshape(n, HEAD_DIM)
        k = k_ref[0, :, rows, :].reshape(n, HEAD_DIM)
        v = v_ref[0, :, rows, :].reshape(n, HEAD_DIM)
        gcol = gcol_ref[0, rows, :]
        grow = grow_ref[0, cc]
        kk = _bdot_nt(k, k)
        qk = _bdot_nt(q, k)
        kt = _bdot_nt(eye_c, k)
        for direction in range(2):
            wq_ref, p_ref, kdt_ref, ug_ref = outs[direction]
            base = 8 * direction
            g_c = stack_cols(gcol, base)
            beta_c = stack_cols(gcol, base + N_HEADS)
            g_r = jnp.concatenate([grow[base + h:base + h + 1, :] for h in range(N_HEADS)], axis=1)
            end = c - 1 if direction == 0 else 0
            g_end_r = jnp.concatenate([jnp.broadcast_to(grow[base + h:base + h + 1, end:end + 1], (1, c))
                                       for h in range(N_HEADS)], axis=1)
            g_end_c = jnp.concatenate([jnp.broadcast_to(gcol[end:end + 1, base + h:base + h + 1], (c, 1))
                                       for h in range(N_HEADS)], axis=0)
            if direction == 0:
                incl, strict = same_head & (r >= s_), same_head & (r > s_)
            else:
                incl, strict = same_head & (r <= s_), same_head & (r < s_)
            decay = jnp.exp(jnp.where(incl, g_c - g_r, -jnp.inf))
            a_mat = jnp.where(strict, beta_c * kk * decay, 0.0)
            eg = jnp.exp(g_c)
            rhs = jnp.concatenate([beta_c * eg * k, beta_c * v], axis=1)
            p_ref[0, cc] = (qk * decay).astype(BF16)
            kdt = kt * jnp.exp(g_end_r - g_r)
            kdt_ref[0, cc] = jnp.where(same_head, jnp.concatenate([kdt] * N_HEADS, axis=0), 0.0).astype(BF16)
            ug_ref[0, cc, 1] = jnp.broadcast_to(jnp.exp(g_end_c), (n, HEAD_DIM))
            chains.append(dict(pw=-a_mat, t_inv=eye - a_mat, rhs=rhs, qg=q * eg, cc=cc, wq_ref=wq_ref, ug_ref=ug_ref))

    for _ in range(5):
        for ch in chains:
            ch["pw"] = _bdot(ch["pw"], ch["pw"])
        for ch in chains:
            ch["t_inv"] = ch["t_inv"] + _bdot(ch["t_inv"], ch["pw"])
    for ch in chains:
        cc, wq_ref, ug_ref = ch["cc"], ch["wq_ref"], ch["ug_ref"]
        sol = _bdot(ch["t_inv"], ch["rhs"])
        wq = jnp.concatenate([sol[:, 0:c], ch["qg"]], axis=0)
        wq_tiled = _bdot(wq, tile)
        wq_ref[0, cc, 0:n, :] = jnp.where(same_head, wq_tiled[0:n], 0.0).astype(BF16)
        wq_ref[0, cc, n:2 * n, :] = jnp.where(same_head, wq_tiled[n:2 * n], 0.0).astype(BF16)
        ug_ref[0, cc, 0] = sol[:, c:2 * c]


GDN_CHUNKS_PER_STEP = 4
GDN_SCAN_CHUNKS = 2


def _gdn_local_call(qn, kn, vn, gcol, grow):
    b_, nh, s, _ = qn.shape
    nch = s // CHUNK
    cps = GDN_CHUNKS_PER_STEP
    n = nh * CHUNK
    head = pl.BlockSpec((1, nh, cps * CHUNK, HEAD_DIM), lambda b, c: (b, 0, c, 0))
    spec = lambda *shape: pl.BlockSpec((1, cps) + shape, lambda b, c: (b, c) + (0,) * len(shape))
    out_specs = [spec(2 * n, n), spec(n, n), spec(n, n), spec(2, n, HEAD_DIM)]
    out_shape = [jax.ShapeDtypeStruct((b_, nch, 2 * n, n), BF16), jax.ShapeDtypeStruct((b_, nch, n, n), BF16),
                 jax.ShapeDtypeStruct((b_, nch, n, n), BF16), jax.ShapeDtypeStruct((b_, nch, 2, n, HEAD_DIM), F32)]
    return pl.pallas_call(
        _gdn_local_kernel,
        grid=(b_, nch // cps),
        in_specs=[head, head, head,
                  pl.BlockSpec((1, cps * CHUNK, LANES), lambda b, c: (b, c, 0)),
                  pl.BlockSpec((1, cps, 16, CHUNK), lambda b, c: (b, c, 0, 0))],
        out_specs=out_specs * 2,
        out_shape=out_shape * 2,
        compiler_params=_cparams(("parallel", "arbitrary")),
        name="gdn_local",
    )(qn, kn, vn, gcol, grow)


def _gdn_scan_kernel(wq_f, p_f, kdt_f, ug_f, wq_b, p_b, kdt_b, ug_b, of_ref, ob_ref, state_ref):
    n = N_HEADS * CHUNK

    @pl.when(pl.program_id(1) == 0)
    def _():
        state_ref[...] = jnp.zeros_like(state_ref)

    nc = wq_f.shape[1]
    for d, (wq, p, kdt, ug, dst) in enumerate(((wq_f, p_f, kdt_f, ug_f, of_ref), (wq_b, p_b, kdt_b, ug_b, ob_ref))):
        st = state_ref[d]
        for cc in (range(nc) if d == 0 else range(nc - 1, -1, -1)):
            ws = jnp.dot(wq[0, cc], st.astype(BF16), preferred_element_type=F32)
            u = ug[0, cc, 0] - ws[0:n]
            u_b = u.astype(BF16)
            o = ws[n:2 * n] + jnp.dot(p[0, cc], u_b, preferred_element_type=F32)
            for h in range(N_HEADS):
                dst[0, h, cc * CHUNK:(cc + 1) * CHUNK, :] = o[h * CHUNK:(h + 1) * CHUNK]
            st = ug[0, cc, 1] * st + jnp.dot(kdt[0, cc], u_b, preferred_element_type=F32)
        state_ref[d] = st


def _gdn_scan_call(local_out, n_lat_chunks, n_ctx_chunks):
    wq = local_out[0]
    b_, nch = wq.shape[0], wq.shape[1]
    nh = N_HEADS
    n = nh * CHUNK
    s = nch * CHUNK
    nc = GDN_SCAN_CHUNKS
    assert n_lat_chunks % nc == 0 and n_ctx_chunks % nc == 0
    nlc, ncc = n_lat_chunks // nc, n_ctx_chunks // nc

    def order_f(i):
        return jnp.where(i < ncc, nlc + i, i - ncc)

    def order_b(i):
        return jnp.where(i < ncc, nlc + ncc - 1 - i, nlc - 1 - (i - ncc))

    def specs(order):
        sp = lambda *shape: pl.BlockSpec((1, nc) + shape, lambda b, i: (b, order(i)) + (0,) * len(shape))
        return [sp(2 * n, n), sp(n, n), sp(n, n), sp(2, n, HEAD_DIM)]

    oo = lambda order: pl.BlockSpec((1, nh, nc * CHUNK, HEAD_DIM), lambda b, i: (b, 0, order(i), 0))
    oshape = jax.ShapeDtypeStruct((b_, nh, s, HEAD_DIM), F32)
    return pl.pallas_call(
        _gdn_scan_kernel,
        grid=(b_, nch // nc),
        in_specs=specs(order_f) + specs(order_b),
        out_specs=[oo(order_f), oo(order_b)],
        out_shape=[oshape, oshape],
        scratch_shapes=[pltpu.VMEM((2, n, HEAD_DIM), F32)],
        compiler_params=_cparams(("parallel", "arbitrary")),
        name="gdn_scan",
    )(*local_out)


def _gdn_finish_kernel(of_ref, ob_ref, gate_ref, g_ref, o_ref):
    gate = gate_ref[0].astype(F32)
    for h in range(N_HEADS):
        o = of_ref[0, h] + ob_ref[0, h]
        o = o * lax.rsqrt(jnp.mean(o * o, axis=1, keepdims=True) + EPS) * g_ref[...]
        o = o * _silu(gate[:, h * HEAD_DIM:(h + 1) * HEAD_DIM])
        o_ref[0, :, h * HEAD_DIM:(h + 1) * HEAD_DIM] = o.astype(o_ref.dtype)


def _gdn_finish_call(o_f, o_b, agate, norm_g):
    b_, nh, s, _ = o_f.shape
    ts = TOKEN_TILE
    head = pl.BlockSpec((1, nh, ts, HEAD_DIM), lambda b, j: (b, 0, j, 0))
    tok = pl.BlockSpec((1, ts, BRANCH_W), lambda b, j: (b, j, 0))
    return pl.pallas_call(
        _gdn_finish_kernel,
        grid=(b_, s // ts),
        in_specs=[head, head, tok, pl.BlockSpec((1, HEAD_DIM), lambda b, j: (0, 0))],
        out_specs=tok,
        out_shape=jax.ShapeDtypeStruct((b_, s, BRANCH_W), BF16),
        compiler_params=_cparams(("parallel", "arbitrary")),
        name="gdn_finish",
    )(o_f, o_b, agate, norm_g)


def _merge_kernel(oa_ref, ob_ref, oc_ref, od_ref, gates_ref, x_ref, mod_ref, wbr_ref, wout_ref,
                  lng_ref, lnb_ref, wr_ref, x1_ref, h2_ref, aff_ref, *, alpha):
    d = D_MODEL
    m = None
    for i, o_ref in enumerate((oa_ref, ob_ref, oc_ref, od_ref)):
        gate = jax.nn.sigmoid(gates_ref[0, :, i * d:(i + 1) * d].astype(F32))
        term = gate * jnp.dot(o_ref[0], wbr_ref[i], preferred_element_type=F32)
        m = term if m is None else m + term
    y = _bdot(m, wout_ref[...])
    g1 = mod_ref[0, 0, :, 2 * d:3 * d]
    sh2 = mod_ref[0, 0, :, 3 * d:4 * d]
    sc2 = mod_ref[0, 0, :, 4 * d:5 * d]
    x1 = _layer_norm(alpha * x_ref[0] + g1 * y, lng_ref[...], lnb_ref[...])
    x1_ref[0] = x1
    h2 = x1 * (1.0 + sc2) + sh2
    _rows_to_tiles(h2_ref.at[0], h2)
    logits = _bdot(h2, wr_ref[...])
    lane = lax.broadcasted_iota(jnp.int32, logits.shape, 1)
    logits = jnp.where(lane < N_EXPERTS, logits, NEG_BIG)
    e = jnp.exp(logits - jnp.max(logits, axis=1, keepdims=True))
    aff_ref[0] = e / jnp.sum(e, axis=1, keepdims=True)


def _merge_call(outs, gates, x_all, modv, w_br, w_out, ln_g, ln_b, w_router, alpha, n_lat_tiles, n_tiles):
    b_, s, d = x_all.shape
    ts = TOKEN_TILE
    tok = lambda w: pl.BlockSpec((1, ts, w), lambda b, j: (b, j, 0))
    const2 = lambda shape: pl.BlockSpec(shape, lambda b, j: (0,) * len(shape))
    return pl.pallas_call(
        functools.partial(_merge_kernel, alpha=alpha),
        grid=(b_, n_tiles),
        in_specs=[tok(BRANCH_W)] * 4 + [
            tok(GATE_W), tok(d),
            pl.BlockSpec((1, 1, 1, 6 * d), lambda b, j: (b, j // n_lat_tiles, 0, 0)),
            const2((N_BRANCH, BRANCH_W, d)), const2((d, d)), const2((1, d)), const2((1, d)),
            const2((d, LANES))],
        out_specs=[tok(d), pl.BlockSpec((1, ts, ROW_SUBLANES, LANES), lambda b, j: (b, j, 0, 0)), tok(LANES)],
        out_shape=[jax.ShapeDtypeStruct((b_, n_tiles * ts, d), F32),
                   jax.ShapeDtypeStruct((b_, n_tiles * ts, ROW_SUBLANES, LANES), F32),
                   jax.ShapeDtypeStruct((b_, n_tiles * ts, LANES), F32)],
        compiler_params=_cparams(("parallel", "arbitrary")),
        name="merge_ln1_router",
    )(*outs, gates, x_all, modv, w_br, w_out, ln_g, ln_b, w_router)


ROW_LOOP_UNROLL = 8
ROW_SUBLANES = D_MODEL // LANES


def _rows_to_tiles(ref, x):
    for sl in range(ROW_SUBLANES):
        ref[:, sl, :] = x[:, sl * LANES:(sl + 1) * LANES]


def _tiles_to_rows(ref):
    return jnp.concatenate([ref[:, sl, :] for sl in range(ROW_SUBLANES)], axis=1)


def _moe_kernel(idx_ref, gval_ref, h_ref, wg_ref, wu_ref, wd_ref, out_hbm, xg_ref, y_ref, acc_ref, sem,
                *, cap):
    e = pl.program_id(1)
    nb = h_ref.shape[0]

    @pl.when(e == 0)
    def _():
        acc_ref[...] = jnp.zeros_like(acc_ref)

    unroll = ROW_LOOP_UNROLL

    def for_each_slot_group(body):
        for bb in range(nb):
            def step(t, carry, bb=bb):
                body(bb, bb * cap + t * unroll)
                return carry
            lax.fori_loop(0, cap // unroll, step, 0)

    def gather(bb, j0):
        for k in range(unroll):
            xg_ref[j0 + k] = h_ref[bb, idx_ref[0, 0, 0, j0 + k]]

    for_each_slot_group(gather)
    xg = _tiles_to_rows(xg_ref).astype(BF16)
    a = jnp.dot(xg, wg_ref[0], preferred_element_type=F32)
    u = jnp.dot(xg, wu_ref[0], preferred_element_type=F32)
    _rows_to_tiles(y_ref, _bdot(_silu(a) * u, wd_ref[0]))

    def scatter(bb, j0):
        rows = [idx_ref[0, 0, 0, j0 + k] for k in range(unroll)]
        new = [acc_ref[bb, rows[k]] + gval_ref[0, 0, 0, j0 + k] * y_ref[j0 + k] for k in range(unroll)]
        for k in range(unroll):
            acc_ref[bb, rows[k]] = new[k]

    for_each_slot_group(scatter)

    @pl.when(e == pl.num_programs(1) - 1)
    def _():
        cp = pltpu.make_async_copy(acc_ref, out_hbm.at[pl.ds(pl.program_id(0) * nb, nb)], sem)
        cp.start()
        cp.wait()


def _moe_call(idx, gval, h2, wg, wu, wd, row_block, n_rows, group):
    g_, n_e, _, slots = idx.shape
    b_ = h2.shape[0]
    d, f = wg.shape[1], wg.shape[2]
    cap = slots // group
    tile = (ROW_SUBLANES, LANES)
    smem = lambda: pl.BlockSpec((1, 1, 1, slots), lambda g, e: (g, e, 0, 0), memory_space=pltpu.SMEM)
    return pl.pallas_call(
        functools.partial(_moe_kernel, cap=cap),
        grid=(g_, n_e),
        in_specs=[smem(), smem(),
                  pl.BlockSpec((group, n_rows) + tile, lambda g, e: (g, row_block, 0, 0),
                               pipeline_mode=pl.Buffered(1)),
                  pl.BlockSpec((1, d, f), lambda g, e: (e, 0, 0)),
                  pl.BlockSpec((1, d, f), lambda g, e: (e, 0, 0)),
                  pl.BlockSpec((1, f, d), lambda g, e: (e, 0, 0))],
        out_specs=pl.BlockSpec(memory_space=pl.ANY),
        out_shape=jax.ShapeDtypeStruct((b_, n_rows) + tile, F32),
        scratch_shapes=[pltpu.VMEM((slots,) + tile, F32), pltpu.VMEM((slots,) + tile, F32),
                        pltpu.VMEM((group, n_rows) + tile, F32), pltpu.SemaphoreType.DMA(())],
        compiler_params=_cparams(("arbitrary", "arbitrary"), vmem=60 * 1024 * 1024),
        name="moe_experts",
    )(idx, gval, h2, wg, wu, wd)


def _route(aff, n_lat, n_ctx, with_ctx):
    b_ = aff.shape[0]

    def pick(a, n):
        cap = max(1, CAPACITY_FACTOR * n // N_EXPERTS)
        gval, idx = lax.top_k(jnp.swapaxes(a, 1, 2), cap)
        return gval, idx.astype(jnp.int32), cap

    gl, il, cap_l = pick(aff[:, :n_lat, :N_EXPERTS], n_lat)
    lat = (il.reshape(b_, N_EXPERTS, 1, cap_l), gl.reshape(b_, N_EXPERTS, 1, cap_l))
    if not with_ctx:
        return lat, None
    gc, ic, cap_c = pick(aff[:, n_lat:, :N_EXPERTS], n_ctx)
    flat = lambda t: jnp.transpose(t, (1, 0, 2)).reshape(1, N_EXPERTS, 1, b_ * cap_c)
    return lat, (flat(ic), flat(gc))


def _ln2_kernel(x1_ref, fl_ref, fc_ref, mod_ref, g_ref, b_ref, o_ref, *, alpha, n_lat_tiles):
    d = D_MODEL
    j = pl.program_id(1)
    f = jnp.where(j < n_lat_tiles, _tiles_to_rows(fl_ref.at[0]), _tiles_to_rows(fc_ref.at[0]))
    g2 = mod_ref[0, 0, :, 5 * d:6 * d]
    o_ref[0] = _layer_norm(alpha * x1_ref[0] + g2 * f, g_ref[...], b_ref[...])


def _ln2_call(x1, f_lat, f_ctx, modv, ln_g, ln_b, alpha, n_lat_tiles, n_tiles):
    b_, s, d = x1.shape
    ts = TOKEN_TILE
    return pl.pallas_call(
        functools.partial(_ln2_kernel, alpha=alpha, n_lat_tiles=n_lat_tiles),
        grid=(b_, n_tiles),
        in_specs=[pl.BlockSpec((1, ts, d), lambda b, j: (b, j, 0)),
                  pl.BlockSpec((1, ts, ROW_SUBLANES, LANES), lambda b, j: (b, jnp.minimum(j, n_lat_tiles - 1), 0, 0)),
                  pl.BlockSpec((1, ts, ROW_SUBLANES, LANES), lambda b, j: (b, jnp.maximum(j - n_lat_tiles, 0), 0, 0)),
                  pl.BlockSpec((1, 1, 1, 6 * d), lambda b, j: (b, j // n_lat_tiles, 0, 0)),
                  pl.BlockSpec((1, d), lambda b, j: (0, 0)),
                  pl.BlockSpec((1, d), lambda b, j: (0, 0))],
        out_specs=pl.BlockSpec((1, ts, d), lambda b, j: (b, j, 0)),
        out_shape=jax.ShapeDtypeStruct((b_, n_tiles * ts, d), F32),
        compiler_params=_cparams(("parallel", "arbitrary")),
        name="ln2",
    )(x1, f_lat, f_ctx, modv, ln_g, ln_b)


def _rope_tables(n_lat, n_ctx, dim):
    nf = dim // 4
    t = jnp.arange(n_lat)
    row = (t // GRID_W).astype(F32)
    col = (t % GRID_W).astype(F32)
    inv = ROPE_THETA ** (-jnp.arange(nf, dtype=F32) / nf)
    ar = row[:, None] * inv
    ac = col[:, None] * inv
    cos = jnp.concatenate([jnp.cos(ar), jnp.cos(ar), jnp.cos(ac), jnp.cos(ac)], axis=1)
    sin = jnp.concatenate([-jnp.sin(ar), jnp.sin(ar), -jnp.sin(ac), jnp.sin(ac)], axis=1)
    reps = BRANCH_W // dim
    cos = jnp.concatenate([jnp.tile(cos, (1, reps)), jnp.ones((n_ctx, BRANCH_W), F32)], axis=0)
    sin = jnp.concatenate([jnp.tile(sin, (1, reps)), jnp.zeros((n_ctx, BRANCH_W), F32)], axis=0)
    return cos, sin


def _reorder_w_in(w_in):
    o_ab = 3 * BRANCH_W
    o_gate = o_ab + 4 * N_HEADS
    o_b = o_gate + BRANCH_W
    o_gates = o_b + 3 * BRANCH_W + 4 * BRANCH_W
    pad = jnp.zeros(w_in.shape[:2] + (LANES - 4 * N_HEADS,), w_in.dtype)
    return jnp.concatenate([w_in[..., o_gates:], w_in[..., :o_ab], w_in[..., o_ab:o_gate], pad,
                            w_in[..., o_gate:o_b], w_in[..., o_b:o_gates]], axis=-1).astype(BF16)


def _lane_vec(p):
    v = jnp.zeros((LANES,), F32)
    v = v.at[0:N_HEADS].set(p[0]).at[8:8 + N_HEADS].set(p[1])
    return v.reshape(1, LANES)


def kernel(x, c, ctx, c_ctx, w_mod, b_mod, w_in, conv_a, a_log, dt_bias, gdn_norm, diff_lambda, diff_norm,
           qk_norm_c, sink_d, w_br, w_out, ln1_g, ln1_b, w_router, w_gate_e, w_up_e, w_down_e, ln2_g, ln2_b):
    b_, n_lat, d = x.shape
    n_ctx = ctx.shape[1]
    depth = w_mod.shape[0]
    ts = TOKEN_TILE
    assert d == D_MODEL and n_lat % ts == 0 and n_ctx == ts and n_lat % GRID_W == 0
    assert n_lat >= ts + 2 * WINDOW or n_lat == ts
    n_lat_tiles = n_lat // ts
    n_tiles = n_lat_tiles + n_ctx // ts
    alpha = (2.0 * depth) ** 0.25

    rows = -(-(b_ + 1) // 8) * 8
    cc = jnp.zeros((rows, d), F32).at[:b_].set(c).at[b_].set(c_ctx)
    mod_all = _mod_call(cc, w_mod, b_mod)

    w_all = _reorder_w_in(w_in)
    w_br_b = w_br.astype(BF16)
    w_out_b = w_out.astype(BF16)
    w_r = jnp.concatenate([w_router, jnp.zeros((depth, d, LANES - N_EXPERTS), F32)], axis=-1).astype(BF16)
    wg_b, wu_b, wd_b = w_gate_e.astype(BF16), w_up_e.astype(BF16), w_down_e.astype(BF16)
    conv_w = jnp.concatenate([conv_a, jnp.zeros((depth, 8 - CONV_K, conv_a.shape[2]), F32)], axis=1)
    tabs = _rope_tables(n_lat, n_ctx, HEAD_DIM) + _rope_tables(n_lat, n_ctx, DH_B)

    x_all = jnp.concatenate([x, ctx], axis=1)
    for l in range(depth):
        last = l == depth - 1
        lam_init = 0.8 - 0.6 * math.exp(-0.3 * l)
        modv = jnp.stack([mod_all[l, :b_], jnp.broadcast_to(mod_all[l, b_], (b_, 6 * d))], axis=1)
        modv = modv.reshape(b_, 2, 1, 6 * d)
        gq = jnp.tile(qk_norm_c[l, 0], N_HEADS).reshape(1, BRANCH_W)
        gk = jnp.tile(qk_norm_c[l, 1], N_HEADS // 2).reshape(1, BRANCH_W // 2)

        (gates, aqkv, aab, agate, qb, kb, vb, qc, kc, vc, qd, kd, vd) = _inproj_call(
            x_all, modv, w_all[l], tabs, gq, gk, n_lat_tiles)

        qn, kn, vn, gcol, grow = _gdn_prep_call(aqkv, aab, conv_w[l], _lane_vec(a_log[l]), _lane_vec(dt_bias[l]),
                                                n_lat_tiles)
        local_out = _gdn_local_call(qn, kn, vn, gcol, grow)
        o_f, o_b = _gdn_scan_call(local_out, n_lat // CHUNK, n_ctx // CHUNK)
        out_a = _gdn_finish_call(o_f, o_b, agate, gdn_norm[l].reshape(1, HEAD_DIM))

        nq = n_lat_tiles if last else n_tiles
        out_b = _diff_attn_call(qb, kb, vb, diff_lambda[l], diff_norm[l].reshape(1, HEAD_DIM), lam_init,
                                n_lat, n_ctx, nq)
        out_c = _gqa_call(qc, kc, vc, n_lat, n_ctx, nq)
        out_d = _window_call(qd, kd, vd, sink_d[l], n_lat, n_ctx, nq)

        x1, h2, aff = _merge_call((out_a, out_b, out_c, out_d), gates, x_all, modv, w_br_b[l], w_out_b[l],
                                  ln1_g[l].reshape(1, d), ln1_b[l].reshape(1, d), w_r[l], alpha,
                                  n_lat_tiles, nq)

        route_lat, route_ctx = _route(aff, n_lat, n_ctx, not last)
        f_lat = _moe_call(route_lat[0], route_lat[1], h2, wg_b[l], wu_b[l], wd_b[l], 0, n_lat, 1)
        if last:
            f_ctx = f_lat
        else:
            f_ctx = _moe_call(route_ctx[0], route_ctx[1], h2, wg_b[l], wu_b[l], wd_b[l],
                              n_lat // n_ctx, n_ctx, b_)
        x_all = _ln2_call(x1, f_lat, f_ctx, modv, ln2_g[l].reshape(1, d), ln2_b[l].reshape(1, d), alpha,
                          n_lat_tiles, nq)
    return x_all[:, :n_lat]
```

```python
import functools
import math

import jax
import jax.numpy as jnp
from jax import lax
from jax.experimental import pallas as pl
from jax.experimental.pallas import tpu as pltpu

F32 = jnp.float32
BF16 = jnp.bfloat16

D_MODEL = 1024
HEAD_DIM = 64
N_HEADS = 4
BRANCH_W = 256
N_BRANCH = 4
GRID_W = 64
ROPE_THETA = 10000.0
EPS = 1e-6
CONV_K = 5
CHUNK = 64
DH_B = 32
WINDOW = 128
N_EXPERTS = 16
CAPACITY_FACTOR = 2
LOG2E = 1.4426950408889634

TOKEN_TILE = 256
LANES = 128
VMEM_LIMIT = 56 * 1024 * 1024

GATE_W = N_BRANCH * D_MODEL
C_GATES = 0
C_AQKV = C_GATES + GATE_W
C_AAB = C_AQKV + 3 * BRANCH_W
C_AGATE = C_AAB + LANES
C_B = C_AGATE + BRANCH_W
C_C = C_B + 3 * BRANCH_W
C_D = C_C + 2 * BRANCH_W
W_ALL = C_D + 2 * BRANCH_W


def _cparams(sem, vmem=VMEM_LIMIT):
    return pltpu.CompilerParams(dimension_semantics=sem, vmem_limit_bytes=vmem)


def _bdot(a, b):
    return jnp.dot(a.astype(BF16), b.astype(BF16), preferred_element_type=F32)


def _bdot_nt(a, b):
    return lax.dot_general(a.astype(BF16), b.astype(BF16), (((1,), (1,)), ((), ())),
                           preferred_element_type=F32)


def _silu(x):
    return x * jax.nn.sigmoid(x)


def _block_diag_ones(width, group):
    r = lax.broadcasted_iota(jnp.int32, (width, width), 0) // group
    c = lax.broadcasted_iota(jnp.int32, (width, width), 1) // group
    return jnp.where(r == c, 1.0, 0.0).astype(BF16)


def _group_sum(x, ones_bd):
    hi = x.astype(BF16)
    lo = (x - hi.astype(F32)).astype(BF16)
    return (jnp.dot(hi, ones_bd, preferred_element_type=F32)
            + jnp.dot(lo, ones_bd, preferred_element_type=F32))


def _pair_swap(x, q):
    w = x.shape[-1]
    lane = lax.broadcasted_iota(jnp.int32, x.shape, x.ndim - 1)
    fwd = pltpu.roll(x, w - q, x.ndim - 1)
    bwd = pltpu.roll(x, q, x.ndim - 1)
    return jnp.where((lane % (2 * q)) < q, fwd, bwd)


def _layer_norm(x, g, b):
    mu = jnp.mean(x, axis=-1, keepdims=True)
    xc = x - mu
    var = jnp.mean(xc * xc, axis=-1, keepdims=True)
    return xc * lax.rsqrt(var + EPS) * g + b


def _mod_kernel(c_ref, w_ref, b_ref, o_ref):
    o_ref[0] = _bdot(_silu(c_ref[...]), w_ref[0]) + b_ref[0]


def _mod_call(cc, w_mod, b_mod):
    depth, d, n6 = w_mod.shape
    rows = cc.shape[0]
    nb = 1536
    return pl.pallas_call(
        _mod_kernel,
        grid=(depth, n6 // nb),
        in_specs=[pl.BlockSpec((rows, d), lambda l, j: (0, 0)),
                  pl.BlockSpec((1, d, nb), lambda l, j: (l, 0, j)),
                  pl.BlockSpec((1, 1, nb), lambda l, j: (l, 0, j))],
        out_specs=pl.BlockSpec((1, rows, nb), lambda l, j: (l, 0, j)),
        out_shape=jax.ShapeDtypeStruct((depth, rows, n6), F32),
        compiler_params=_cparams(("arbitrary", "arbitrary")),
        name="mod_vectors",
    )(cc, w_mod, b_mod.reshape(depth, 1, n6))


def _rope(x, cos, sin, quarter):
    return x * cos + _pair_swap(x, quarter) * sin


def _store_heads(ref, x, n_heads):
    for h in range(n_heads):
        ref[0, h] = x[:, h * HEAD_DIM:(h + 1) * HEAD_DIM].astype(ref.dtype)


def _store_key_heads(ref, x, n_heads):
    xt = jnp.transpose(x)
    for h in range(n_heads):
        ref[0, h] = xt[h * HEAD_DIM:(h + 1) * HEAD_DIM, :].astype(ref.dtype)


def _store_value_heads(ref, x, n_heads):
    lane = lax.broadcasted_iota(jnp.int32, (x.shape[0], HEAD_DIM), 1)
    ones_col = jnp.where(lane == 0, 1.0, 0.0)
    for h in range(n_heads):
        ref[0, h] = jnp.concatenate([x[:, h * HEAD_DIM:(h + 1) * HEAD_DIM], ones_col], axis=1).astype(ref.dtype)


def _inproj_kernel(x_ref, mod_ref, w_ref, cos_hd_ref, sin_hd_ref, cos_b_ref, sin_b_ref, gq_ref, gk_ref,
                   gates_ref, aqkv_ref, aab_ref, agate_ref,
                   qb_ref, kb_ref, vb_ref, qc_ref, kc_ref, vc_ref, qd_ref, kd_ref, vd_ref):
    d = D_MODEL
    x = x_ref[0]
    sh1 = mod_ref[0, 0, :, 0:d]
    sc1 = mod_ref[0, 0, :, d:2 * d]
    h = (x * (1.0 + sc1) + sh1).astype(BF16)

    def proj(a, width):
        return jnp.dot(h, w_ref[:, a:a + width], preferred_element_type=F32)

    gates_ref[0] = proj(C_GATES, GATE_W).astype(gates_ref.dtype)
    aqkv_ref[0] = proj(C_AQKV, 3 * BRANCH_W)
    aab_ref[0] = proj(C_AAB, LANES)
    agate_ref[0] = proj(C_AGATE, BRANCH_W).astype(agate_ref.dtype)

    cos_hd, sin_hd = cos_hd_ref[...], sin_hd_ref[...]
    cos_b, sin_b = cos_b_ref[...], sin_b_ref[...]
    kv_w = BRANCH_W // 2

    pb = proj(C_B, 3 * BRANCH_W)
    qb = _rope(pb[:, 0:BRANCH_W], cos_b, sin_b, DH_B // 4) * (DH_B ** -0.5 * LOG2E)
    kb = _rope(pb[:, BRANCH_W:2 * BRANCH_W], cos_b, sin_b, DH_B // 4)
    _store_heads(qb_ref, qb, N_HEADS)
    _store_key_heads(kb_ref, kb, N_HEADS)
    _store_value_heads(vb_ref, pb[:, 2 * BRANCH_W:3 * BRANCH_W], N_HEADS)

    pc = proj(C_C, 2 * BRANCH_W)
    ones_q = _block_diag_ones(BRANCH_W, HEAD_DIM)
    ones_k = _block_diag_ones(kv_w, HEAD_DIM)
    qc = pc[:, 0:BRANCH_W]
    kc = pc[:, BRANCH_W:BRANCH_W + kv_w]
    qc = qc * lax.rsqrt(_group_sum(qc * qc, ones_q) * (1.0 / HEAD_DIM) + EPS) * gq_ref[...]
    kc = kc * lax.rsqrt(_group_sum(kc * kc, ones_k) * (1.0 / HEAD_DIM) + EPS) * gk_ref[...]
    qc = _rope(qc, cos_hd, sin_hd, HEAD_DIM // 4) * (HEAD_DIM ** -0.5 * LOG2E)
    kc = _rope(kc, cos_hd[:, 0:kv_w], sin_hd[:, 0:kv_w], HEAD_DIM // 4)
    _store_heads(qc_ref, qc, N_HEADS)
    _store_key_heads(kc_ref, kc, N_HEADS // 2)
    _store_value_heads(vc_ref, pc[:, BRANCH_W + kv_w:2 * BRANCH_W], N_HEADS // 2)

    pd = proj(C_D, 2 * BRANCH_W)
    qd = _rope(pd[:, 0:BRANCH_W], cos_hd, sin_hd, HEAD_DIM // 4) * (HEAD_DIM ** -0.5 * LOG2E)
    kd = _rope(pd[:, BRANCH_W:BRANCH_W + kv_w], cos_hd[:, 0:kv_w], sin_hd[:, 0:kv_w], HEAD_DIM // 4)
    _store_heads(qd_ref, qd, N_HEADS)
    _store_key_heads(kd_ref, kd, N_HEADS // 2)
    _store_value_heads(vd_ref, pd[:, BRANCH_W + kv_w:2 * BRANCH_W], N_HEADS // 2)


def _inproj_call(x_all, modv, w_all, tabs, gq, gk, n_lat_tiles):
    b_, s, d = x_all.shape
    ts = TOKEN_TILE
    nt = s // ts
    tok = lambda w: pl.BlockSpec((1, ts, w), lambda b, j: (b, j, 0))
    tab = pl.BlockSpec((ts, BRANCH_W), lambda b, j: (j, 0))
    head = lambda nh, w=HEAD_DIM: pl.BlockSpec((1, nh, ts, w), lambda b, j: (b, 0, j, 0))
    hshape = lambda nh, w=HEAD_DIM: jax.ShapeDtypeStruct((b_, nh, s, w), BF16)
    khead = lambda nh: pl.BlockSpec((1, nh, HEAD_DIM, ts), lambda b, j: (b, 0, 0, j))
    kshape = lambda nh: jax.ShapeDtypeStruct((b_, nh, HEAD_DIM, s), BF16)
    kvh = N_HEADS // 2
    return pl.pallas_call(
        _inproj_kernel,
        grid=(b_, nt),
        in_specs=[tok(d),
                  pl.BlockSpec((1, 1, 1, 6 * d), lambda b, j: (b, j // n_lat_tiles, 0, 0)),
                  pl.BlockSpec((d, W_ALL), lambda b, j: (0, 0), pipeline_mode=pl.Buffered(1)),
                  tab, tab, tab, tab,
                  pl.BlockSpec((1, BRANCH_W), lambda b, j: (0, 0)),
                  pl.BlockSpec((1, BRANCH_W // 2), lambda b, j: (0, 0))],
        out_specs=[tok(GATE_W), tok(3 * BRANCH_W), tok(LANES), tok(BRANCH_W),
                   head(N_HEADS), khead(N_HEADS), head(N_HEADS, LANES),
                   head(N_HEADS), khead(kvh), head(kvh, LANES),
                   head(N_HEADS), khead(kvh), head(kvh, LANES)],
        out_shape=[jax.ShapeDtypeStruct((b_, s, GATE_W), BF16),
                   jax.ShapeDtypeStruct((b_, s, 3 * BRANCH_W), F32),
                   jax.ShapeDtypeStruct((b_, s, LANES), F32),
                   jax.ShapeDtypeStruct((b_, s, BRANCH_W), BF16),
                   hshape(N_HEADS), kshape(N_HEADS), hshape(N_HEADS, LANES),
                   hshape(N_HEADS), kshape(kvh), hshape(kvh, LANES),
                   hshape(N_HEADS), kshape(kvh), hshape(kvh, LANES)],
        compiler_params=_cparams(("parallel", "arbitrary")),
        name="inproj",
    )(x_all, modv, w_all, *tabs, gq, gk)


NEG_BIG = -1e30


def _softmax_pv(streams, chunks, mask_fn=None):
    m = [st[3] for st in streams]
    acc = [None] * len(streams)
    for i, (q2, k_ref, v_ref, _) in enumerate(streams):
        for ks, n, tag in chunks:
            s = jnp.dot(q2, k_ref[:, pl.ds(ks, n)], preferred_element_type=F32)
            if mask_fn is not None:
                s = mask_fn(s, ks, n, tag)
            cm = jnp.max(s, axis=1, keepdims=True)
            m_new = cm if m[i] is None else jnp.maximum(m[i], cm)
            pv = jnp.dot(jnp.exp2(s - m_new).astype(BF16), v_ref[pl.ds(ks, n), :], preferred_element_type=F32)
            acc[i] = pv if acc[i] is None else acc[i] * jnp.exp2(m[i] - m_new) + pv
            m[i] = m_new
    out = []
    for i, st in enumerate(streams):
        l = acc[i][:, HEAD_DIM:HEAD_DIM + 1]
        if st[3] is not None:
            l = l + jnp.exp2(st[3] - m[i])
        out.append((acc[i][:, 0:HEAD_DIM], l))
    return out


def _key_chunks(start, total, size):
    out, a = [], 0
    while a < total:
        n = min(size, total - a)
        out.append((start + a, n, a))
        a += n
    return out


KEY_CHUNK = 2048


def _diff_attn_kernel(lam_ref, q_ref, k_ref, v_ref, g_ref, o_ref, *, n_lat, n_ctx, lam_init):
    tq = q_ref.shape[2]
    i = pl.program_id(2)
    lv = lam_ref[...]
    lam = (jnp.exp(jnp.sum(lv[0:1] * lv[1:2], axis=1, keepdims=True))
           - jnp.exp(jnp.sum(lv[2:3] * lv[3:4], axis=1, keepdims=True)) + lam_init)
    lane = lax.broadcasted_iota(jnp.int32, (tq, HEAD_DIM), 1)

    def run(chunks):
        streams = []
        for hh in range(N_HEADS):
            q = q_ref[0, hh]
            zero = jnp.zeros_like(q)
            q2 = jnp.concatenate([jnp.where(lane < DH_B, q, zero), jnp.where(lane < DH_B, zero, q)], axis=0)
            streams.append((q2, k_ref.at[0, hh], v_ref.at[0, hh], None))
        for hh, (acc, l) in enumerate(_softmax_pv(streams, chunks)):
            o = acc / l
            o = o[0:tq] - lam * o[tq:2 * tq]
            o = o * lax.rsqrt(jnp.mean(o * o, axis=1, keepdims=True) + EPS) * g_ref[...] * (1.0 - lam_init)
            o_ref[0, :, hh * HEAD_DIM:(hh + 1) * HEAD_DIM] = o.astype(o_ref.dtype)

    @pl.when(i * tq < n_lat)
    def _():
        run(_key_chunks(0, n_lat + n_ctx, KEY_CHUNK))

    @pl.when(i * tq >= n_lat)
    def _():
        run(_key_chunks(n_lat, n_ctx, KEY_CHUNK))


def _gqa_kernel(q_ref, k_ref, v_ref, o_ref, *, n_lat, n_ctx):
    tq = q_ref.shape[2]
    i = pl.program_id(2)

    def run(chunks):
        streams = [(jnp.concatenate([q_ref[0, 2 * g], q_ref[0, 2 * g + 1]], axis=0), k_ref.at[0, g], v_ref.at[0, g],
                    None) for g in range(N_HEADS // 2)]
        for g, (acc, l) in enumerate(_softmax_pv(streams, chunks)):
            o = acc / l
            for j in range(2):
                c0 = (2 * g + j) * HEAD_DIM
                o_ref[0, :, c0:c0 + HEAD_DIM] = o[j * tq:(j + 1) * tq].astype(o_ref.dtype)

    @pl.when(i * tq < n_lat)
    def _():
        run(_key_chunks(0, n_lat + n_ctx, KEY_CHUNK))

    @pl.when(i * tq >= n_lat)
    def _():
        run(_key_chunks(n_lat, n_ctx, KEY_CHUNK))


def _window_kernel(sink_ref, q_ref, k_ref, v_ref, o_ref, *, n_lat, n_ctx, win_keys):
    tq = q_ref.shape[2]
    i = pl.program_id(2)
    row = lax.broadcasted_iota(jnp.int32, (2 * tq, 1), 0)

    def run(chunks, mask):
        streams = []
        for g in range(N_HEADS // 2):
            sink = jnp.where(row < tq, sink_ref[2 * g], sink_ref[2 * g + 1]) * LOG2E
            q2 = jnp.concatenate([q_ref[0, 2 * g], q_ref[0, 2 * g + 1]], axis=0)
            streams.append((q2, k_ref.at[0, g], v_ref.at[0, g], sink))
        for g, (acc, l) in enumerate(_softmax_pv(streams, chunks, mask_fn=mask)):
            o = acc / l
            for j in range(2):
                c0 = (2 * g + j) * HEAD_DIM
                o_ref[0, :, c0:c0 + HEAD_DIM] = o[j * tq:(j + 1) * tq].astype(o_ref.dtype)

    @pl.when(i * tq < n_lat)
    def _():
        start = jnp.clip(i * tq - WINDOW, 0, n_lat - win_keys)
        start = pl.multiple_of(start, WINDOW)

        def mask(s, ks, n, col):
            if col < n_ctx:
                return s
            t = i * tq + lax.broadcasted_iota(jnp.int32, s.shape, 0) % tq
            sp = start + lax.broadcasted_iota(jnp.int32, s.shape, 1)
            return jnp.where(jnp.abs(t - sp) <= WINDOW, s, NEG_BIG)

        run([(n_lat, n_ctx, 0), (start, win_keys, n_ctx)], mask)

    @pl.when(i * tq >= n_lat)
    def _():
        run([(n_lat, n_ctx, 0)], None)


def _attn_specs(s, tq, q_heads, kv_heads):
    q_spec = pl.BlockSpec((1, q_heads, tq, HEAD_DIM), lambda b, g, i: (b, 0, i, 0))
    k_spec = pl.BlockSpec((1, kv_heads, HEAD_DIM, s), lambda b, g, i: (b, 0, 0, 0))
    v_spec = pl.BlockSpec((1, kv_heads, s, LANES), lambda b, g, i: (b, 0, 0, 0))
    o_spec = pl.BlockSpec((1, tq, BRANCH_W), lambda b, g, i: (b, i, 0))
    return q_spec, k_spec, v_spec, o_spec


def _diff_attn_call(q, k, v, lam_vecs, norm_g, lam_init, n_lat, n_ctx, n_q_tiles):
    b_, _, s, _ = q.shape
    tq = TOKEN_TILE
    q_spec, k_spec, v_spec, o_spec = _attn_specs(s, tq, N_HEADS, N_HEADS)
    return pl.pallas_call(
        functools.partial(_diff_attn_kernel, n_lat=n_lat, n_ctx=n_ctx, lam_init=lam_init),
        grid=(b_, 1, n_q_tiles),
        in_specs=[pl.BlockSpec((4, DH_B), lambda b, g, i: (0, 0)), q_spec, k_spec, v_spec,
                  pl.BlockSpec((1, HEAD_DIM), lambda b, g, i: (0, 0))],
        out_specs=o_spec,
        out_shape=jax.ShapeDtypeStruct((b_, n_q_tiles * tq, BRANCH_W), BF16),
        compiler_params=_cparams(("parallel", "arbitrary", "arbitrary")),
        name="diff_attn",
    )(lam_vecs, q, k, v, norm_g)


def _gqa_call(q, k, v, n_lat, n_ctx, n_q_tiles):
    b_, _, s, _ = q.shape
    tq = TOKEN_TILE
    q_spec, k_spec, v_spec, o_spec = _attn_specs(s, tq, N_HEADS, N_HEADS // 2)
    return pl.pallas_call(
        functools.partial(_gqa_kernel, n_lat=n_lat, n_ctx=n_ctx),
        grid=(b_, 1, n_q_tiles),
        in_specs=[q_spec, k_spec, v_spec],
        out_specs=o_spec,
        out_shape=jax.ShapeDtypeStruct((b_, n_q_tiles * tq, BRANCH_W), BF16),
        compiler_params=_cparams(("parallel", "arbitrary", "arbitrary")),
        name="gqa_attn",
    )(q, k, v)


def _window_call(q, k, v, sink, n_lat, n_ctx, n_q_tiles):
    b_, _, s, _ = q.shape
    tq = TOKEN_TILE
    win_keys = min(tq + 2 * WINDOW, n_lat)
    q_spec, k_spec, v_spec, o_spec = _attn_specs(s, tq, N_HEADS, N_HEADS // 2)
    return pl.pallas_call(
        functools.partial(_window_kernel, n_lat=n_lat, n_ctx=n_ctx, win_keys=win_keys),
        grid=(b_, 1, n_q_tiles),
        in_specs=[pl.BlockSpec(memory_space=pltpu.SMEM), q_spec, k_spec, v_spec],
        out_specs=o_spec,
        out_shape=jax.ShapeDtypeStruct((b_, n_q_tiles * tq, BRANCH_W), BF16),
        compiler_params=_cparams(("parallel", "arbitrary", "arbitrary")),
        name="window_attn",
    )(sink, q, k, v)


def _row_iota(shape):
    return lax.broadcasted_iota(jnp.int32, shape, 0)


def _chunk_scan(x, pos, forward):
    n = x.shape[0]
    d = 1
    while d < CHUNK:
        if forward:
            x = x + jnp.where(pos >= d, pltpu.roll(x, d, 0), 0.0)
        else:
            x = x + jnp.where(pos < CHUNK - d, pltpu.roll(x, n - d, 0), 0.0)
        d *= 2
    return x


def _gdn_prep_kernel(prev_ref, cur_ref, next_ref, ab_ref, convw_ref, alog_ref, dtb_ref,
                     q_ref, k_ref, v_ref, gcol_ref, grow_ref, *, n_lat_tiles):
    ts = cur_ref.shape[1]
    j = pl.program_id(1)
    has_prev = jnp.logical_and(j != 0, j != n_lat_tiles)
    has_next = jnp.logical_and(j != n_lat_tiles - 1, j < n_lat_tiles)
    cur = cur_ref[0]
    prev = jnp.where(has_prev, prev_ref[0], 0.0)
    nxt = jnp.where(has_next, next_ref[0], 0.0)
    rows = _row_iota(cur.shape)
    half = CONV_K // 2
    y = cur * convw_ref[half:half + 1, :]
    for d in range(1, half + 1):
        back = jnp.where(rows < d, pltpu.roll(prev, d, 0), pltpu.roll(cur, d, 0))
        fwd = jnp.where(rows >= ts - d, pltpu.roll(nxt, ts - d, 0), pltpu.roll(cur, ts - d, 0))
        y = y + back * convw_ref[half - d:half - d + 1, :] + fwd * convw_ref[half + d:half + d + 1, :]
    y = _silu(y)
    ones_bd = _block_diag_ones(BRANCH_W, HEAD_DIM)
    q = y[:, 0:BRANCH_W]
    k = y[:, BRANCH_W:2 * BRANCH_W]
    q = q * lax.rsqrt(_group_sum(q * q, ones_bd) + EPS) * (HEAD_DIM ** -0.5)
    k = k * lax.rsqrt(_group_sum(k * k, ones_bd) + EPS)
    _store_heads(q_ref, q, N_HEADS)
    _store_heads(k_ref, k, N_HEADS)
    _store_heads(v_ref, y[:, 2 * BRANCH_W:3 * BRANCH_W], N_HEADS)

    ab = ab_ref[0]
    lane = lax.broadcasted_iota(jnp.int32, ab.shape, 1)
    is_a = (lane % 8) < N_HEADS
    z = ab + dtb_ref[...]
    softplus = jnp.maximum(z, 0.0) + jnp.log(1.0 + jnp.exp(-jnp.abs(z)))
    lg = jnp.where(is_a, -jnp.exp(alog_ref[...]) * softplus, 0.0)
    pos = _row_iota(ab.shape) % CHUNK
    cum = jnp.where(lane < 8, _chunk_scan(lg, pos, True), _chunk_scan(lg, pos, False))
    gates = jnp.where(is_a, cum, jax.nn.sigmoid(ab))
    gcol_ref[0] = gates
    gt = jnp.transpose(gates)
    for c in range(ts // CHUNK):
        grow_ref[0, c] = gt[0:16, c * CHUNK:(c + 1) * CHUNK]


def _gdn_prep_call(aqkv, aab, conv_w, alog_vec, dtb_vec, n_lat_tiles):
    b_, s, _ = aqkv.shape
    ts = TOKEN_TILE
    nt = s // ts
    w3 = 3 * BRANCH_W
    head = pl.BlockSpec((1, N_HEADS, ts, HEAD_DIM), lambda b, j: (b, 0, j, 0))
    hshape = jax.ShapeDtypeStruct((b_, N_HEADS, s, HEAD_DIM), F32)
    cpt = ts // CHUNK
    return pl.pallas_call(
        functools.partial(_gdn_prep_kernel, n_lat_tiles=n_lat_tiles),
        grid=(b_, nt),
        in_specs=[pl.BlockSpec((1, ts, w3), lambda b, j: (b, jnp.maximum(j - 1, 0), 0)),
                  pl.BlockSpec((1, ts, w3), lambda b, j: (b, j, 0)),
                  pl.BlockSpec((1, ts, w3), lambda b, j: (b, jnp.minimum(j + 1, nt - 1), 0)),
                  pl.BlockSpec((1, ts, LANES), lambda b, j: (b, j, 0)),
                  pl.BlockSpec((8, w3), lambda b, j: (0, 0)),
                  pl.BlockSpec((1, LANES), lambda b, j: (0, 0)),
                  pl.BlockSpec((1, LANES), lambda b, j: (0, 0))],
        out_specs=[head, head, head,
                   pl.BlockSpec((1, ts, LANES), lambda b, j: (b, j, 0)),
                   pl.BlockSpec((1, cpt, 16, CHUNK), lambda b, j: (b, j, 0, 0))],
        out_shape=[hshape, hshape, hshape,
                   jax.ShapeDtypeStruct((b_, s, LANES), F32),
                   jax.ShapeDtypeStruct((b_, s // CHUNK, 16, CHUNK), F32)],
        compiler_params=_cparams(("parallel", "arbitrary")),
        name="gdn_prep",
    )(aqkv, aqkv, aqkv, aab, conv_w, alog_vec, dtb_vec)


def _gdn_local_kernel(q_ref, k_ref, v_ref, gcol_ref, grow_ref,
                      wq_f, p_f, kdt_f, ug_f, wq_b, p_b, kdt_b, ug_b):
    c = CHUNK
    n = N_HEADS * c
    r = lax.broadcasted_iota(jnp.int32, (n, n), 0)
    s_ = lax.broadcasted_iota(jnp.int32, (n, n), 1)
    same_head = (r // c) == (s_ // c)
    eye = jnp.where(r == s_, 1.0, 0.0)
    eye_c = eye[0:c, 0:c]
    tile = jnp.where(lax.broadcasted_iota(jnp.int32, (c, n), 0) == lax.broadcasted_iota(jnp.int32, (c, n), 1) % c,
                     1.0, 0.0)
    outs = ((wq_f, p_f, kdt_f, ug_f), (wq_b, p_b, kdt_b, ug_b))
    stack_cols = lambda a, lane0: jnp.concatenate([a[:, lane0 + h:lane0 + h + 1] for h in range(N_HEADS)], axis=0)
    chains = []
    for cc in range(q_ref.shape[2] // c):
        rows = slice(cc * c, (cc + 1) * c)
        q = q_ref[0, :, rows, :].reshape(n, HEAD_DIM)
        k = k_ref[0, :, rows, :].reshape(n, HEAD_DIM)
        v = v_ref[0, :, rows, :].reshape(n, HEAD_DIM)
        gcol = gcol_ref[0, rows, :]
        grow = grow_ref[0, cc]
        kk = _bdot_nt(k, k)
        qk = _bdot_nt(q, k)
        kt = _bdot_nt(eye_c, k)
        for direction in range(2):
            wq_ref, p_ref, kdt_ref, ug_ref = outs[direction]
            base = 8 * direction
            g_c = stack_cols(gcol, base)
            beta_c = stack_cols(gcol, base + N_HEADS)
            g_r = jnp.concatenate([grow[base + h:base + h + 1, :] for h in range(N_HEADS)], axis=1)
            end = c - 1 if direction == 0 else 0
            g_end_r = jnp.concatenate([jnp.broadcast_to(grow[base + h:base + h + 1, end:end + 1], (1, c))
                                       for h in range(N_HEADS)], axis=1)
            g_end_c = jnp.concatenate([jnp.broadcast_to(gcol[end:end + 1, base + h:base + h + 1], (c, 1))
                                       for h in range(N_HEADS)], axis=0)
            if direction == 0:
                incl, strict = same_head & (r >= s_), same_head & (r > s_)
            else:
                incl, strict = same_head & (r <= s_), same_head & (r < s_)
            decay = jnp.exp(jnp.where(incl, g_c - g_r, -jnp.inf))
            a_mat = jnp.where(strict, beta_c * kk * decay, 0.0)
            eg = jnp.exp(g_c)
            rhs = jnp.concatenate([beta_c * eg * k, beta_c * v], axis=1)
            p_ref[0, cc] = (qk * decay).astype(BF16)
            kdt = kt * jnp.exp(g_end_r - g_r)
            kdt_ref[0, cc] = jnp.where(same_head, jnp.concatenate([kdt] * N_HEADS, axis=0), 0.0).astype(BF16)
            ug_ref[0, cc, 1] = jnp.broadcast_to(jnp.exp(g_end_c), (n, HEAD_DIM))
            chains.append(dict(pw=-a_mat, t_inv=eye - a_mat, rhs=rhs, qg=q * eg, cc=cc, wq_ref=wq_ref, ug_ref=ug_ref))

    for _ in range(5):
        for ch in chains:
            ch["pw"] = _bdot(ch["pw"], ch["pw"])
        for ch in chains:
            ch["t_inv"] = ch["t_inv"] + _bdot(ch["t_inv"], ch["pw"])
    for ch in chains:
        cc, wq_ref, ug_ref = ch["cc"], ch["wq_ref"], ch["ug_ref"]
        sol = _bdot(ch["t_inv"], ch["rhs"])
        wq = jnp.concatenate([sol[:, 0:c], ch["qg"]], axis=0)
        wq_tiled = _bdot(wq, tile)
        wq_ref[0, cc, 0:n, :] = jnp.where(same_head, wq_tiled[0:n], 0.0).astype(BF16)
        wq_ref[0, cc, n:2 * n, :] = jnp.where(same_head, wq_tiled[n:2 * n], 0.0).astype(BF16)
        ug_ref[0, cc, 0] = sol[:, c:2 * c]


GDN_CHUNKS_PER_STEP = 4
GDN_SCAN_CHUNKS = 2
GDN_SCAN_BATCH = 4


def _gdn_local_call(qn, kn, vn, gcol, grow):
    b_, nh, s, _ = qn.shape
    nch = s // CHUNK
    cps = GDN_CHUNKS_PER_STEP
    n = nh * CHUNK
    head = pl.BlockSpec((1, nh, cps * CHUNK, HEAD_DIM), lambda b, c: (b, 0, c, 0))
    spec = lambda *shape: pl.BlockSpec((1, cps) + shape, lambda b, c: (b, c) + (0,) * len(shape))
    out_specs = [spec(2 * n, n), spec(n, n), spec(n, n), spec(2, n, HEAD_DIM)]
    out_shape = [jax.ShapeDtypeStruct((b_, nch, 2 * n, n), BF16), jax.ShapeDtypeStruct((b_, nch, n, n), BF16),
                 jax.ShapeDtypeStruct((b_, nch, n, n), BF16), jax.ShapeDtypeStruct((b_, nch, 2, n, HEAD_DIM), F32)]
    return pl.pallas_call(
        _gdn_local_kernel,
        grid=(b_, nch // cps),
        in_specs=[head, head, head,
                  pl.BlockSpec((1, cps * CHUNK, LANES), lambda b, c: (b, c, 0)),
                  pl.BlockSpec((1, cps, 16, CHUNK), lambda b, c: (b, c, 0, 0))],
        out_specs=out_specs * 2,
        out_shape=out_shape * 2,
        compiler_params=_cparams(("parallel", "arbitrary")),
        name="gdn_local",
    )(qn, kn, vn, gcol, grow)


def _gdn_scan_kernel(wq_f, p_f, kdt_f, ug_f, wq_b, p_b, kdt_b, ug_b, of_ref, ob_ref, state_ref):
    n = N_HEADS * CHUNK

    @pl.when(pl.program_id(1) == 0)
    def _():
        state_ref[...] = jnp.zeros_like(state_ref)

    nb, nc = wq_f.shape[0], wq_f.shape[1]
    dirs = ((wq_f, p_f, kdt_f, ug_f, of_ref), (wq_b, p_b, kdt_b, ug_b, ob_ref))
    chains = [(bb, d) for bb in range(nb) for d in range(2)]
    st = {ch: state_ref[ch[0], ch[1]] for ch in chains}
    for t in range(nc):
        cc = {ch: (t if ch[1] == 0 else nc - 1 - t) for ch in chains}
        ws = {ch: jnp.dot(dirs[ch[1]][0][ch[0], cc[ch]], st[ch].astype(BF16), preferred_element_type=F32)
              for ch in chains}
        u_b = {ch: (dirs[ch[1]][3][ch[0], cc[ch], 0] - ws[ch][0:n]).astype(BF16) for ch in chains}
        for ch in chains:
            bb, d = ch
            o = ws[ch][n:2 * n] + jnp.dot(dirs[d][1][bb, cc[ch]], u_b[ch], preferred_element_type=F32)
            for h in range(N_HEADS):
                dirs[d][4][bb, h, cc[ch] * CHUNK:(cc[ch] + 1) * CHUNK, :] = o[h * CHUNK:(h + 1) * CHUNK]
        st = {ch: dirs[ch[1]][3][ch[0], cc[ch], 1] * st[ch]
              + jnp.dot(dirs[ch[1]][2][ch[0], cc[ch]], u_b[ch], preferred_element_type=F32) for ch in chains}
    for ch in chains:
        state_ref[ch[0], ch[1]] = st[ch]


def _gdn_scan_call(local_out, n_lat_chunks, n_ctx_chunks):
    wq = local_out[0]
    b_, nch = wq.shape[0], wq.shape[1]
    nh = N_HEADS
    n = nh * CHUNK
    s = nch * CHUNK
    nc = GDN_SCAN_CHUNKS
    assert n_lat_chunks % nc == 0 and n_ctx_chunks % nc == 0
    nlc, ncc = n_lat_chunks // nc, n_ctx_chunks // nc

    def order_f(i):
        return jnp.where(i < ncc, nlc + i, i - ncc)

    def order_b(i):
        return jnp.where(i < ncc, nlc + ncc - 1 - i, nlc - 1 - (i - ncc))

    nb = max(g for g in (GDN_SCAN_BATCH, 2, 1) if b_ % g == 0)

    def specs(order):
        sp = lambda *shape: pl.BlockSpec((nb, nc) + shape, lambda b, i: (b, order(i)) + (0,) * len(shape))
        return [sp(2 * n, n), sp(n, n), sp(n, n), sp(2, n, HEAD_DIM)]

    oo = lambda order: pl.BlockSpec((nb, nh, nc * CHUNK, HEAD_DIM), lambda b, i: (b, 0, order(i), 0))
    oshape = jax.ShapeDtypeStruct((b_, nh, s, HEAD_DIM), F32)
    return pl.pallas_call(
        _gdn_scan_kernel,
        grid=(b_ // nb, nch // nc),
        in_specs=specs(order_f) + specs(order_b),
        out_specs=[oo(order_f), oo(order_b)],
        out_shape=[oshape, oshape],
        scratch_shapes=[pltpu.VMEM((nb, 2, n, HEAD_DIM), F32)],
        compiler_params=_cparams(("parallel", "arbitrary")),
        name="gdn_scan",
    )(*local_out)


def _gdn_finish_kernel(of_ref, ob_ref, gate_ref, g_ref, o_ref):
    gate = gate_ref[0].astype(F32)
    for h in range(N_HEADS):
        o = of_ref[0, h] + ob_ref[0, h]
        o = o * lax.rsqrt(jnp.mean(o * o, axis=1, keepdims=True) + EPS) * g_ref[...]
        o = o * _silu(gate[:, h * HEAD_DIM:(h + 1) * HEAD_DIM])
        o_ref[0, :, h * HEAD_DIM:(h + 1) * HEAD_DIM] = o.astype(o_ref.dtype)


def _gdn_finish_call(o_f, o_b, agate, norm_g):
    b_, nh, s, _ = o_f.shape
    ts = TOKEN_TILE
    head = pl.BlockSpec((1, nh, ts, HEAD_DIM), lambda b, j: (b, 0, j, 0))
    tok = pl.BlockSpec((1, ts, BRANCH_W), lambda b, j: (b, j, 0))
    return pl.pallas_call(
        _gdn_finish_kernel,
        grid=(b_, s // ts),
        in_specs=[head, head, tok, pl.BlockSpec((1, HEAD_DIM), lambda b, j: (0, 0))],
        out_specs=tok,
        out_shape=jax.ShapeDtypeStruct((b_, s, BRANCH_W), BF16),
        compiler_params=_cparams(("parallel", "arbitrary")),
        name="gdn_finish",
    )(o_f, o_b, agate, norm_g)


def _merge_kernel(oa_ref, ob_ref, oc_ref, od_ref, gates_ref, x_ref, mod_ref, wbr_ref, wout_ref,
                  lng_ref, lnb_ref, wr_ref, x1_ref, h2_ref, aff_ref, *, alpha):
    d = D_MODEL
    m = None
    for i, o_ref in enumerate((oa_ref, ob_ref, oc_ref, od_ref)):
        gate = jax.nn.sigmoid(gates_ref[0, :, i * d:(i + 1) * d].astype(F32))
        term = gate * jnp.dot(o_ref[0], wbr_ref[i], preferred_element_type=F32)
        m = term if m is None else m + term
    y = _bdot(m, wout_ref[...])
    g1 = mod_ref[0, 0, :, 2 * d:3 * d]
    sh2 = mod_ref[0, 0, :, 3 * d:4 * d]
    sc2 = mod_ref[0, 0, :, 4 * d:5 * d]
    x1 = _layer_norm(alpha * x_ref[0] + g1 * y, lng_ref[...], lnb_ref[...])
    x1_ref[0] = x1
    h2 = x1 * (1.0 + sc2) + sh2
    _rows_to_tiles(h2_ref.at[0], h2)
    logits = _bdot(h2, wr_ref[...])
    lane = lax.broadcasted_iota(jnp.int32, logits.shape, 1)
    logits = jnp.where(lane < N_EXPERTS, logits, NEG_BIG)
    e = jnp.exp(logits - jnp.max(logits, axis=1, keepdims=True))
    aff_ref[0] = e / jnp.sum(e, axis=1, keepdims=True)


def _merge_call(outs, gates, x_all, modv, w_br, w_out, ln_g, ln_b, w_router, alpha, n_lat_tiles, n_tiles):
    b_, s, d = x_all.shape
    ts = TOKEN_TILE
    tok = lambda w: pl.BlockSpec((1, ts, w), lambda b, j: (b, j, 0))
    const2 = lambda shape: pl.BlockSpec(shape, lambda b, j: (0,) * len(shape))
    return pl.pallas_call(
        functools.partial(_merge_kernel, alpha=alpha),
        grid=(b_, n_tiles),
        in_specs=[tok(BRANCH_W)] * 4 + [
            tok(GATE_W), tok(d),
            pl.BlockSpec((1, 1, 1, 6 * d), lambda b, j: (b, j // n_lat_tiles, 0, 0)),
            const2((N_BRANCH, BRANCH_W, d)), const2((d, d)), const2((1, d)), const2((1, d)),
            const2((d, LANES))],
        out_specs=[tok(d), pl.BlockSpec((1, ts, ROW_SUBLANES, LANES), lambda b, j: (b, j, 0, 0)), tok(LANES)],
        out_shape=[jax.ShapeDtypeStruct((b_, n_tiles * ts, d), F32),
                   jax.ShapeDtypeStruct((b_, n_tiles * ts, ROW_SUBLANES, LANES), F32),
                   jax.ShapeDtypeStruct((b_, n_tiles * ts, LANES), F32)],
        compiler_params=_cparams(("parallel", "arbitrary")),
        name="merge_ln1_router",
    )(*outs, gates, x_all, modv, w_br, w_out, ln_g, ln_b, w_router)


ROW_LOOP_UNROLL = 8
ROW_SUBLANES = D_MODEL // LANES


def _rows_to_tiles(ref, x):
    for sl in range(ROW_SUBLANES):
        ref[:, sl, :] = x[:, sl * LANES:(sl + 1) * LANES]


def _tiles_to_rows(ref):
    return jnp.concatenate([ref[:, sl, :] for sl in range(ROW_SUBLANES)], axis=1)


def _moe_kernel(idx_prev, idx_cur, idx_next, gv_prev, gv_cur, h_ref, wg_ref, wu_ref, wd_ref, out_hbm,
                xg_a, xg_b, y_a, y_b, acc_ref, sem, *, cap):
    e = pl.program_id(1)
    n_e = pl.num_programs(1)
    nb = h_ref.shape[0]
    slots = xg_a.shape[0]
    unroll = ROW_LOOP_UNROLL

    def gather(idx_ref, xg_ref):
        for j in range(slots):
            xg_ref[j] = h_ref[j // cap, idx_ref[0, 0, 0, j]]

    def scatter(idx_ref, gv_ref, y_ref):
        for j0 in range(0, slots, unroll):
            bb = j0 // cap
            rows = [idx_ref[0, 0, 0, j0 + k] for k in range(unroll)]
            new = [acc_ref[bb, rows[k]] + gv_ref[0, 0, 0, j0 + k] * y_ref[j0 + k] for k in range(unroll)]
            for k in range(unroll):
                acc_ref[bb, rows[k]] = new[k]

    def expert(xg_ref, y_ref):
        half = slots // 2
        for r0 in (0, half):
            xg = _tiles_to_rows(xg_ref.at[r0:r0 + half]).astype(BF16)
            a = jnp.dot(xg, wg_ref[0], preferred_element_type=F32)
            u = jnp.dot(xg, wu_ref[0], preferred_element_type=F32)
            _rows_to_tiles(y_ref.at[r0:r0 + half], _bdot(_silu(a) * u, wd_ref[0]))

    @pl.when(e == 0)
    def _():
        acc_ref[...] = jnp.zeros_like(acc_ref)
        y_b[...] = jnp.zeros_like(y_b)
        gather(idx_cur, xg_a)

    def step(xg_cur, y_cur, xg_nxt, y_prv):
        scatter(idx_prev, gv_prev, y_prv)
        expert(xg_cur, y_cur)
        gather(idx_next, xg_nxt)

    @pl.when(e % 2 == 0)
    def _():
        step(xg_a, y_a, xg_b, y_b)

    @pl.when(e % 2 == 1)
    def _():
        step(xg_b, y_b, xg_a, y_a)

    @pl.when(e == n_e - 1)
    def _():
        scatter(idx_cur, gv_cur, y_b)
        cp = pltpu.make_async_copy(acc_ref, out_hbm.at[pl.ds(pl.program_id(0) * nb, nb)], sem)
        cp.start()
        cp.wait()


def _moe_call(idx, gval, h2, wg, wu, wd, row_block, n_rows, group):
    g_, n_e, _, slots = idx.shape
    assert n_e % 2 == 0 and slots % (2 * ROW_LOOP_UNROLL) == 0
    b_ = h2.shape[0]
    d, f = wg.shape[1], wg.shape[2]
    cap = slots // group
    tile = (ROW_SUBLANES, LANES)
    smem = lambda shift: pl.BlockSpec((1, 1, 1, slots), lambda g, e: (g, jnp.clip(e + shift, 0, n_e - 1), 0, 0),
                                      memory_space=pltpu.SMEM)
    rows_buf = pltpu.VMEM((slots,) + tile, F32)
    return pl.pallas_call(
        functools.partial(_moe_kernel, cap=cap),
        grid=(g_, n_e),
        in_specs=[smem(-1), smem(0), smem(1), smem(-1), smem(0),
                  pl.BlockSpec((group, n_rows) + tile, lambda g, e: (g, row_block, 0, 0),
                               pipeline_mode=pl.Buffered(1)),
                  pl.BlockSpec((1, d, f), lambda g, e: (e, 0, 0)),
                  pl.BlockSpec((1, d, f), lambda g, e: (e, 0, 0)),
                  pl.BlockSpec((1, f, d), lambda g, e: (e, 0, 0))],
        out_specs=pl.BlockSpec(memory_space=pl.ANY),
        out_shape=jax.ShapeDtypeStruct((b_, n_rows) + tile, F32),
        scratch_shapes=[rows_buf, rows_buf, rows_buf, rows_buf,
                        pltpu.VMEM((group, n_rows) + tile, F32), pltpu.SemaphoreType.DMA(())],
        compiler_params=_cparams(("arbitrary", "arbitrary"), vmem=60 * 1024 * 1024),
        name="moe_experts",
    )(idx, idx, idx, gval, gval, h2, wg, wu, wd)


def _route(aff, n_lat, n_ctx, with_ctx):
    b_ = aff.shape[0]

    def pick(a, n):
        cap = max(1, CAPACITY_FACTOR * n // N_EXPERTS)
        gval, idx = lax.top_k(jnp.swapaxes(a, 1, 2), cap)
        return gval, idx.astype(jnp.int32), cap

    gl, il, cap_l = pick(aff[:, :n_lat, :N_EXPERTS], n_lat)
    lat = (il.reshape(b_, N_EXPERTS, 1, cap_l), gl.reshape(b_, N_EXPERTS, 1, cap_l))
    if not with_ctx:
        return lat, None
    gc, ic, cap_c = pick(aff[:, n_lat:, :N_EXPERTS], n_ctx)
    flat = lambda t: jnp.transpose(t, (1, 0, 2)).reshape(1, N_EXPERTS, 1, b_ * cap_c)
    return lat, (flat(ic), flat(gc))


def _ln2_kernel(x1_ref, fl_ref, fc_ref, mod_ref, g_ref, b_ref, o_ref, *, alpha, n_lat_tiles):
    d = D_MODEL
    j = pl.program_id(1)
    f = jnp.where(j < n_lat_tiles, _tiles_to_rows(fl_ref.at[0]), _tiles_to_rows(fc_ref.at[0]))
    g2 = mod_ref[0, 0, :, 5 * d:6 * d]
    o_ref[0] = _layer_norm(alpha * x1_ref[0] + g2 * f, g_ref[...], b_ref[...])


def _ln2_call(x1, f_lat, f_ctx, modv, ln_g, ln_b, alpha, n_lat_tiles, n_tiles):
    b_, s, d = x1.shape
    ts = TOKEN_TILE
    return pl.pallas_call(
        functools.partial(_ln2_kernel, alpha=alpha, n_lat_tiles=n_lat_tiles),
        grid=(b_, n_tiles),
        in_specs=[pl.BlockSpec((1, ts, d), lambda b, j: (b, j, 0)),
                  pl.BlockSpec((1, ts, ROW_SUBLANES, LANES), lambda b, j: (b, jnp.minimum(j, n_lat_tiles - 1), 0, 0)),
                  pl.BlockSpec((1, ts, ROW_SUBLANES, LANES), lambda b, j: (b, jnp.maximum(j - n_lat_tiles, 0), 0, 0)),
                  pl.BlockSpec((1, 1, 1, 6 * d), lambda b, j: (b, j // n_lat_tiles, 0, 0)),
                  pl.BlockSpec((1, d), lambda b, j: (0, 0)),
                  pl.BlockSpec((1, d), lambda b, j: (0, 0))],
        out_specs=pl.BlockSpec((1, ts, d), lambda b, j: (b, j, 0)),
        out_shape=jax.ShapeDtypeStruct((b_, n_tiles * ts, d), F32),
        compiler_params=_cparams(("parallel", "arbitrary")),
        name="ln2",
    )(x1, f_lat, f_ctx, modv, ln_g, ln_b)


def _rope_tables(n_lat, n_ctx, dim):
    nf = dim // 4
    t = jnp.arange(n_lat)
    row = (t // GRID_W).astype(F32)
    col = (t % GRID_W).astype(F32)
    inv = ROPE_THETA ** (-jnp.arange(nf, dtype=F32) / nf)
    ar = row[:, None] * inv
    ac = col[:, None] * inv
    cos = jnp.concatenate([jnp.cos(ar), jnp.cos(ar), jnp.cos(ac), jnp.cos(ac)], axis=1)
    sin = jnp.concatenate([-jnp.sin(ar), jnp.sin(ar), -jnp.sin(ac), jnp.sin(ac)], axis=1)
    reps = BRANCH_W // dim
    cos = jnp.concatenate([jnp.tile(cos, (1, reps)), jnp.ones((n_ctx, BRANCH_W), F32)], axis=0)
    sin = jnp.concatenate([jnp.tile(sin, (1, reps)), jnp.zeros((n_ctx, BRANCH_W), F32)], axis=0)
    return cos, sin


def _reorder_w_in(w_in):
    o_ab = 3 * BRANCH_W
    o_gate = o_ab + 4 * N_HEADS
    o_b = o_gate + BRANCH_W
    o_gates = o_b + 3 * BRANCH_W + 4 * BRANCH_W
    pad = jnp.zeros(w_in.shape[:2] + (LANES - 4 * N_HEADS,), w_in.dtype)
    return jnp.concatenate([w_in[..., o_gates:], w_in[..., :o_ab], w_in[..., o_ab:o_gate], pad,
                            w_in[..., o_gate:o_b], w_in[..., o_b:o_gates]], axis=-1).astype(BF16)


def _lane_vec(p):
    v = jnp.zeros((LANES,), F32)
    v = v.at[0:N_HEADS].set(p[0]).at[8:8 + N_HEADS].set(p[1])
    return v.reshape(1, LANES)


def kernel(x, c, ctx, c_ctx, w_mod, b_mod, w_in, conv_a, a_log, dt_bias, gdn_norm, diff_lambda, diff_norm,
           qk_norm_c, sink_d, w_br, w_out, ln1_g, ln1_b, w_router, w_gate_e, w_up_e, w_down_e, ln2_g, ln2_b):
    b_, n_lat, d = x.shape
    n_ctx = ctx.shape[1]
    depth = w_mod.shape[0]
    ts = TOKEN_TILE
    assert d == D_MODEL and n_lat % ts == 0 and n_ctx == ts and n_lat % GRID_W == 0
    assert n_lat >= ts + 2 * WINDOW or n_lat == ts
    n_lat_tiles = n_lat // ts
    n_tiles = n_lat_tiles + n_ctx // ts
    alpha = (2.0 * depth) ** 0.25

    rows = -(-(b_ + 1) // 8) * 8
    cc = jnp.zeros((rows, d), F32).at[:b_].set(c).at[b_].set(c_ctx)
    mod_all = _mod_call(cc, w_mod, b_mod)

    w_all = _reorder_w_in(w_in)
    w_br_b = w_br.astype(BF16)
    w_out_b = w_out.astype(BF16)
    w_r = jnp.concatenate([w_router, jnp.zeros((depth, d, LANES - N_EXPERTS), F32)], axis=-1).astype(BF16)
    wg_b, wu_b, wd_b = w_gate_e.astype(BF16), w_up_e.astype(BF16), w_down_e.astype(BF16)
    conv_w = jnp.concatenate([conv_a, jnp.zeros((depth, 8 - CONV_K, conv_a.shape[2]), F32)], axis=1)
    tabs = _rope_tables(n_lat, n_ctx, HEAD_DIM) + _rope_tables(n_lat, n_ctx, DH_B)

    x_all = jnp.concatenate([x, ctx], axis=1)
    for l in range(depth):
        last = l == depth - 1
        lam_init = 0.8 - 0.6 * math.exp(-0.3 * l)
        modv = jnp.stack([mod_all[l, :b_], jnp.broadcast_to(mod_all[l, b_], (b_, 6 * d))], axis=1)
        modv = modv.reshape(b_, 2, 1, 6 * d)
        gq = jnp.tile(qk_norm_c[l, 0], N_HEADS).reshape(1, BRANCH_W)
        gk = jnp.tile(qk_norm_c[l, 1], N_HEADS // 2).reshape(1, BRANCH_W // 2)

        (gates, aqkv, aab, agate, qb, kb, vb, qc, kc, vc, qd, kd, vd) = _inproj_call(
            x_all, modv, w_all[l], tabs, gq, gk, n_lat_tiles)

        qn, kn, vn, gcol, grow = _gdn_prep_call(aqkv, aab, conv_w[l], _lane_vec(a_log[l]), _lane_vec(dt_bias[l]),
                                                n_lat_tiles)
        local_out = _gdn_local_call(qn, kn, vn, gcol, grow)
        o_f, o_b = _gdn_scan_call(local_out, n_lat // CHUNK, n_ctx // CHUNK)
        out_a = _gdn_finish_call(o_f, o_b, agate, gdn_norm[l].reshape(1, HEAD_DIM))

        nq = n_lat_tiles if last else n_tiles
        out_b = _diff_attn_call(qb, kb, vb, diff_lambda[l], diff_norm[l].reshape(1, HEAD_DIM), lam_init,
                                n_lat, n_ctx, nq)
        out_c = _gqa_call(qc, kc, vc, n_lat, n_ctx, nq)
        out_d = _window_call(qd, kd, vd, sink_d[l], n_lat, n_ctx, nq)

        x1, h2, aff = _merge_call((out_a, out_b, out_c, out_d), gates, x_all, modv, w_br_b[l], w_out_b[l],
                                  ln1_g[l].reshape(1, d), ln1_b[l].reshape(1, d), w_r[l], alpha,
                                  n_lat_tiles, nq)

        route_lat, route_ctx = _route(aff, n_lat, n_ctx, not last)
        f_lat = _moe_call(route_lat[0], route_lat[1], h2, wg_b[l], wu_b[l], wd_b[l], 0, n_lat, 1)
        if last:
            f_ctx = f_lat
        else:
            f_ctx = _moe_call(route_ctx[0], route_ctx[1], h2, wg_b[l], wu_b[l], wd_b[l],
                              n_lat // n_ctx, n_ctx, b_)
        x_all = _ln2_call(x1, f_lat, f_ctx, modv, ln2_g[l].reshape(1, d), ln2_b[l].reshape(1, d), alpha,
                          n_lat_tiles, nq)
    return x_all[:, :n_lat]
```

```python
import functools
import math

import jax
import jax.numpy as jnp
from jax import lax
from jax.experimental import pallas as pl
from jax.experimental.pallas import tpu as pltpu

F32 = jnp.float32
BF16 = jnp.bfloat16

D_MODEL = 1024
HEAD_DIM = 64
N_HEADS = 4
BRANCH_W = 256
N_BRANCH = 4
GRID_W = 64
ROPE_THETA = 10000.0
EPS = 1e-6
CONV_K = 5
CHUNK = 64
DH_B = 32
WINDOW = 128
N_EXPERTS = 16
CAPACITY_FACTOR = 2
LOG2E = 1.4426950408889634

TOKEN_TILE = 256
LANES = 128
VMEM_LIMIT = 56 * 1024 * 1024

GATE_W = N_BRANCH * D_MODEL
C_GATES = 0
C_AQKV = C_GATES + GATE_W
C_AAB = C_AQKV + 3 * BRANCH_W
C_AGATE = C_AAB + LANES
C_B = C_AGATE + BRANCH_W
C_C = C_B + 3 * BRANCH_W
C_D = C_C + 2 * BRANCH_W
W_ALL = C_D + 2 * BRANCH_W


def _cparams(sem, vmem=VMEM_LIMIT):
    return pltpu.CompilerParams(dimension_semantics=sem, vmem_limit_bytes=vmem)


def _bdot(a, b):
    return jnp.dot(a.astype(BF16), b.astype(BF16), preferred_element_type=F32)


def _bdot_nt(a, b):
    return lax.dot_general(a.astype(BF16), b.astype(BF16), (((1,), (1,)), ((), ())),
                           preferred_element_type=F32)


def _silu(x):
    return x * jax.nn.sigmoid(x)


def _block_diag_ones(width, group):
    r = lax.broadcasted_iota(jnp.int32, (width, width), 0) // group
    c = lax.broadcasted_iota(jnp.int32, (width, width), 1) // group
    return jnp.where(r == c, 1.0, 0.0).astype(BF16)


def _group_sum(x, ones_bd):
    hi = x.astype(BF16)
    lo = (x - hi.astype(F32)).astype(BF16)
    return (jnp.dot(hi, ones_bd, preferred_element_type=F32)
            + jnp.dot(lo, ones_bd, preferred_element_type=F32))


def _pair_swap(x, q):
    w = x.shape[-1]
    lane = lax.broadcasted_iota(jnp.int32, x.shape, x.ndim - 1)
    fwd = pltpu.roll(x, w - q, x.ndim - 1)
    bwd = pltpu.roll(x, q, x.ndim - 1)
    return jnp.where((lane % (2 * q)) < q, fwd, bwd)


def _layer_norm(x, g, b):
    mu = jnp.mean(x, axis=-1, keepdims=True)
    xc = x - mu
    var = jnp.mean(xc * xc, axis=-1, keepdims=True)
    return xc * lax.rsqrt(var + EPS) * g + b


def _mod_kernel(c_ref, w_ref, b_ref, o_ref):
    o_ref[0] = _bdot(_silu(c_ref[...]), w_ref[0]) + b_ref[0]


def _mod_call(cc, w_mod, b_mod):
    depth, d, n6 = w_mod.shape
    rows = cc.shape[0]
    nb = 1536
    return pl.pallas_call(
        _mod_kernel,
        grid=(depth, n6 // nb),
        in_specs=[pl.BlockSpec((rows, d), lambda l, j: (0, 0)),
                  pl.BlockSpec((1, d, nb), lambda l, j: (l, 0, j)),
                  pl.BlockSpec((1, 1, nb), lambda l, j: (l, 0, j))],
        out_specs=pl.BlockSpec((1, rows, nb), lambda l, j: (l, 0, j)),
        out_shape=jax.ShapeDtypeStruct((depth, rows, n6), F32),
        compiler_params=_cparams(("arbitrary", "arbitrary")),
        name="mod_vectors",
    )(cc, w_mod, b_mod.reshape(depth, 1, n6))


def _rope(x, cos, sin, quarter):
    return x * cos + _pair_swap(x, quarter) * sin


def _store_heads(ref, x, n_heads):
    for h in range(n_heads):
        ref[0, h] = x[:, h * HEAD_DIM:(h + 1) * HEAD_DIM].astype(ref.dtype)


def _store_key_heads(ref, x, n_heads):
    xt = jnp.transpose(x)
    for h in range(n_heads):
        ref[0, h] = xt[h * HEAD_DIM:(h + 1) * HEAD_DIM, :].astype(ref.dtype)


def _store_value_heads(ref, x, n_heads):
    lane = lax.broadcasted_iota(jnp.int32, (x.shape[0], HEAD_DIM), 1)
    ones_col = jnp.where(lane == 0, 1.0, 0.0)
    for h in range(n_heads):
        ref[0, h] = jnp.concatenate([x[:, h * HEAD_DIM:(h + 1) * HEAD_DIM], ones_col], axis=1).astype(ref.dtype)


def _inproj_kernel(*refs, prev_ln, alpha, n_lat_tiles):
    d = D_MODEL
    if prev_ln:
        x1_ref, fl_ref, fc_ref, pmod_ref, plg_ref, plb_ref = refs[:6]
        refs, xo_ref = refs[6:-1], refs[-1]
        f = jnp.where(pl.program_id(1) < n_lat_tiles, _tiles_to_rows(fl_ref.at[0]), _tiles_to_rows(fc_ref.at[0]))
        x = _layer_norm(alpha * x1_ref[0] + pmod_ref[0, 0, :, 5 * d:6 * d] * f, plg_ref[...], plb_ref[...])
        xo_ref[0] = x
    else:
        x, refs = refs[0][0], refs[1:]
    (mod_ref, w_ref, cos_hd_ref, sin_hd_ref, cos_b_ref, sin_b_ref, gq_ref, gk_ref,
     gates_ref, aqkv_ref, aab_ref, agate_ref,
     qb_ref, kb_ref, vb_ref, qc_ref, kc_ref, vc_ref, qd_ref, kd_ref, vd_ref) = refs
    sh1 = mod_ref[0, 0, :, 0:d]
    sc1 = mod_ref[0, 0, :, d:2 * d]
    h = (x * (1.0 + sc1) + sh1).astype(BF16)

    def proj(a, width):
        return jnp.dot(h, w_ref[:, a:a + width], preferred_element_type=F32)

    gates_ref[0] = proj(C_GATES, GATE_W).astype(gates_ref.dtype)
    aqkv_ref[0] = proj(C_AQKV, 3 * BRANCH_W)
    aab_ref[0] = proj(C_AAB, LANES)
    agate_ref[0] = proj(C_AGATE, BRANCH_W).astype(agate_ref.dtype)

    cos_hd, sin_hd = cos_hd_ref[...], sin_hd_ref[...]
    cos_b, sin_b = cos_b_ref[...], sin_b_ref[...]
    kv_w = BRANCH_W // 2

    pb = proj(C_B, 3 * BRANCH_W)
    qb = _rope(pb[:, 0:BRANCH_W], cos_b, sin_b, DH_B // 4) * (DH_B ** -0.5 * LOG2E)
    kb = _rope(pb[:, BRANCH_W:2 * BRANCH_W], cos_b, sin_b, DH_B // 4)
    _store_heads(qb_ref, qb, N_HEADS)
    _store_key_heads(kb_ref, kb, N_HEADS)
    _store_value_heads(vb_ref, pb[:, 2 * BRANCH_W:3 * BRANCH_W], N_HEADS)

    pc = proj(C_C, 2 * BRANCH_W)
    ones_q = _block_diag_ones(BRANCH_W, HEAD_DIM)
    ones_k = _block_diag_ones(kv_w, HEAD_DIM)
    qc = pc[:, 0:BRANCH_W]
    kc = pc[:, BRANCH_W:BRANCH_W + kv_w]
    qc = qc * lax.rsqrt(_group_sum(qc * qc, ones_q) * (1.0 / HEAD_DIM) + EPS) * gq_ref[...]
    kc = kc * lax.rsqrt(_group_sum(kc * kc, ones_k) * (1.0 / HEAD_DIM) + EPS) * gk_ref[...]
    qc = _rope(qc, cos_hd, sin_hd, HEAD_DIM // 4) * (HEAD_DIM ** -0.5 * LOG2E)
    kc = _rope(kc, cos_hd[:, 0:kv_w], sin_hd[:, 0:kv_w], HEAD_DIM // 4)
    _store_heads(qc_ref, qc, N_HEADS)
    _store_key_heads(kc_ref, kc, N_HEADS // 2)
    _store_value_heads(vc_ref, pc[:, BRANCH_W + kv_w:2 * BRANCH_W], N_HEADS // 2)

    pd = proj(C_D, 2 * BRANCH_W)
    qd = _rope(pd[:, 0:BRANCH_W], cos_hd, sin_hd, HEAD_DIM // 4) * (HEAD_DIM ** -0.5 * LOG2E)
    kd = _rope(pd[:, BRANCH_W:BRANCH_W + kv_w], cos_hd[:, 0:kv_w], sin_hd[:, 0:kv_w], HEAD_DIM // 4)
    _store_heads(qd_ref, qd, N_HEADS)
    _store_key_heads(kd_ref, kd, N_HEADS // 2)
    _store_value_heads(vd_ref, pd[:, BRANCH_W + kv_w:2 * BRANCH_W], N_HEADS // 2)


def _inproj_call(stream, modv, w_all, tabs, gq, gk, n_lat_tiles, alpha):
    prev_ln = len(stream) > 1
    b_, s, d = stream[0].shape
    ts = TOKEN_TILE
    nt = s // ts
    vec = pl.BlockSpec((1, d), lambda b, j: (0, 0))
    tile_rows = lambda index: pl.BlockSpec((1, ts, ROW_SUBLANES, LANES), index)
    mod_spec = pl.BlockSpec((1, 1, 1, 6 * d), lambda b, j: (b, j // n_lat_tiles, 0, 0))
    tok = lambda w: pl.BlockSpec((1, ts, w), lambda b, j: (b, j, 0))
    tab = pl.BlockSpec((ts, BRANCH_W), lambda b, j: (j, 0))
    head = lambda nh, w=HEAD_DIM: pl.BlockSpec((1, nh, ts, w), lambda b, j: (b, 0, j, 0))
    hshape = lambda nh, w=HEAD_DIM: jax.ShapeDtypeStruct((b_, nh, s, w), BF16)
    khead = lambda nh: pl.BlockSpec((1, nh, HEAD_DIM, ts), lambda b, j: (b, 0, 0, j))
    kshape = lambda nh: jax.ShapeDtypeStruct((b_, nh, HEAD_DIM, s), BF16)
    kvh = N_HEADS // 2
    stream_specs = [tok(d)]
    if prev_ln:
        stream_specs += [tile_rows(lambda b, j: (b, jnp.minimum(j, n_lat_tiles - 1), 0, 0)),
                         tile_rows(lambda b, j: (b, jnp.maximum(j - n_lat_tiles, 0), 0, 0)),
                         mod_spec, vec, vec]
    return pl.pallas_call(
        functools.partial(_inproj_kernel, prev_ln=prev_ln, alpha=alpha, n_lat_tiles=n_lat_tiles),
        grid=(b_, nt),
        in_specs=stream_specs + [
                  mod_spec,
                  pl.BlockSpec((d, W_ALL), lambda b, j: (0, 0), pipeline_mode=pl.Buffered(1)),
                  tab, tab, tab, tab,
                  pl.BlockSpec((1, BRANCH_W), lambda b, j: (0, 0)),
                  pl.BlockSpec((1, BRANCH_W // 2), lambda b, j: (0, 0))],
        out_specs=[tok(GATE_W), tok(3 * BRANCH_W), tok(LANES), tok(BRANCH_W),
                   head(N_HEADS), khead(N_HEADS), head(N_HEADS, LANES),
                   head(N_HEADS), khead(kvh), head(kvh, LANES),
                   head(N_HEADS), khead(kvh), head(kvh, LANES)] + ([tok(d)] if prev_ln else []),
        out_shape=[jax.ShapeDtypeStruct((b_, s, GATE_W), BF16),
                   jax.ShapeDtypeStruct((b_, s, 3 * BRANCH_W), F32),
                   jax.ShapeDtypeStruct((b_, s, LANES), F32),
                   jax.ShapeDtypeStruct((b_, s, BRANCH_W), BF16),
                   hshape(N_HEADS), kshape(N_HEADS), hshape(N_HEADS, LANES),
                   hshape(N_HEADS), kshape(kvh), hshape(kvh, LANES),
                   hshape(N_HEADS), kshape(kvh), hshape(kvh, LANES)]
                  + ([jax.ShapeDtypeStruct((b_, s, d), F32)] if prev_ln else []),
        compiler_params=_cparams(("parallel", "arbitrary")),
        name="inproj",
    )(*stream, modv, w_all, *tabs, gq, gk)


NEG_BIG = -1e30


def _softmax_pv(streams, chunks, mask_fn=None):
    m = [st[3] for st in streams]
    acc = [None] * len(streams)
    for i, (q2, k_ref, v_ref, _) in enumerate(streams):
        for ks, n, tag in chunks:
            s = jnp.dot(q2, k_ref[:, pl.ds(ks, n)], preferred_element_type=F32)
            if mask_fn is not None:
                s = mask_fn(s, ks, n, tag)
            cm = jnp.max(s, axis=1, keepdims=True)
            m_new = cm if m[i] is None else jnp.maximum(m[i], cm)
            pv = jnp.dot(jnp.exp2(s - m_new).astype(BF16), v_ref[pl.ds(ks, n), :], preferred_element_type=F32)
            acc[i] = pv if acc[i] is None else acc[i] * jnp.exp2(m[i] - m_new) + pv
            m[i] = m_new
    out = []
    for i, st in enumerate(streams):
        l = acc[i][:, HEAD_DIM:HEAD_DIM + 1]
        if st[3] is not None:
            l = l + jnp.exp2(st[3] - m[i])
        out.append((acc[i][:, 0:HEAD_DIM], l))
    return out


def _key_chunks(start, total, size):
    out, a = [], 0
    while a < total:
        n = min(size, total - a)
        out.append((start + a, n, a))
        a += n
    return out


KEY_CHUNK = 2048


def _diff_attn_kernel(lam_ref, q_ref, k_ref, v_ref, g_ref, o_ref, *, n_lat, n_ctx, lam_init):
    tq = q_ref.shape[2]
    i = pl.program_id(2)
    lv = lam_ref[...]
    lam = (jnp.exp(jnp.sum(lv[0:1] * lv[1:2], axis=1, keepdims=True))
           - jnp.exp(jnp.sum(lv[2:3] * lv[3:4], axis=1, keepdims=True)) + lam_init)
    lane = lax.broadcasted_iota(jnp.int32, (tq, HEAD_DIM), 1)

    def run(chunks):
        streams = []
        for hh in range(N_HEADS):
            q = q_ref[0, hh]
            zero = jnp.zeros_like(q)
            q2 = jnp.concatenate([jnp.where(lane < DH_B, q, zero), jnp.where(lane < DH_B, zero, q)], axis=0)
            streams.append((q2, k_ref.at[0, hh], v_ref.at[0, hh], None))
        for hh, (acc, l) in enumerate(_softmax_pv(streams, chunks)):
            o = acc / l
            o = o[0:tq] - lam * o[tq:2 * tq]
            o = o * lax.rsqrt(jnp.mean(o * o, axis=1, keepdims=True) + EPS) * g_ref[...] * (1.0 - lam_init)
            o_ref[0, :, hh * HEAD_DIM:(hh + 1) * HEAD_DIM] = o.astype(o_ref.dtype)

    @pl.when(i * tq < n_lat)
    def _():
        run(_key_chunks(0, n_lat + n_ctx, KEY_CHUNK))

    @pl.when(i * tq >= n_lat)
    def _():
        run(_key_chunks(n_lat, n_ctx, KEY_CHUNK))


def _gqa_kernel(q_ref, k_ref, v_ref, o_ref, *, n_lat, n_ctx):
    tq = q_ref.shape[2]
    i = pl.program_id(2)

    def run(chunks):
        streams = [(jnp.concatenate([q_ref[0, 2 * g], q_ref[0, 2 * g + 1]], axis=0), k_ref.at[0, g], v_ref.at[0, g],
                    None) for g in range(N_HEADS // 2)]
        for g, (acc, l) in enumerate(_softmax_pv(streams, chunks)):
            o = acc / l
            for j in range(2):
                c0 = (2 * g + j) * HEAD_DIM
                o_ref[0, :, c0:c0 + HEAD_DIM] = o[j * tq:(j + 1) * tq].astype(o_ref.dtype)

    @pl.when(i * tq < n_lat)
    def _():
        run(_key_chunks(0, n_lat + n_ctx, KEY_CHUNK))

    @pl.when(i * tq >= n_lat)
    def _():
        run(_key_chunks(n_lat, n_ctx, KEY_CHUNK))


def _window_kernel(sink_ref, q_ref, k_ref, v_ref, o_ref, *, n_lat, n_ctx, win_keys):
    tq = q_ref.shape[2]
    i = pl.program_id(2)
    row = lax.broadcasted_iota(jnp.int32, (2 * tq, 1), 0)

    def run(chunks, mask):
        streams = []
        for g in range(N_HEADS // 2):
            sink = jnp.where(row < tq, sink_ref[2 * g], sink_ref[2 * g + 1]) * LOG2E
            q2 = jnp.concatenate([q_ref[0, 2 * g], q_ref[0, 2 * g + 1]], axis=0)
            streams.append((q2, k_ref.at[0, g], v_ref.at[0, g], sink))
        for g, (acc, l) in enumerate(_softmax_pv(streams, chunks, mask_fn=mask)):
            o = acc / l
            for j in range(2):
                c0 = (2 * g + j) * HEAD_DIM
                o_ref[0, :, c0:c0 + HEAD_DIM] = o[j * tq:(j + 1) * tq].astype(o_ref.dtype)

    @pl.when(i * tq < n_lat)
    def _():
        start = jnp.clip(i * tq - WINDOW, 0, n_lat - win_keys)
        start = pl.multiple_of(start, WINDOW)

        def mask(s, ks, n, tag):
            if tag < n_ctx:
                return s
            t = i * tq + lax.broadcasted_iota(jnp.int32, s.shape, 0) % tq
            sp = start + lax.broadcasted_iota(jnp.int32, s.shape, 1)
            return jnp.where(jnp.abs(t - sp) <= WINDOW, s, NEG_BIG)

        run([(n_lat, n_ctx, 0), (start, win_keys, n_ctx)], mask)

    @pl.when(i * tq >= n_lat)
    def _():
        run([(n_lat, n_ctx, 0)], None)


def _attn_specs(s, tq, q_heads, kv_heads):
    q_spec = pl.BlockSpec((1, q_heads, tq, HEAD_DIM), lambda b, g, i: (b, 0, i, 0))
    k_spec = pl.BlockSpec((1, kv_heads, HEAD_DIM, s), lambda b, g, i: (b, 0, 0, 0))
    v_spec = pl.BlockSpec((1, kv_heads, s, LANES), lambda b, g, i: (b, 0, 0, 0))
    o_spec = pl.BlockSpec((1, tq, BRANCH_W), lambda b, g, i: (b, i, 0))
    return q_spec, k_spec, v_spec, o_spec


def _diff_attn_call(q, k, v, lam_vecs, norm_g, lam_init, n_lat, n_ctx, n_q_tiles):
    b_, _, s, _ = q.shape
    tq = TOKEN_TILE
    q_spec, k_spec, v_spec, o_spec = _attn_specs(s, tq, N_HEADS, N_HEADS)
    return pl.pallas_call(
        functools.partial(_diff_attn_kernel, n_lat=n_lat, n_ctx=n_ctx, lam_init=lam_init),
        grid=(b_, 1, n_q_tiles),
        in_specs=[pl.BlockSpec((4, DH_B), lambda b, g, i: (0, 0)), q_spec, k_spec, v_spec,
                  pl.BlockSpec((1, HEAD_DIM), lambda b, g, i: (0, 0))],
        out_specs=o_spec,
        out_shape=jax.ShapeDtypeStruct((b_, n_q_tiles * tq, BRANCH_W), BF16),
        compiler_params=_cparams(("parallel", "arbitrary", "arbitrary")),
        name="diff_attn",
    )(lam_vecs, q, k, v, norm_g)


def _gqa_call(q, k, v, n_lat, n_ctx, n_q_tiles):
    b_, _, s, _ = q.shape
    tq = TOKEN_TILE
    q_spec, k_spec, v_spec, o_spec = _attn_specs(s, tq, N_HEADS, N_HEADS // 2)
    return pl.pallas_call(
        functools.partial(_gqa_kernel, n_lat=n_lat, n_ctx=n_ctx),
        grid=(b_, 1, n_q_tiles),
        in_specs=[q_spec, k_spec, v_spec],
        out_specs=o_spec,
        out_shape=jax.ShapeDtypeStruct((b_, n_q_tiles * tq, BRANCH_W), BF16),
        compiler_params=_cparams(("parallel", "arbitrary", "arbitrary")),
        name="gqa_attn",
    )(q, k, v)


def _window_call(q, k, v, sink, n_lat, n_ctx, n_q_tiles):
    b_, _, s, _ = q.shape
    tq = TOKEN_TILE
    win_keys = min(tq + 2 * WINDOW, n_lat)
    q_spec, k_spec, v_spec, o_spec = _attn_specs(s, tq, N_HEADS, N_HEADS // 2)
    return pl.pallas_call(
        functools.partial(_window_kernel, n_lat=n_lat, n_ctx=n_ctx, win_keys=win_keys),
        grid=(b_, 1, n_q_tiles),
        in_specs=[pl.BlockSpec(memory_space=pltpu.SMEM), q_spec, k_spec, v_spec],
        out_specs=o_spec,
        out_shape=jax.ShapeDtypeStruct((b_, n_q_tiles * tq, BRANCH_W), BF16),
        compiler_params=_cparams(("parallel", "arbitrary", "arbitrary")),
        name="window_attn",
    )(sink, q, k, v)


def _row_iota(shape):
    return lax.broadcasted_iota(jnp.int32, shape, 0)


def _chunk_scan(x, pos, forward):
    n = x.shape[0]
    d = 1
    while d < CHUNK:
        if forward:
            x = x + jnp.where(pos >= d, pltpu.roll(x, d, 0), 0.0)
        else:
            x = x + jnp.where(pos < CHUNK - d, pltpu.roll(x, n - d, 0), 0.0)
        d *= 2
    return x


HALO_ROWS = 8


def _gdn_prep_kernel(prev_ref, cur_ref, next_ref, ab_ref, convw_ref, alog_ref, dtb_ref,
                     q_ref, k_ref, v_ref, gcol_ref, grow_ref, *, n_lat_tiles):
    ts = cur_ref.shape[1]
    j = pl.program_id(1)
    has_prev = jnp.logical_and(j != 0, j != n_lat_tiles)
    has_next = jnp.logical_and(j != n_lat_tiles - 1, j < n_lat_tiles)
    cur = cur_ref[0]
    prev8 = jnp.where(has_prev, prev_ref[0], 0.0)
    next8 = jnp.where(has_next, next_ref[0], 0.0)
    reps = ts // HALO_ROWS
    rows = _row_iota(cur.shape)
    half = CONV_K // 2
    y = cur * convw_ref[half:half + 1, :]
    for d in range(1, half + 1):
        back = jnp.where(rows < d, jnp.concatenate([pltpu.roll(prev8, d, 0)] * reps, axis=0), pltpu.roll(cur, d, 0))
        fwd = jnp.where(rows >= ts - d, jnp.concatenate([pltpu.roll(next8, HALO_ROWS - d, 0)] * reps, axis=0),
                        pltpu.roll(cur, ts - d, 0))
        y = y + back * convw_ref[half - d:half - d + 1, :] + fwd * convw_ref[half + d:half + d + 1, :]
    y = _silu(y)
    ones_bd = _block_diag_ones(BRANCH_W, HEAD_DIM)
    q = y[:, 0:BRANCH_W]
    k = y[:, BRANCH_W:2 * BRANCH_W]
    q = q * lax.rsqrt(_group_sum(q * q, ones_bd) + EPS) * (HEAD_DIM ** -0.5)
    k = k * lax.rsqrt(_group_sum(k * k, ones_bd) + EPS)
    _store_heads(q_ref, q, N_HEADS)
    _store_heads(k_ref, k, N_HEADS)
    _store_heads(v_ref, y[:, 2 * BRANCH_W:3 * BRANCH_W], N_HEADS)

    ab = ab_ref[0]
    lane = lax.broadcasted_iota(jnp.int32, ab.shape, 1)
    is_a = (lane % 8) < N_HEADS
    z = ab + dtb_ref[...]
    softplus = jnp.maximum(z, 0.0) + jnp.log(1.0 + jnp.exp(-jnp.abs(z)))
    lg = jnp.where(is_a, -jnp.exp(alog_ref[...]) * softplus, 0.0)
    pos = _row_iota(ab.shape) % CHUNK
    cum = jnp.where(lane < 8, _chunk_scan(lg, pos, True), _chunk_scan(lg, pos, False))
    gates = jnp.where(is_a, cum, jax.nn.sigmoid(ab))
    gcol_ref[0] = gates
    gt = jnp.transpose(gates)
    for c in range(ts // CHUNK):
        grow_ref[0, c] = gt[0:16, c * CHUNK:(c + 1) * CHUNK]


def _gdn_prep_call(aqkv, aab, conv_w, alog_vec, dtb_vec, n_lat_tiles):
    b_, s, _ = aqkv.shape
    ts = TOKEN_TILE
    nt = s // ts
    w3 = 3 * BRANCH_W
    head = pl.BlockSpec((1, N_HEADS, ts, HEAD_DIM), lambda b, j: (b, 0, j, 0))
    hshape = jax.ShapeDtypeStruct((b_, N_HEADS, s, HEAD_DIM), F32)
    cpt = ts // CHUNK
    hpt = ts // HALO_ROWS
    return pl.pallas_call(
        functools.partial(_gdn_prep_kernel, n_lat_tiles=n_lat_tiles),
        grid=(b_, nt),
        in_specs=[pl.BlockSpec((1, HALO_ROWS, w3), lambda b, j: (b, jnp.maximum(j * hpt - 1, 0), 0)),
                  pl.BlockSpec((1, ts, w3), lambda b, j: (b, j, 0)),
                  pl.BlockSpec((1, HALO_ROWS, w3), lambda b, j: (b, jnp.minimum((j + 1) * hpt, nt * hpt - 1), 0)),
                  pl.BlockSpec((1, ts, LANES), lambda b, j: (b, j, 0)),
                  pl.BlockSpec((8, w3), lambda b, j: (0, 0)),
                  pl.BlockSpec((1, LANES), lambda b, j: (0, 0)),
                  pl.BlockSpec((1, LANES), lambda b, j: (0, 0))],
        out_specs=[head, head, head,
                   pl.BlockSpec((1, ts, LANES), lambda b, j: (b, j, 0)),
                   pl.BlockSpec((1, cpt, 16, CHUNK), lambda b, j: (b, j, 0, 0))],
        out_shape=[hshape, hshape, hshape,
                   jax.ShapeDtypeStruct((b_, s, LANES), F32),
                   jax.ShapeDtypeStruct((b_, s // CHUNK, 16, CHUNK), F32)],
        compiler_params=_cparams(("parallel", "arbitrary")),
        name="gdn_prep",
    )(aqkv, aqkv, aqkv, aab, conv_w, alog_vec, dtb_vec)


def _gdn_local_kernel(q_ref, k_ref, v_ref, gcol_ref, grow_ref,
                      wq_f, p_f, kdt_f, ug_f, wq_b, p_b, kdt_b, ug_b):
    c = CHUNK
    n = N_HEADS * c
    r = lax.broadcasted_iota(jnp.int32, (n, n), 0)
    s_ = lax.broadcasted_iota(jnp.int32, (n, n), 1)
    same_head = (r // c) == (s_ // c)
    eye = jnp.where(r == s_, 1.0, 0.0)
    eye_c = eye[0:c, 0:c]
    tile = jnp.where(lax.broadcasted_iota(jnp.int32, (c, n), 0) == lax.broadcasted_iota(jnp.int32, (c, n), 1) % c,
                     1.0, 0.0)
    outs = ((wq_f, p_f, kdt_f, ug_f), (wq_b, p_b, kdt_b, ug_b))
    stack_cols = lambda a, lane0: jnp.concatenate([a[:, lane0 + h:lane0 + h + 1] for h in range(N_HEADS)], axis=0)
    chains = []
    for cc in range(q_ref.shape[2] // c):
        rows = slice(cc * c, (cc + 1) * c)
        q = q_ref[0, :, rows, :].reshape(n, HEAD_DIM)
        k = k_ref[0, :, rows, :].reshape(n, HEAD_DIM)
        v = v_ref[0, :, rows, :].reshape(n, HEAD_DIM)
        gcol = gcol_ref[0, rows, :]
        grow = grow_ref[0, cc]
        kk = _bdot_nt(k, k)
        qk = _bdot_nt(q, k)
        kt = _bdot_nt(eye_c, k)
        for direction in range(2):
            wq_ref, p_ref, kdt_ref, ug_ref = outs[direction]
            base = 8 * direction
            g_c = stack_cols(gcol, base)
            beta_c = stack_cols(gcol, base + N_HEADS)
            g_r = jnp.concatenate([grow[base + h:base + h + 1, :] for h in range(N_HEADS)], axis=1)
            end = c - 1 if direction == 0 else 0
            g_end_r = jnp.concatenate([jnp.broadcast_to(grow[base + h:base + h + 1, end:end + 1], (1, c))
                                       for h in range(N_HEADS)], axis=1)
            g_end_c = jnp.concatenate([jnp.broadcast_to(gcol[end:end + 1, base + h:base + h + 1], (c, 1))
                                       for h in range(N_HEADS)], axis=0)
            if direction == 0:
                incl, strict = same_head & (r >= s_), same_head & (r > s_)
            else:
                incl, strict = same_head & (r <= s_), same_head & (r < s_)
            decay = jnp.exp(jnp.where(incl, g_c - g_r, -jnp.inf))
            a_mat = jnp.where(strict, beta_c * kk * decay, 0.0)
            eg = jnp.exp(g_c)
            rhs = jnp.concatenate([beta_c * eg * k, beta_c * v], axis=1)
            p_ref[0, cc] = (qk * decay).astype(BF16)
            kdt = kt * jnp.exp(g_end_r - g_r)
            kdt_ref[0, cc] = jnp.where(same_head, jnp.concatenate([kdt] * N_HEADS, axis=0), 0.0).astype(BF16)
            ug_ref[0, cc, 1] = jnp.broadcast_to(jnp.exp(g_end_c), (n, HEAD_DIM))
            chains.append(dict(pw=-a_mat, t_inv=eye - a_mat, rhs=rhs, qg=q * eg, cc=cc, wq_ref=wq_ref, ug_ref=ug_ref))

    for _ in range(5):
        for ch in chains:
            ch["pw"] = _bdot(ch["pw"], ch["pw"])
        for ch in chains:
            ch["t_inv"] = ch["t_inv"] + _bdot(ch["t_inv"], ch["pw"])
    for ch in chains:
        cc, wq_ref, ug_ref = ch["cc"], ch["wq_ref"], ch["ug_ref"]
        sol = _bdot(ch["t_inv"], ch["rhs"])
        wq = jnp.concatenate([sol[:, 0:c], ch["qg"]], axis=0)
        wq_tiled = _bdot(wq, tile)
        wq_ref[0, cc, 0:n, :] = jnp.where(same_head, wq_tiled[0:n], 0.0).astype(BF16)
        wq_ref[0, cc, n:2 * n, :] = jnp.where(same_head, wq_tiled[n:2 * n], 0.0).astype(BF16)
        ug_ref[0, cc, 0] = sol[:, c:2 * c]


GDN_CHUNKS_PER_STEP = 4
GDN_SCAN_CHUNKS = 2
GDN_SCAN_BATCH = 4


def _gdn_local_call(qn, kn, vn, gcol, grow):
    b_, nh, s, _ = qn.shape
    nch = s // CHUNK
    cps = GDN_CHUNKS_PER_STEP
    n = nh * CHUNK
    head = pl.BlockSpec((1, nh, cps * CHUNK, HEAD_DIM), lambda b, c: (b, 0, c, 0))
    spec = lambda *shape: pl.BlockSpec((1, cps) + shape, lambda b, c: (b, c) + (0,) * len(shape))
    out_specs = [spec(2 * n, n), spec(n, n), spec(n, n), spec(2, n, HEAD_DIM)]
    out_shape = [jax.ShapeDtypeStruct((b_, nch, 2 * n, n), BF16), jax.ShapeDtypeStruct((b_, nch, n, n), BF16),
                 jax.ShapeDtypeStruct((b_, nch, n, n), BF16), jax.ShapeDtypeStruct((b_, nch, 2, n, HEAD_DIM), F32)]
    return pl.pallas_call(
        _gdn_local_kernel,
        grid=(b_, nch // cps),
        in_specs=[head, head, head,
                  pl.BlockSpec((1, cps * CHUNK, LANES), lambda b, c: (b, c, 0)),
                  pl.BlockSpec((1, cps, 16, CHUNK), lambda b, c: (b, c, 0, 0))],
        out_specs=out_specs * 2,
        out_shape=out_shape * 2,
        compiler_params=_cparams(("parallel", "arbitrary")),
        name="gdn_local",
    )(qn, kn, vn, gcol, grow)


def _gdn_scan_kernel(wq_f, p_f, kdt_f, ug_f, wq_b, p_b, kdt_b, ug_b, of_ref, ob_ref, state_ref):
    n = N_HEADS * CHUNK

    @pl.when(pl.program_id(1) == 0)
    def _():
        state_ref[...] = jnp.zeros_like(state_ref)

    nb, nc = wq_f.shape[0], wq_f.shape[1]
    dirs = ((wq_f, p_f, kdt_f, ug_f, of_ref), (wq_b, p_b, kdt_b, ug_b, ob_ref))
    chains = [(bb, d) for bb in range(nb) for d in range(2)]
    st = {ch: state_ref[ch[0], ch[1]] for ch in chains}
    for t in range(nc):
        cc = {ch: (t if ch[1] == 0 else nc - 1 - t) for ch in chains}
        ws = {ch: jnp.dot(dirs[ch[1]][0][ch[0], cc[ch]], st[ch].astype(BF16), preferred_element_type=F32)
              for ch in chains}
        u_b = {ch: (dirs[ch[1]][3][ch[0], cc[ch], 0] - ws[ch][0:n]).astype(BF16) for ch in chains}
        for ch in chains:
            bb, d = ch
            o = ws[ch][n:2 * n] + jnp.dot(dirs[d][1][bb, cc[ch]], u_b[ch], preferred_element_type=F32)
            for h in range(N_HEADS):
                dirs[d][4][bb, h, cc[ch] * CHUNK:(cc[ch] + 1) * CHUNK, :] = o[h * CHUNK:(h + 1) * CHUNK]
        st = {ch: dirs[ch[1]][3][ch[0], cc[ch], 1] * st[ch]
              + jnp.dot(dirs[ch[1]][2][ch[0], cc[ch]], u_b[ch], preferred_element_type=F32) for ch in chains}
    for ch in chains:
        state_ref[ch[0], ch[1]] = st[ch]


def _gdn_scan_call(local_out, n_lat_chunks, n_ctx_chunks):
    wq = local_out[0]
    b_, nch = wq.shape[0], wq.shape[1]
    nh = N_HEADS
    n = nh * CHUNK
    s = nch * CHUNK
    nc = GDN_SCAN_CHUNKS
    assert n_lat_chunks % nc == 0 and n_ctx_chunks % nc == 0
    nlc, ncc = n_lat_chunks // nc, n_ctx_chunks // nc

    def order_f(i):
        return jnp.where(i < ncc, nlc + i, i - ncc)

    def order_b(i):
        return jnp.where(i < ncc, nlc + ncc - 1 - i, nlc - 1 - (i - ncc))

    nb = max(g for g in (GDN_SCAN_BATCH, 2, 1) if b_ % g == 0)

    def specs(order):
        sp = lambda *shape: pl.BlockSpec((nb, nc) + shape, lambda b, i: (b, order(i)) + (0,) * len(shape))
        return [sp(2 * n, n), sp(n, n), sp(n, n), sp(2, n, HEAD_DIM)]

    oo = lambda order: pl.BlockSpec((nb, nh, nc * CHUNK, HEAD_DIM), lambda b, i: (b, 0, order(i), 0))
    oshape = jax.ShapeDtypeStruct((b_, nh, s, HEAD_DIM), F32)
    return pl.pallas_call(
        _gdn_scan_kernel,
        grid=(b_ // nb, nch // nc),
        in_specs=specs(order_f) + specs(order_b),
        out_specs=[oo(order_f), oo(order_b)],
        out_shape=[oshape, oshape],
        scratch_shapes=[pltpu.VMEM((nb, 2, n, HEAD_DIM), F32)],
        compiler_params=_cparams(("parallel", "arbitrary")),
        name="gdn_scan",
    )(*local_out)


def _gdn_output(of_ref, ob_ref, gate_ref, g_ref):
    gate = gate_ref[0].astype(F32)
    heads = []
    for h in range(N_HEADS):
        o = of_ref[0, h] + ob_ref[0, h]
        o = o * lax.rsqrt(jnp.mean(o * o, axis=1, keepdims=True) + EPS) * g_ref[...]
        heads.append(o * _silu(gate[:, h * HEAD_DIM:(h + 1) * HEAD_DIM]))
    return jnp.concatenate(heads, axis=1).astype(BF16)


def _merge_kernel(of_ref, obw_ref, agate_ref, gn_ref, ob_ref, oc_ref, od_ref, gates_ref, x_ref, mod_ref, wbr_ref,
                  wout_ref, lng_ref, lnb_ref, wr_ref, x1_ref, h2_ref, aff_ref, *, alpha):
    d = D_MODEL
    branch_out = (_gdn_output(of_ref, obw_ref, agate_ref, gn_ref), ob_ref[0], oc_ref[0], od_ref[0])
    m = None
    for i, o in enumerate(branch_out):
        gate = jax.nn.sigmoid(gates_ref[0, :, i * d:(i + 1) * d].astype(F32))
        term = gate * jnp.dot(o, wbr_ref[i], preferred_element_type=F32)
        m = term if m is None else m + term
    y = _bdot(m, wout_ref[...])
    g1 = mod_ref[0, 0, :, 2 * d:3 * d]
    sh2 = mod_ref[0, 0, :, 3 * d:4 * d]
    sc2 = mod_ref[0, 0, :, 4 * d:5 * d]
    x1 = _layer_norm(alpha * x_ref[0] + g1 * y, lng_ref[...], lnb_ref[...])
    x1_ref[0] = x1
    h2 = x1 * (1.0 + sc2) + sh2
    _rows_to_tiles(h2_ref.at[0], h2)
    logits = _bdot(h2, wr_ref[...])
    lane = lax.broadcasted_iota(jnp.int32, logits.shape, 1)
    logits = jnp.where(lane < N_EXPERTS, logits, NEG_BIG)
    e = jnp.exp(logits - jnp.max(logits, axis=1, keepdims=True))
    aff_ref[0] = e / jnp.sum(e, axis=1, keepdims=True)


def _merge_call(gdn, outs, gates, x_all, modv, w_br, w_out, ln_g, ln_b, w_router, alpha, n_lat_tiles, n_tiles):
    b_, s, d = x_all.shape
    ts = TOKEN_TILE
    tok = lambda w: pl.BlockSpec((1, ts, w), lambda b, j: (b, j, 0))
    head = pl.BlockSpec((1, N_HEADS, ts, HEAD_DIM), lambda b, j: (b, 0, j, 0))
    const2 = lambda shape: pl.BlockSpec(shape, lambda b, j: (0,) * len(shape))
    return pl.pallas_call(
        functools.partial(_merge_kernel, alpha=alpha),
        grid=(b_, n_tiles),
        in_specs=[head, head, tok(BRANCH_W), const2((1, HEAD_DIM))] + [tok(BRANCH_W)] * 3 + [
            tok(GATE_W), tok(d),
            pl.BlockSpec((1, 1, 1, 6 * d), lambda b, j: (b, j // n_lat_tiles, 0, 0)),
            const2((N_BRANCH, BRANCH_W, d)), const2((d, d)), const2((1, d)), const2((1, d)),
            const2((d, LANES))],
        out_specs=[tok(d), pl.BlockSpec((1, ts, ROW_SUBLANES, LANES), lambda b, j: (b, j, 0, 0)), tok(LANES)],
        out_shape=[jax.ShapeDtypeStruct((b_, n_tiles * ts, d), F32),
                   jax.ShapeDtypeStruct((b_, n_tiles * ts, ROW_SUBLANES, LANES), F32),
                   jax.ShapeDtypeStruct((b_, n_tiles * ts, LANES), F32)],
        compiler_params=_cparams(("parallel", "arbitrary")),
        name="merge_ln1_router",
    )(*gdn, *outs, gates, x_all, modv, w_br, w_out, ln_g, ln_b, w_router)


ROW_LOOP_UNROLL = 8
ROW_SUBLANES = D_MODEL // LANES


def _rows_to_tiles(ref, x):
    for sl in range(ROW_SUBLANES):
        ref[:, sl, :] = x[:, sl * LANES:(sl + 1) * LANES]


def _tiles_to_rows(ref):
    return jnp.concatenate([ref[:, sl, :] for sl in range(ROW_SUBLANES)], axis=1)


def _moe_kernel(idx_prev, idx_cur, idx_next, gv_prev, gv_cur, h_ref, wg_ref, wu_ref, wd_ref, out_hbm,
                xg_a, xg_b, y_a, y_b, acc_ref, sem, *, cap):
    e = pl.program_id(1)
    n_e = pl.num_programs(1)
    nb = h_ref.shape[0]
    slots = xg_a.shape[0]
    unroll = ROW_LOOP_UNROLL

    def gather(idx_ref, xg_ref):
        for j in range(slots):
            xg_ref[j] = h_ref[j // cap, idx_ref[0, 0, 0, j]]

    def scatter(idx_ref, gv_ref, y_ref):
        for j0 in range(0, slots, unroll):
            bb = j0 // cap
            rows = [idx_ref[0, 0, 0, j0 + k] for k in range(unroll)]
            new = [acc_ref[bb, rows[k]] + gv_ref[0, 0, 0, j0 + k] * y_ref[j0 + k] for k in range(unroll)]
            for k in range(unroll):
                acc_ref[bb, rows[k]] = new[k]

    def expert(xg_ref, y_ref):
        half = slots // 2
        for r0 in (0, half):
            xg = _tiles_to_rows(xg_ref.at[r0:r0 + half]).astype(BF16)
            a = jnp.dot(xg, wg_ref[0], preferred_element_type=F32)
            u = jnp.dot(xg, wu_ref[0], preferred_element_type=F32)
            _rows_to_tiles(y_ref.at[r0:r0 + half], _bdot(_silu(a) * u, wd_ref[0]))

    @pl.when(e == 0)
    def _():
        acc_ref[...] = jnp.zeros_like(acc_ref)
        y_b[...] = jnp.zeros_like(y_b)
        gather(idx_cur, xg_a)

    def step(xg_cur, y_cur, xg_nxt, y_prv):
        scatter(idx_prev, gv_prev, y_prv)
        expert(xg_cur, y_cur)
        gather(idx_next, xg_nxt)

    @pl.when(e % 2 == 0)
    def _():
        step(xg_a, y_a, xg_b, y_b)

    @pl.when(e % 2 == 1)
    def _():
        step(xg_b, y_b, xg_a, y_a)

    @pl.when(e == n_e - 1)
    def _():
        scatter(idx_cur, gv_cur, y_b)
        cp = pltpu.make_async_copy(acc_ref, out_hbm.at[pl.ds(pl.program_id(0) * nb, nb)], sem)
        cp.start()
        cp.wait()


def _moe_call(idx, gval, h2, wg, wu, wd, row_block, n_rows, group):
    g_, n_e, _, slots = idx.shape
    assert n_e % 2 == 0 and slots % (2 * ROW_LOOP_UNROLL) == 0
    b_ = h2.shape[0]
    d, f = wg.shape[1], wg.shape[2]
    cap = slots // group
    tile = (ROW_SUBLANES, LANES)
    smem = lambda shift: pl.BlockSpec((1, 1, 1, slots), lambda g, e: (g, jnp.clip(e + shift, 0, n_e - 1), 0, 0),
                                      memory_space=pltpu.SMEM)
    rows_buf = pltpu.VMEM((slots,) + tile, F32)
    return pl.pallas_call(
        functools.partial(_moe_kernel, cap=cap),
        grid=(g_, n_e),
        in_specs=[smem(-1), smem(0), smem(1), smem(-1), smem(0),
                  pl.BlockSpec((group, n_rows) + tile, lambda g, e: (g, row_block, 0, 0),
                               pipeline_mode=pl.Buffered(1)),
                  pl.BlockSpec((1, d, f), lambda g, e: (e, 0, 0)),
                  pl.BlockSpec((1, d, f), lambda g, e: (e, 0, 0)),
                  pl.BlockSpec((1, f, d), lambda g, e: (e, 0, 0))],
        out_specs=pl.BlockSpec(memory_space=pl.ANY),
        out_shape=jax.ShapeDtypeStruct((b_, n_rows) + tile, F32),
        scratch_shapes=[rows_buf, rows_buf, rows_buf, rows_buf,
                        pltpu.VMEM((group, n_rows) + tile, F32), pltpu.SemaphoreType.DMA(())],
        compiler_params=_cparams(("arbitrary", "arbitrary"), vmem=60 * 1024 * 1024),
        name="moe_experts",
    )(idx, idx, idx, gval, gval, h2, wg, wu, wd)


def _route(aff, n_lat, n_ctx, with_ctx):
    b_ = aff.shape[0]

    def pick(a, n):
        cap = max(1, CAPACITY_FACTOR * n // N_EXPERTS)
        gval, idx = lax.top_k(jnp.swapaxes(a, 1, 2), cap)
        return gval, idx.astype(jnp.int32), cap

    gl, il, cap_l = pick(aff[:, :n_lat, :N_EXPERTS], n_lat)
    lat = (il.reshape(b_, N_EXPERTS, 1, cap_l), gl.reshape(b_, N_EXPERTS, 1, cap_l))
    if not with_ctx:
        return lat, None
    gc, ic, cap_c = pick(aff[:, n_lat:, :N_EXPERTS], n_ctx)
    flat = lambda t: jnp.transpose(t, (1, 0, 2)).reshape(1, N_EXPERTS, 1, b_ * cap_c)
    return lat, (flat(ic), flat(gc))


def _ln2_kernel(x1_ref, fl_ref, fc_ref, mod_ref, g_ref, b_ref, o_ref, *, alpha, n_lat_tiles):
    d = D_MODEL
    j = pl.program_id(1)
    f = jnp.where(j < n_lat_tiles, _tiles_to_rows(fl_ref.at[0]), _tiles_to_rows(fc_ref.at[0]))
    g2 = mod_ref[0, 0, :, 5 * d:6 * d]
    o_ref[0] = _layer_norm(alpha * x1_ref[0] + g2 * f, g_ref[...], b_ref[...])


def _ln2_call(x1, f_lat, f_ctx, modv, ln_g, ln_b, alpha, n_lat_tiles, n_tiles):
    b_, s, d = x1.shape
    ts = TOKEN_TILE
    return pl.pallas_call(
        functools.partial(_ln2_kernel, alpha=alpha, n_lat_tiles=n_lat_tiles),
        grid=(b_, n_tiles),
        in_specs=[pl.BlockSpec((1, ts, d), lambda b, j: (b, j, 0)),
                  pl.BlockSpec((1, ts, ROW_SUBLANES, LANES), lambda b, j: (b, jnp.minimum(j, n_lat_tiles - 1), 0, 0)),
                  pl.BlockSpec((1, ts, ROW_SUBLANES, LANES), lambda b, j: (b, jnp.maximum(j - n_lat_tiles, 0), 0, 0)),
                  pl.BlockSpec((1, 1, 1, 6 * d), lambda b, j: (b, j // n_lat_tiles, 0, 0)),
                  pl.BlockSpec((1, d), lambda b, j: (0, 0)),
                  pl.BlockSpec((1, d), lambda b, j: (0, 0))],
        out_specs=pl.BlockSpec((1, ts, d), lambda b, j: (b, j, 0)),
        out_shape=jax.ShapeDtypeStruct((b_, n_tiles * ts, d), F32),
        compiler_params=_cparams(("parallel", "arbitrary")),
        name="ln2",
    )(x1, f_lat, f_ctx, modv, ln_g, ln_b)


def _rope_tables(n_lat, n_ctx, dim):
    nf = dim // 4
    t = jnp.arange(n_lat)
    row = (t // GRID_W).astype(F32)
    col = (t % GRID_W).astype(F32)
    inv = ROPE_THETA ** (-jnp.arange(nf, dtype=F32) / nf)
    ar = row[:, None] * inv
    ac = col[:, None] * inv
    cos = jnp.concatenate([jnp.cos(ar), jnp.cos(ar), jnp.cos(ac), jnp.cos(ac)], axis=1)
    sin = jnp.concatenate([-jnp.sin(ar), jnp.sin(ar), -jnp.sin(ac), jnp.sin(ac)], axis=1)
    reps = BRANCH_W // dim
    cos = jnp.concatenate([jnp.tile(cos, (1, reps)), jnp.ones((n_ctx, BRANCH_W), F32)], axis=0)
    sin = jnp.concatenate([jnp.tile(sin, (1, reps)), jnp.zeros((n_ctx, BRANCH_W), F32)], axis=0)
    return cos, sin


def _reorder_w_in(w_in):
    o_ab = 3 * BRANCH_W
    o_gate = o_ab + 4 * N_HEADS
    o_b = o_gate + BRANCH_W
    o_gates = o_b + 3 * BRANCH_W + 4 * BRANCH_W
    pad = jnp.zeros(w_in.shape[:2] + (LANES - 4 * N_HEADS,), w_in.dtype)
    return jnp.concatenate([w_in[..., o_gates:], w_in[..., :o_ab], w_in[..., o_ab:o_gate], pad,
                            w_in[..., o_gate:o_b], w_in[..., o_b:o_gates]], axis=-1).astype(BF16)


def _lane_vec(p):
    v = jnp.zeros((LANES,), F32)
    v = v.at[0:N_HEADS].set(p[0]).at[8:8 + N_HEADS].set(p[1])
    return v.reshape(1, LANES)


def kernel(x, c, ctx, c_ctx, w_mod, b_mod, w_in, conv_a, a_log, dt_bias, gdn_norm, diff_lambda, diff_norm,
           qk_norm_c, sink_d, w_br, w_out, ln1_g, ln1_b, w_router, w_gate_e, w_up_e, w_down_e, ln2_g, ln2_b):
    b_, n_lat, d = x.shape
    n_ctx = ctx.shape[1]
    depth = w_mod.shape[0]
    ts = TOKEN_TILE
    assert d == D_MODEL and n_lat % ts == 0 and n_ctx == ts and n_lat % GRID_W == 0
    assert n_lat >= ts + 2 * WINDOW or n_lat == ts
    n_lat_tiles = n_lat // ts
    n_tiles = n_lat_tiles + n_ctx // ts
    alpha = (2.0 * depth) ** 0.25

    rows = -(-(b_ + 1) // 8) * 8
    cc = jnp.zeros((rows, d), F32).at[:b_].set(c).at[b_].set(c_ctx)
    mod_all = _mod_call(cc, w_mod, b_mod)

    w_all = _reorder_w_in(w_in)
    w_br_b = w_br.astype(BF16)
    w_out_b = w_out.astype(BF16)
    w_r = jnp.concatenate([w_router, jnp.zeros((depth, d, LANES - N_EXPERTS), F32)], axis=-1).astype(BF16)
    wg_b, wu_b, wd_b = w_gate_e.astype(BF16), w_up_e.astype(BF16), w_down_e.astype(BF16)
    conv_w = jnp.concatenate([conv_a, jnp.zeros((depth, 8 - CONV_K, conv_a.shape[2]), F32)], axis=1)
    tabs = _rope_tables(n_lat, n_ctx, HEAD_DIM) + _rope_tables(n_lat, n_ctx, DH_B)

    stream = (jnp.concatenate([x, ctx], axis=1),)
    for l in range(depth):
        last = l == depth - 1
        lam_init = 0.8 - 0.6 * math.exp(-0.3 * l)
        modv = jnp.stack([mod_all[l, :b_], jnp.broadcast_to(mod_all[l, b_], (b_, 6 * d))], axis=1)
        modv = modv.reshape(b_, 2, 1, 6 * d)
        gq = jnp.tile(qk_norm_c[l, 0], N_HEADS).reshape(1, BRANCH_W)
        gk = jnp.tile(qk_norm_c[l, 1], N_HEADS // 2).reshape(1, BRANCH_W // 2)

        proj = _inproj_call(stream, modv, w_all[l], tabs, gq, gk, n_lat_tiles, alpha)
        (gates, aqkv, aab, agate, qb, kb, vb, qc, kc, vc, qd, kd, vd) = proj[:13]
        x_all = stream[0] if l == 0 else proj[13]

        qn, kn, vn, gcol, grow = _gdn_prep_call(aqkv, aab, conv_w[l], _lane_vec(a_log[l]), _lane_vec(dt_bias[l]),
                                                n_lat_tiles)
        local_out = _gdn_local_call(qn, kn, vn, gcol, grow)
        o_f, o_b = _gdn_scan_call(local_out, n_lat // CHUNK, n_ctx // CHUNK)
        gdn = (o_f, o_b, agate, gdn_norm[l].reshape(1, HEAD_DIM))

        nq = n_lat_tiles if last else n_tiles
        out_b = _diff_attn_call(qb, kb, vb, diff_lambda[l], diff_norm[l].reshape(1, HEAD_DIM), lam_init,
                                n_lat, n_ctx, nq)
        out_c = _gqa_call(qc, kc, vc, n_lat, n_ctx, nq)
        out_d = _window_call(qd, kd, vd, sink_d[l], n_lat, n_ctx, nq)

        x1, h2, aff = _merge_call(gdn, (out_b, out_c, out_d), gates, x_all, modv, w_br_b[l], w_out_b[l],
                                  ln1_g[l].reshape(1, d), ln1_b[l].reshape(1, d), w_r[l], alpha,
                                  n_lat_tiles, nq)

        route_lat, route_ctx = _route(aff, n_lat, n_ctx, not last)
        f_lat = _moe_call(route_lat[0], route_lat[1], h2, wg_b[l], wu_b[l], wd_b[l], 0, n_lat, 1)
        if last:
            return _ln2_call(x1, f_lat, f_lat, modv, ln2_g[l].reshape(1, d), ln2_b[l].reshape(1, d), alpha,
                             n_lat_tiles, nq)
        f_ctx = _moe_call(route_ctx[0], route_ctx[1], h2, wg_b[l], wu_b[l], wd_b[l], n_lat // n_ctx, n_ctx, b_)
        stream = (x1, f_lat, f_ctx, modv, ln2_g[l].reshape(1, d), ln2_b[l].reshape(1, d))
```

```python
import functools
import math

import jax
import jax.numpy as jnp
from jax import lax
from jax.experimental import pallas as pl
from jax.experimental.pallas import tpu as pltpu

F32 = jnp.float32
BF16 = jnp.bfloat16

D_MODEL = 1024
HEAD_DIM = 64
N_HEADS = 4
BRANCH_W = 256
N_BRANCH = 4
GRID_W = 64
ROPE_THETA = 10000.0
EPS = 1e-6
CONV_K = 5
CHUNK = 64
DH_B = 32
WINDOW = 128
N_EXPERTS = 16
CAPACITY_FACTOR = 2
LOG2E = 1.4426950408889634

TOKEN_TILE = 256
LANES = 128
VMEM_LIMIT = 56 * 1024 * 1024

GATE_W = N_BRANCH * D_MODEL
C_GATES = 0
C_AQKV = C_GATES + GATE_W
C_AAB = C_AQKV + 3 * BRANCH_W
C_AGATE = C_AAB + LANES
C_B = C_AGATE + BRANCH_W
C_C = C_B + 3 * BRANCH_W
C_D = C_C + 2 * BRANCH_W
W_ALL = C_D + 2 * BRANCH_W


def _cparams(sem, vmem=VMEM_LIMIT):
    return pltpu.CompilerParams(dimension_semantics=sem, vmem_limit_bytes=vmem)


def _bdot(a, b):
    return jnp.dot(a.astype(BF16), b.astype(BF16), preferred_element_type=F32)


def _bdot_nt(a, b):
    return lax.dot_general(a.astype(BF16), b.astype(BF16), (((1,), (1,)), ((), ())),
                           preferred_element_type=F32)


def _silu(x):
    return x * jax.nn.sigmoid(x)


def _block_diag_ones(width, group):
    r = lax.broadcasted_iota(jnp.int32, (width, width), 0) // group
    c = lax.broadcasted_iota(jnp.int32, (width, width), 1) // group
    return jnp.where(r == c, 1.0, 0.0).astype(BF16)


def _group_sum(x, ones_bd):
    hi = x.astype(BF16)
    lo = (x - hi.astype(F32)).astype(BF16)
    return (jnp.dot(hi, ones_bd, preferred_element_type=F32)
            + jnp.dot(lo, ones_bd, preferred_element_type=F32))


def _pair_swap(x, q):
    w = x.shape[-1]
    lane = lax.broadcasted_iota(jnp.int32, x.shape, x.ndim - 1)
    fwd = pltpu.roll(x, w - q, x.ndim - 1)
    bwd = pltpu.roll(x, q, x.ndim - 1)
    return jnp.where((lane % (2 * q)) < q, fwd, bwd)


def _layer_norm(x, g, b):
    mu = jnp.mean(x, axis=-1, keepdims=True)
    xc = x - mu
    var = jnp.mean(xc * xc, axis=-1, keepdims=True)
    return xc * lax.rsqrt(var + EPS) * g + b


def _mod_kernel(c_ref, w_ref, b_ref, o_ref):
    o_ref[0] = _bdot(_silu(c_ref[...]), w_ref[0]) + b_ref[0]


def _mod_call(cc, w_mod, b_mod):
    depth, d, n6 = w_mod.shape
    rows = cc.shape[0]
    nb = 1536
    return pl.pallas_call(
        _mod_kernel,
        grid=(depth, n6 // nb),
        in_specs=[pl.BlockSpec((rows, d), lambda l, j: (0, 0)),
                  pl.BlockSpec((1, d, nb), lambda l, j: (l, 0, j)),
                  pl.BlockSpec((1, 1, nb), lambda l, j: (l, 0, j))],
        out_specs=pl.BlockSpec((1, rows, nb), lambda l, j: (l, 0, j)),
        out_shape=jax.ShapeDtypeStruct((depth, rows, n6), F32),
        compiler_params=_cparams(("arbitrary", "arbitrary")),
        name="mod_vectors",
    )(cc, w_mod, b_mod.reshape(depth, 1, n6))


def _rope(x, cos, sin, quarter):
    return x * cos + _pair_swap(x, quarter) * sin


def _store_heads(ref, x, n_heads):
    for h in range(n_heads):
        ref[0, h] = x[:, h * HEAD_DIM:(h + 1) * HEAD_DIM].astype(ref.dtype)


def _store_key_heads(ref, x, n_heads):
    xt = jnp.transpose(x)
    for h in range(n_heads):
        ref[0, h] = xt[h * HEAD_DIM:(h + 1) * HEAD_DIM, :].astype(ref.dtype)


def _store_value_heads(ref, x, n_heads):
    lane = lax.broadcasted_iota(jnp.int32, (x.shape[0], HEAD_DIM), 1)
    ones_col = jnp.where(lane == 0, 1.0, 0.0)
    for h in range(n_heads):
        ref[0, h] = jnp.concatenate([x[:, h * HEAD_DIM:(h + 1) * HEAD_DIM], ones_col], axis=1).astype(ref.dtype)


def _inproj_kernel(*refs, prev_ln, alpha, n_lat_tiles):
    d = D_MODEL
    if prev_ln:
        x1_ref, fl_ref, fc_ref, pmod_ref, plg_ref, plb_ref = refs[:6]
        refs, xo_ref = refs[6:-1], refs[-1]
        f = jnp.where(pl.program_id(1) < n_lat_tiles, _tiles_to_rows(fl_ref.at[0]), _tiles_to_rows(fc_ref.at[0]))
        x = _layer_norm(alpha * x1_ref[0] + pmod_ref[0, 0, :, 5 * d:6 * d] * f, plg_ref[...], plb_ref[...])
        xo_ref[0] = x
    else:
        x, refs = refs[0][0], refs[1:]
    (mod_ref, w_ref, cos_hd_ref, sin_hd_ref, cos_b_ref, sin_b_ref, gq_ref, gk_ref,
     gates_ref, aqkv_ref, aab_ref, agate_ref,
     qb_ref, kb_ref, vb_ref, qc_ref, kc_ref, vc_ref, qd_ref, kd_ref, vd_ref) = refs
    sh1 = mod_ref[0, 0, :, 0:d]
    sc1 = mod_ref[0, 0, :, d:2 * d]
    h = (x * (1.0 + sc1) + sh1).astype(BF16)

    def proj(a, width):
        return jnp.dot(h, w_ref[:, a:a + width], preferred_element_type=F32)

    gates_ref[0] = proj(C_GATES, GATE_W).astype(gates_ref.dtype)
    aqkv_ref[0] = proj(C_AQKV, 3 * BRANCH_W)
    aab_ref[0] = proj(C_AAB, LANES)
    agate_ref[0] = proj(C_AGATE, BRANCH_W).astype(agate_ref.dtype)

    cos_hd, sin_hd = cos_hd_ref[...], sin_hd_ref[...]
    cos_b, sin_b = cos_b_ref[...], sin_b_ref[...]
    kv_w = BRANCH_W // 2

    pb = proj(C_B, 3 * BRANCH_W)
    qb = _rope(pb[:, 0:BRANCH_W], cos_b, sin_b, DH_B // 4) * (DH_B ** -0.5 * LOG2E)
    kb = _rope(pb[:, BRANCH_W:2 * BRANCH_W], cos_b, sin_b, DH_B // 4)
    _store_heads(qb_ref, qb, N_HEADS)
    _store_key_heads(kb_ref, kb, N_HEADS)
    _store_value_heads(vb_ref, pb[:, 2 * BRANCH_W:3 * BRANCH_W], N_HEADS)

    pc = proj(C_C, 2 * BRANCH_W)
    ones_q = _block_diag_ones(BRANCH_W, HEAD_DIM)
    ones_k = _block_diag_ones(kv_w, HEAD_DIM)
    qc = pc[:, 0:BRANCH_W]
    kc = pc[:, BRANCH_W:BRANCH_W + kv_w]
    qc = qc * lax.rsqrt(_group_sum(qc * qc, ones_q) * (1.0 / HEAD_DIM) + EPS) * gq_ref[...]
    kc = kc * lax.rsqrt(_group_sum(kc * kc, ones_k) * (1.0 / HEAD_DIM) + EPS) * gk_ref[...]
    qc = _rope(qc, cos_hd, sin_hd, HEAD_DIM // 4) * (HEAD_DIM ** -0.5 * LOG2E)
    kc = _rope(kc, cos_hd[:, 0:kv_w], sin_hd[:, 0:kv_w], HEAD_DIM // 4)
    _store_heads(qc_ref, qc, N_HEADS)
    _store_key_heads(kc_ref, kc, N_HEADS // 2)
    _store_value_heads(vc_ref, pc[:, BRANCH_W + kv_w:2 * BRANCH_W], N_HEADS // 2)

    pd = proj(C_D, 2 * BRANCH_W)
    qd = _rope(pd[:, 0:BRANCH_W], cos_hd, sin_hd, HEAD_DIM // 4) * (HEAD_DIM ** -0.5 * LOG2E)
    kd = _rope(pd[:, BRANCH_W:BRANCH_W + kv_w], cos_hd[:, 0:kv_w], sin_hd[:, 0:kv_w], HEAD_DIM // 4)
    _store_heads(qd_ref, qd, N_HEADS)
    _store_key_heads(kd_ref, kd, N_HEADS // 2)
    _store_value_heads(vd_ref, pd[:, BRANCH_W + kv_w:2 * BRANCH_W], N_HEADS // 2)


def _inproj_call(stream, modv, w_all, tabs, gq, gk, n_lat_tiles, alpha):
    prev_ln = len(stream) > 1
    b_, s, d = stream[0].shape
    ts = TOKEN_TILE
    nt = s // ts
    vec = pl.BlockSpec((1, d), lambda b, j: (0, 0))
    tile_rows = lambda index: pl.BlockSpec((1, ts, ROW_SUBLANES, LANES), index)
    mod_spec = pl.BlockSpec((1, 1, 1, 6 * d), lambda b, j: (b, j // n_lat_tiles, 0, 0))
    tok = lambda w: pl.BlockSpec((1, ts, w), lambda b, j: (b, j, 0))
    tab = pl.BlockSpec((ts, BRANCH_W), lambda b, j: (j, 0))
    head = lambda nh, w=HEAD_DIM: pl.BlockSpec((1, nh, ts, w), lambda b, j: (b, 0, j, 0))
    hshape = lambda nh, w=HEAD_DIM: jax.ShapeDtypeStruct((b_, nh, s, w), BF16)
    khead = lambda nh: pl.BlockSpec((1, nh, HEAD_DIM, ts), lambda b, j: (b, 0, 0, j))
    kshape = lambda nh: jax.ShapeDtypeStruct((b_, nh, HEAD_DIM, s), BF16)
    kvh = N_HEADS // 2
    stream_specs = [tok(d)]
    if prev_ln:
        stream_specs += [tile_rows(lambda b, j: (b, jnp.minimum(j, n_lat_tiles - 1), 0, 0)),
                         tile_rows(lambda b, j: (b, jnp.maximum(j - n_lat_tiles, 0), 0, 0)),
                         mod_spec, vec, vec]
    return pl.pallas_call(
        functools.partial(_inproj_kernel, prev_ln=prev_ln, alpha=alpha, n_lat_tiles=n_lat_tiles),
        grid=(b_, nt),
        in_specs=stream_specs + [
                  mod_spec,
                  pl.BlockSpec((d, W_ALL), lambda b, j: (0, 0), pipeline_mode=pl.Buffered(1)),
                  tab, tab, tab, tab,
                  pl.BlockSpec((1, BRANCH_W), lambda b, j: (0, 0)),
                  pl.BlockSpec((1, BRANCH_W // 2), lambda b, j: (0, 0))],
        out_specs=[tok(GATE_W), tok(3 * BRANCH_W), tok(LANES), tok(BRANCH_W),
                   head(N_HEADS), khead(N_HEADS), head(N_HEADS, LANES),
                   head(N_HEADS), khead(kvh), head(kvh, LANES),
                   head(N_HEADS), khead(kvh), head(kvh, LANES)] + ([tok(d)] if prev_ln else []),
        out_shape=[jax.ShapeDtypeStruct((b_, s, GATE_W), BF16),
                   jax.ShapeDtypeStruct((b_, s, 3 * BRANCH_W), F32),
                   jax.ShapeDtypeStruct((b_, s, LANES), F32),
                   jax.ShapeDtypeStruct((b_, s, BRANCH_W), BF16),
                   hshape(N_HEADS), kshape(N_HEADS), hshape(N_HEADS, LANES),
                   hshape(N_HEADS), kshape(kvh), hshape(kvh, LANES),
                   hshape(N_HEADS), kshape(kvh), hshape(kvh, LANES)]
                  + ([jax.ShapeDtypeStruct((b_, s, d), F32)] if prev_ln else []),
        compiler_params=_cparams(("parallel", "arbitrary")),
        name="inproj",
    )(*stream, modv, w_all, *tabs, gq, gk)


NEG_BIG = -1e30


def _softmax_pv(streams, chunks, mask_fn=None):
    m = [st[3] for st in streams]
    acc = [None] * len(streams)
    for i, (q2, k_ref, v_ref, _) in enumerate(streams):
        for ks, n, tag in chunks:
            s = jnp.dot(q2, k_ref[:, pl.ds(ks, n)], preferred_element_type=F32)
            if mask_fn is not None:
                s = mask_fn(s, ks, n, tag)
            cm = jnp.max(s, axis=1, keepdims=True)
            m_new = cm if m[i] is None else jnp.maximum(m[i], cm)
            pv = jnp.dot(jnp.exp2(s - m_new).astype(BF16), v_ref[pl.ds(ks, n), :], preferred_element_type=F32)
            acc[i] = pv if acc[i] is None else acc[i] * jnp.exp2(m[i] - m_new) + pv
            m[i] = m_new
    out = []
    for i, st in enumerate(streams):
        l = acc[i][:, HEAD_DIM:HEAD_DIM + 1]
        if st[3] is not None:
            l = l + jnp.exp2(st[3] - m[i])
        out.append((acc[i][:, 0:HEAD_DIM], l))
    return out


def _key_chunks(start, total, size):
    out, a = [], 0
    while a < total:
        n = min(size, total - a)
        out.append((start + a, n, a))
        a += n
    return out


KEY_CHUNK = 2048


def _diff_attn_kernel(lam_ref, q_ref, k_ref, v_ref, g_ref, o_ref, *, n_lat, n_ctx, lam_init):
    tq = q_ref.shape[2]
    i = pl.program_id(2)
    lv = lam_ref[...]
    lam = (jnp.exp(jnp.sum(lv[0:1] * lv[1:2], axis=1, keepdims=True))
           - jnp.exp(jnp.sum(lv[2:3] * lv[3:4], axis=1, keepdims=True)) + lam_init)
    lane = lax.broadcasted_iota(jnp.int32, (tq, HEAD_DIM), 1)

    def run(chunks):
        streams = []
        for hh in range(N_HEADS):
            q = q_ref[0, hh]
            zero = jnp.zeros_like(q)
            q2 = jnp.concatenate([jnp.where(lane < DH_B, q, zero), jnp.where(lane < DH_B, zero, q)], axis=0)
            streams.append((q2, k_ref.at[0, hh], v_ref.at[0, hh], None))
        for hh, (acc, l) in enumerate(_softmax_pv(streams, chunks)):
            o = acc / l
            o = o[0:tq] - lam * o[tq:2 * tq]
            o = o * lax.rsqrt(jnp.mean(o * o, axis=1, keepdims=True) + EPS) * g_ref[...] * (1.0 - lam_init)
            o_ref[0, :, hh * HEAD_DIM:(hh + 1) * HEAD_DIM] = o.astype(o_ref.dtype)

    @pl.when(i * tq < n_lat)
    def _():
        run(_key_chunks(0, n_lat + n_ctx, KEY_CHUNK))

    @pl.when(i * tq >= n_lat)
    def _():
        run(_key_chunks(n_lat, n_ctx, KEY_CHUNK))


def _gqa_kernel(q_ref, k_ref, v_ref, o_ref, *, n_lat, n_ctx):
    tq = q_ref.shape[2]
    i = pl.program_id(2)

    def run(chunks):
        streams = [(jnp.concatenate([q_ref[0, 2 * g], q_ref[0, 2 * g + 1]], axis=0), k_ref.at[0, g], v_ref.at[0, g],
                    None) for g in range(N_HEADS // 2)]
        for g, (acc, l) in enumerate(_softmax_pv(streams, chunks)):
            o = acc / l
            for j in range(2):
                c0 = (2 * g + j) * HEAD_DIM
                o_ref[0, :, c0:c0 + HEAD_DIM] = o[j * tq:(j + 1) * tq].astype(o_ref.dtype)

    @pl.when(i * tq < n_lat)
    def _():
        run(_key_chunks(0, n_lat + n_ctx, KEY_CHUNK))

    @pl.when(i * tq >= n_lat)
    def _():
        run(_key_chunks(n_lat, n_ctx, KEY_CHUNK))


def _window_kernel(sink_ref, q_ref, k_ref, v_ref, o_ref, *, n_lat, n_ctx, win_keys):
    tq = q_ref.shape[2]
    i = pl.program_id(2)
    row = lax.broadcasted_iota(jnp.int32, (2 * tq, 1), 0)

    def run(chunks, mask):
        streams = []
        for g in range(N_HEADS // 2):
            sink = jnp.where(row < tq, sink_ref[2 * g], sink_ref[2 * g + 1]) * LOG2E
            q2 = jnp.concatenate([q_ref[0, 2 * g], q_ref[0, 2 * g + 1]], axis=0)
            streams.append((q2, k_ref.at[0, g], v_ref.at[0, g], sink))
        for g, (acc, l) in enumerate(_softmax_pv(streams, chunks, mask_fn=mask)):
            o = acc / l
            for j in range(2):
                c0 = (2 * g + j) * HEAD_DIM
                o_ref[0, :, c0:c0 + HEAD_DIM] = o[j * tq:(j + 1) * tq].astype(o_ref.dtype)

    @pl.when(i * tq < n_lat)
    def _():
        start = jnp.clip(i * tq - WINDOW, 0, n_lat - win_keys)
        start = pl.multiple_of(start, WINDOW)

        def mask(s, ks, n, tag):
            if tag < n_ctx:
                return s
            t = i * tq + lax.broadcasted_iota(jnp.int32, s.shape, 0) % tq
            sp = start + lax.broadcasted_iota(jnp.int32, s.shape, 1)
            return jnp.where(jnp.abs(t - sp) <= WINDOW, s, NEG_BIG)

        run([(n_lat, n_ctx, 0), (start, win_keys, n_ctx)], mask)

    @pl.when(i * tq >= n_lat)
    def _():
        run([(n_lat, n_ctx, 0)], None)


def _attn_specs(s, tq, q_heads, kv_heads):
    q_spec = pl.BlockSpec((1, q_heads, tq, HEAD_DIM), lambda b, g, i: (b, 0, i, 0))
    k_spec = pl.BlockSpec((1, kv_heads, HEAD_DIM, s), lambda b, g, i: (b, 0, 0, 0))
    v_spec = pl.BlockSpec((1, kv_heads, s, LANES), lambda b, g, i: (b, 0, 0, 0))
    o_spec = pl.BlockSpec((1, tq, BRANCH_W), lambda b, g, i: (b, i, 0))
    return q_spec, k_spec, v_spec, o_spec


def _diff_attn_call(q, k, v, lam_vecs, norm_g, lam_init, n_lat, n_ctx, n_q_tiles):
    b_, _, s, _ = q.shape
    tq = TOKEN_TILE
    q_spec, k_spec, v_spec, o_spec = _attn_specs(s, tq, N_HEADS, N_HEADS)
    return pl.pallas_call(
        functools.partial(_diff_attn_kernel, n_lat=n_lat, n_ctx=n_ctx, lam_init=lam_init),
        grid=(b_, 1, n_q_tiles),
        in_specs=[pl.BlockSpec((4, DH_B), lambda b, g, i: (0, 0)), q_spec, k_spec, v_spec,
                  pl.BlockSpec((1, HEAD_DIM), lambda b, g, i: (0, 0))],
        out_specs=o_spec,
        out_shape=jax.ShapeDtypeStruct((b_, n_q_tiles * tq, BRANCH_W), BF16),
        compiler_params=_cparams(("parallel", "arbitrary", "arbitrary")),
        name="diff_attn",
    )(lam_vecs, q, k, v, norm_g)


def _gqa_call(q, k, v, n_lat, n_ctx, n_q_tiles):
    b_, _, s, _ = q.shape
    tq = TOKEN_TILE
    q_spec, k_spec, v_spec, o_spec = _attn_specs(s, tq, N_HEADS, N_HEADS // 2)
    return pl.pallas_call(
        functools.partial(_gqa_kernel, n_lat=n_lat, n_ctx=n_ctx),
        grid=(b_, 1, n_q_tiles),
        in_specs=[q_spec, k_spec, v_spec],
        out_specs=o_spec,
        out_shape=jax.ShapeDtypeStruct((b_, n_q_tiles * tq, BRANCH_W), BF16),
        compiler_params=_cparams(("parallel", "arbitrary", "arbitrary")),
        name="gqa_attn",
    )(q, k, v)


def _window_call(q, k, v, sink, n_lat, n_ctx, n_q_tiles):
    b_, _, s, _ = q.shape
    tq = TOKEN_TILE
    win_keys = min(tq + 2 * WINDOW, n_lat)
    q_spec, k_spec, v_spec, o_spec = _attn_specs(s, tq, N_HEADS, N_HEADS // 2)
    return pl.pallas_call(
        functools.partial(_window_kernel, n_lat=n_lat, n_ctx=n_ctx, win_keys=win_keys),
        grid=(b_, 1, n_q_tiles),
        in_specs=[pl.BlockSpec(memory_space=pltpu.SMEM), q_spec, k_spec, v_spec],
        out_specs=o_spec,
        out_shape=jax.ShapeDtypeStruct((b_, n_q_tiles * tq, BRANCH_W), BF16),
        compiler_params=_cparams(("parallel", "arbitrary", "arbitrary")),
        name="window_attn",
    )(sink, q, k, v)


def _row_iota(shape):
    return lax.broadcasted_iota(jnp.int32, shape, 0)


def _chunk_scan(x, pos, forward):
    n = x.shape[0]
    d = 1
    while d < CHUNK:
        if forward:
            x = x + jnp.where(pos >= d, pltpu.roll(x, d, 0), 0.0)
        else:
            x = x + jnp.where(pos < CHUNK - d, pltpu.roll(x, n - d, 0), 0.0)
        d *= 2
    return x


HALO_ROWS = 8


def _gdn_prep_kernel(prev_ref, cur_ref, next_ref, ab_ref, convw_ref, alog_ref, dtb_ref,
                     q_ref, k_ref, v_ref, gcol_ref, grow_ref, *, n_lat_tiles):
    ts = cur_ref.shape[1]
    j = pl.program_id(1)
    has_prev = jnp.logical_and(j != 0, j != n_lat_tiles)
    has_next = jnp.logical_and(j != n_lat_tiles - 1, j < n_lat_tiles)
    cur = cur_ref[0]
    prev8 = jnp.where(has_prev, prev_ref[0], 0.0)
    next8 = jnp.where(has_next, next_ref[0], 0.0)
    reps = ts // HALO_ROWS
    rows = _row_iota(cur.shape)
    half = CONV_K // 2
    y = cur * convw_ref[half:half + 1, :]
    for d in range(1, half + 1):
        back = jnp.where(rows < d, jnp.concatenate([pltpu.roll(prev8, d, 0)] * reps, axis=0), pltpu.roll(cur, d, 0))
        fwd = jnp.where(rows >= ts - d, jnp.concatenate([pltpu.roll(next8, HALO_ROWS - d, 0)] * reps, axis=0),
                        pltpu.roll(cur, ts - d, 0))
        y = y + back * convw_ref[half - d:half - d + 1, :] + fwd * convw_ref[half + d:half + d + 1, :]
    y = _silu(y)
    ones_bd = _block_diag_ones(BRANCH_W, HEAD_DIM)
    q = y[:, 0:BRANCH_W]
    k = y[:, BRANCH_W:2 * BRANCH_W]
    q = q * lax.rsqrt(_group_sum(q * q, ones_bd) + EPS) * (HEAD_DIM ** -0.5)
    k = k * lax.rsqrt(_group_sum(k * k, ones_bd) + EPS)
    _store_heads(q_ref, q, N_HEADS)
    _store_heads(k_ref, k, N_HEADS)
    _store_heads(v_ref, y[:, 2 * BRANCH_W:3 * BRANCH_W], N_HEADS)

    ab = ab_ref[0]
    lane = lax.broadcasted_iota(jnp.int32, ab.shape, 1)
    is_a = (lane % 8) < N_HEADS
    z = ab + dtb_ref[...]
    softplus = jnp.maximum(z, 0.0) + jnp.log(1.0 + jnp.exp(-jnp.abs(z)))
    lg = jnp.where(is_a, -jnp.exp(alog_ref[...]) * softplus, 0.0)
    pos = _row_iota(ab.shape) % CHUNK
    cum = jnp.where(lane < 8, _chunk_scan(lg, pos, True), _chunk_scan(lg, pos, False))
    gates = jnp.where(is_a, cum, jax.nn.sigmoid(ab))
    gcol_ref[0] = gates
    gt = jnp.transpose(gates)
    for c in range(ts // CHUNK):
        grow_ref[0, c] = gt[0:16, c * CHUNK:(c + 1) * CHUNK]


def _gdn_prep_call(aqkv, aab, conv_w, alog_vec, dtb_vec, n_lat_tiles):
    b_, s, _ = aqkv.shape
    ts = TOKEN_TILE
    nt = s // ts
    w3 = 3 * BRANCH_W
    head = pl.BlockSpec((1, N_HEADS, ts, HEAD_DIM), lambda b, j: (b, 0, j, 0))
    hshape = jax.ShapeDtypeStruct((b_, N_HEADS, s, HEAD_DIM), F32)
    cpt = ts // CHUNK
    hpt = ts // HALO_ROWS
    return pl.pallas_call(
        functools.partial(_gdn_prep_kernel, n_lat_tiles=n_lat_tiles),
        grid=(b_, nt),
        in_specs=[pl.BlockSpec((1, HALO_ROWS, w3), lambda b, j: (b, jnp.maximum(j * hpt - 1, 0), 0)),
                  pl.BlockSpec((1, ts, w3), lambda b, j: (b, j, 0)),
                  pl.BlockSpec((1, HALO_ROWS, w3), lambda b, j: (b, jnp.minimum((j + 1) * hpt, nt * hpt - 1), 0)),
                  pl.BlockSpec((1, ts, LANES), lambda b, j: (b, j, 0)),
                  pl.BlockSpec((8, w3), lambda b, j: (0, 0)),
                  pl.BlockSpec((1, LANES), lambda b, j: (0, 0)),
                  pl.BlockSpec((1, LANES), lambda b, j: (0, 0))],
        out_specs=[head, head, head,
                   pl.BlockSpec((1, ts, LANES), lambda b, j: (b, j, 0)),
                   pl.BlockSpec((1, cpt, 16, CHUNK), lambda b, j: (b, j, 0, 0))],
        out_shape=[hshape, hshape, hshape,
                   jax.ShapeDtypeStruct((b_, s, LANES), F32),
                   jax.ShapeDtypeStruct((b_, s // CHUNK, 16, CHUNK), F32)],
        compiler_params=_cparams(("parallel", "arbitrary")),
        name="gdn_prep",
    )(aqkv, aqkv, aqkv, aab, conv_w, alog_vec, dtb_vec)


def _gdn_local_kernel(q_ref, k_ref, v_ref, gcol_ref, grow_ref,
                      wq_f, p_f, kdt_f, ug_f, wq_b, p_b, kdt_b, ug_b):
    c = CHUNK
    n = N_HEADS * c
    r = lax.broadcasted_iota(jnp.int32, (n, n), 0)
    s_ = lax.broadcasted_iota(jnp.int32, (n, n), 1)
    same_head = (r // c) == (s_ // c)
    eye = jnp.where(r == s_, 1.0, 0.0)
    eye_c = eye[0:c, 0:c]
    untile = jnp.where(lax.broadcasted_iota(jnp.int32, (n, c), 0) % c == lax.broadcasted_iota(jnp.int32, (n, c), 1),
                       1.0, 0.0)
    outs = ((wq_f, p_f, kdt_f, ug_f), (wq_b, p_b, kdt_b, ug_b))
    stack_cols = lambda a, lane0: jnp.concatenate([a[:, lane0 + h:lane0 + h + 1] for h in range(N_HEADS)], axis=0)
    chains = []
    for cc in range(q_ref.shape[2] // c):
        rows = slice(cc * c, (cc + 1) * c)
        q = q_ref[0, :, rows, :].reshape(n, HEAD_DIM)
        k = k_ref[0, :, rows, :].reshape(n, HEAD_DIM)
        v = v_ref[0, :, rows, :].reshape(n, HEAD_DIM)
        gcol = gcol_ref[0, rows, :]
        grow = grow_ref[0, cc]
        kk = _bdot_nt(k, k)
        qk = _bdot_nt(q, k)
        kt = _bdot_nt(eye_c, k)
        for direction in range(2):
            wq_ref, p_ref, kdt_ref, ug_ref = outs[direction]
            base = 8 * direction
            g_c = stack_cols(gcol, base)
            beta_c = stack_cols(gcol, base + N_HEADS)
            g_r = jnp.concatenate([grow[base + h:base + h + 1, :] for h in range(N_HEADS)], axis=1)
            end = c - 1 if direction == 0 else 0
            g_end_r = jnp.concatenate([jnp.broadcast_to(grow[base + h:base + h + 1, end:end + 1], (1, c))
                                       for h in range(N_HEADS)], axis=1)
            g_end_c = jnp.concatenate([jnp.broadcast_to(gcol[end:end + 1, base + h:base + h + 1], (c, 1))
                                       for h in range(N_HEADS)], axis=0)
            if direction == 0:
                incl, strict = same_head & (r >= s_), same_head & (r > s_)
            else:
                incl, strict = same_head & (r <= s_), same_head & (r < s_)
            decay = jnp.exp(jnp.where(incl, g_c - g_r, -jnp.inf))
            a_mat = jnp.where(strict, beta_c * kk * decay, 0.0)
            eg = jnp.exp(g_c)
            rhs = jnp.concatenate([beta_c * eg * k, beta_c * v], axis=1)
            p_ref[0, cc] = _bdot(qk * decay, untile).astype(BF16)
            kdt = kt * jnp.exp(g_end_r - g_r)
            kdt_ref[0, cc] = kdt.astype(BF16)
            ug_ref[0, cc, 1] = jnp.broadcast_to(jnp.exp(g_end_c), (n, HEAD_DIM))
            chains.append(dict(pw=-a_mat, t_inv=eye - a_mat, rhs=rhs, qg=q * eg, cc=cc, wq_ref=wq_ref, ug_ref=ug_ref))

    for _ in range(5):
        for ch in chains:
            ch["pw"] = _bdot(ch["pw"], ch["pw"])
        for ch in chains:
            ch["t_inv"] = ch["t_inv"] + _bdot(ch["t_inv"], ch["pw"])
    for ch in chains:
        cc, wq_ref, ug_ref = ch["cc"], ch["wq_ref"], ch["ug_ref"]
        sol = _bdot(ch["t_inv"], ch["rhs"])
        wq = jnp.concatenate([sol[:, 0:c], ch["qg"]], axis=0)
        wq_ref[0, cc] = wq.astype(BF16)
        ug_ref[0, cc, 0] = sol[:, c:2 * c]


GDN_CHUNKS_PER_STEP = 4
GDN_SCAN_CHUNKS = 2
GDN_SCAN_BATCH = 4


def _gdn_local_call(qn, kn, vn, gcol, grow):
    b_, nh, s, _ = qn.shape
    nch = s // CHUNK
    cps = GDN_CHUNKS_PER_STEP
    n = nh * CHUNK
    head = pl.BlockSpec((1, nh, cps * CHUNK, HEAD_DIM), lambda b, c: (b, 0, c, 0))
    spec = lambda *shape: pl.BlockSpec((1, cps) + shape, lambda b, c: (b, c) + (0,) * len(shape))
    out_specs = [spec(2 * n, CHUNK), spec(n, CHUNK), spec(CHUNK, n), spec(2, n, HEAD_DIM)]
    out_shape = [jax.ShapeDtypeStruct((b_, nch, 2 * n, CHUNK), BF16), jax.ShapeDtypeStruct((b_, nch, n, CHUNK), BF16),
                 jax.ShapeDtypeStruct((b_, nch, CHUNK, n), BF16), jax.ShapeDtypeStruct((b_, nch, 2, n, HEAD_DIM), F32)]
    return pl.pallas_call(
        _gdn_local_kernel,
        grid=(b_, nch // cps),
        in_specs=[head, head, head,
                  pl.BlockSpec((1, cps * CHUNK, LANES), lambda b, c: (b, c, 0)),
                  pl.BlockSpec((1, cps, 16, CHUNK), lambda b, c: (b, c, 0, 0))],
        out_specs=out_specs * 2,
        out_shape=out_shape * 2,
        compiler_params=_cparams(("parallel", "arbitrary")),
        name="gdn_local",
    )(qn, kn, vn, gcol, grow)


def _gdn_scan_kernel(wq_f, p_f, kdt_f, ug_f, wq_b, p_b, kdt_b, ug_b, of_ref, ob_ref, state_ref):
    n = N_HEADS * CHUNK

    @pl.when(pl.program_id(1) == 0)
    def _():
        state_ref[...] = jnp.zeros_like(state_ref)

    nb, nc = wq_f.shape[0], wq_f.shape[1]
    dirs = ((wq_f, p_f, kdt_f, ug_f, of_ref), (wq_b, p_b, kdt_b, ug_b, ob_ref))
    chains = [(bb, d) for bb in range(nb) for d in range(2)]
    st = {ch: state_ref[ch[0], ch[1]] for ch in chains}
    r2 = lax.broadcasted_iota(jnp.int32, (2 * n, n), 0)
    c2 = lax.broadcasted_iota(jnp.int32, (2 * n, n), 1)
    same_head2 = ((r2 % n) // CHUNK) == (c2 // CHUNK)
    same_head = same_head2[0:n]
    tile = jnp.where(lax.broadcasted_iota(jnp.int32, (CHUNK, n), 0)
                     == lax.broadcasted_iota(jnp.int32, (CHUNK, n), 1) % CHUNK, 1.0, 0.0).astype(BF16)
    lane_tile = lambda x: jnp.dot(x, tile, preferred_element_type=F32)
    for t in range(nc):
        cc = {ch: (t if ch[1] == 0 else nc - 1 - t) for ch in chains}
        wq = {ch: jnp.where(same_head2, lane_tile(dirs[ch[1]][0][ch[0], cc[ch]]), 0.0).astype(BF16) for ch in chains}
        pm = {ch: jnp.where(same_head, lane_tile(dirs[ch[1]][1][ch[0], cc[ch]]), 0.0).astype(BF16) for ch in chains}
        kdt = {ch: jnp.where(same_head, jnp.concatenate([dirs[ch[1]][2][ch[0], cc[ch]]] * N_HEADS, axis=0),
                             jnp.zeros((), BF16)) for ch in chains}
        ws = {ch: jnp.dot(wq[ch], st[ch].astype(BF16), preferred_element_type=F32) for ch in chains}
        u_b = {ch: (dirs[ch[1]][3][ch[0], cc[ch], 0] - ws[ch][0:n]).astype(BF16) for ch in chains}
        for ch in chains:
            bb, d = ch
            o = ws[ch][n:2 * n] + jnp.dot(pm[ch], u_b[ch], preferred_element_type=F32)
            for h in range(N_HEADS):
                dirs[d][4][bb, h, cc[ch] * CHUNK:(cc[ch] + 1) * CHUNK, :] = o[h * CHUNK:(h + 1) * CHUNK]
        st = {ch: dirs[ch[1]][3][ch[0], cc[ch], 1] * st[ch]
              + jnp.dot(kdt[ch], u_b[ch], preferred_element_type=F32) for ch in chains}
    for ch in chains:
        state_ref[ch[0], ch[1]] = st[ch]


def _gdn_scan_call(local_out, n_lat_chunks, n_ctx_chunks):
    wq = local_out[0]
    b_, nch = wq.shape[0], wq.shape[1]
    nh = N_HEADS
    n = nh * CHUNK
    s = nch * CHUNK
    nc = GDN_SCAN_CHUNKS
    assert n_lat_chunks % nc == 0 and n_ctx_chunks % nc == 0
    nlc, ncc = n_lat_chunks // nc, n_ctx_chunks // nc

    def order_f(i):
        return jnp.where(i < ncc, nlc + i, i - ncc)

    def order_b(i):
        return jnp.where(i < ncc, nlc + ncc - 1 - i, nlc - 1 - (i - ncc))

    nb = max(g for g in (GDN_SCAN_BATCH, 2, 1) if b_ % g == 0)

    def specs(order):
        sp = lambda *shape: pl.BlockSpec((nb, nc) + shape, lambda b, i: (b, order(i)) + (0,) * len(shape))
        return [sp(2 * n, CHUNK), sp(n, CHUNK), sp(CHUNK, n), sp(2, n, HEAD_DIM)]

    oo = lambda order: pl.BlockSpec((nb, nh, nc * CHUNK, HEAD_DIM), lambda b, i: (b, 0, order(i), 0))
    oshape = jax.ShapeDtypeStruct((b_, nh, s, HEAD_DIM), F32)
    return pl.pallas_call(
        _gdn_scan_kernel,
        grid=(b_ // nb, nch // nc),
        in_specs=specs(order_f) + specs(order_b),
        out_specs=[oo(order_f), oo(order_b)],
        out_shape=[oshape, oshape],
        scratch_shapes=[pltpu.VMEM((nb, 2, n, HEAD_DIM), F32)],
        compiler_params=_cparams(("parallel", "arbitrary")),
        name="gdn_scan",
    )(*local_out)


def _gdn_output(of_ref, ob_ref, gate_ref, g_ref):
    gate = gate_ref[0].astype(F32)
    heads = []
    for h in range(N_HEADS):
        o = of_ref[0, h] + ob_ref[0, h]
        o = o * lax.rsqrt(jnp.mean(o * o, axis=1, keepdims=True) + EPS) * g_ref[...]
        heads.append(o * _silu(gate[:, h * HEAD_DIM:(h + 1) * HEAD_DIM]))
    return jnp.concatenate(heads, axis=1).astype(BF16)


def _merge_kernel(of_ref, obw_ref, agate_ref, gn_ref, ob_ref, oc_ref, od_ref, gates_ref, x_ref, mod_ref, wbr_ref,
                  wout_ref, lng_ref, lnb_ref, wr_ref, x1_ref, h2_ref, aff_ref, *, alpha):
    d = D_MODEL
    branch_out = (_gdn_output(of_ref, obw_ref, agate_ref, gn_ref), ob_ref[0], oc_ref[0], od_ref[0])
    m = None
    for i, o in enumerate(branch_out):
        gate = jax.nn.sigmoid(gates_ref[0, :, i * d:(i + 1) * d].astype(F32))
        term = gate * jnp.dot(o, wbr_ref[i], preferred_element_type=F32)
        m = term if m is None else m + term
    y = _bdot(m, wout_ref[...])
    g1 = mod_ref[0, 0, :, 2 * d:3 * d]
    sh2 = mod_ref[0, 0, :, 3 * d:4 * d]
    sc2 = mod_ref[0, 0, :, 4 * d:5 * d]
    x1 = _layer_norm(alpha * x_ref[0] + g1 * y, lng_ref[...], lnb_ref[...])
    x1_ref[0] = x1
    h2 = x1 * (1.0 + sc2) + sh2
    _rows_to_tiles(h2_ref.at[0], h2)
    logits = _bdot(h2, wr_ref[...])
    lane = lax.broadcasted_iota(jnp.int32, logits.shape, 1)
    logits = jnp.where(lane < N_EXPERTS, logits, NEG_BIG)
    e = jnp.exp(logits - jnp.max(logits, axis=1, keepdims=True))
    aff_ref[0] = e / jnp.sum(e, axis=1, keepdims=True)


def _merge_call(gdn, outs, gates, x_all, modv, w_br, w_out, ln_g, ln_b, w_router, alpha, n_lat_tiles, n_tiles):
    b_, s, d = x_all.shape
    ts = TOKEN_TILE
    tok = lambda w: pl.BlockSpec((1, ts, w), lambda b, j: (b, j, 0))
    head = pl.BlockSpec((1, N_HEADS, ts, HEAD_DIM), lambda b, j: (b, 0, j, 0))
    const2 = lambda shape: pl.BlockSpec(shape, lambda b, j: (0,) * len(shape))
    return pl.pallas_call(
        functools.partial(_merge_kernel, alpha=alpha),
        grid=(b_, n_tiles),
        in_specs=[head, head, tok(BRANCH_W), const2((1, HEAD_DIM))] + [tok(BRANCH_W)] * 3 + [
            tok(GATE_W), tok(d),
            pl.BlockSpec((1, 1, 1, 6 * d), lambda b, j: (b, j // n_lat_tiles, 0, 0)),
            const2((N_BRANCH, BRANCH_W, d)), const2((d, d)), const2((1, d)), const2((1, d)),
            const2((d, LANES))],
        out_specs=[tok(d), pl.BlockSpec((1, ts, ROW_SUBLANES, LANES), lambda b, j: (b, j, 0, 0)), tok(LANES)],
        out_shape=[jax.ShapeDtypeStruct((b_, n_tiles * ts, d), F32),
                   jax.ShapeDtypeStruct((b_, n_tiles * ts, ROW_SUBLANES, LANES), F32),
                   jax.ShapeDtypeStruct((b_, n_tiles * ts, LANES), F32)],
        compiler_params=_cparams(("parallel", "arbitrary")),
        name="merge_ln1_router",
    )(*gdn, *outs, gates, x_all, modv, w_br, w_out, ln_g, ln_b, w_router)


ROW_LOOP_UNROLL = 8
ROW_SUBLANES = D_MODEL // LANES


def _rows_to_tiles(ref, x):
    for sl in range(ROW_SUBLANES):
        ref[:, sl, :] = x[:, sl * LANES:(sl + 1) * LANES]


def _tiles_to_rows(ref):
    return jnp.concatenate([ref[:, sl, :] for sl in range(ROW_SUBLANES)], axis=1)


def _moe_kernel(idx_prev, idx_cur, idx_next, gv_prev, gv_cur, h_ref, wg_ref, wu_ref, wd_ref, out_hbm,
                xg_a, xg_b, y_a, y_b, acc_ref, sem, *, cap):
    e = pl.program_id(1)
    n_e = pl.num_programs(1)
    nb = h_ref.shape[0]
    slots = xg_a.shape[0]
    unroll = ROW_LOOP_UNROLL

    def gather(idx_ref, xg_ref):
        for j in range(slots):
            xg_ref[j] = h_ref[j // cap, idx_ref[0, 0, 0, j]]

    def scatter(idx_ref, gv_ref, y_ref):
        for j0 in range(0, slots, unroll):
            bb = j0 // cap
            rows = [idx_ref[0, 0, 0, j0 + k] for k in range(unroll)]
            new = [acc_ref[bb, rows[k]] + gv_ref[0, 0, 0, j0 + k] * y_ref[j0 + k] for k in range(unroll)]
            for k in range(unroll):
                acc_ref[bb, rows[k]] = new[k]

    def expert(xg_ref, y_ref):
        half = slots // 2
        for r0 in (0, half):
            xg = _tiles_to_rows(xg_ref.at[r0:r0 + half]).astype(BF16)
            a = jnp.dot(xg, wg_ref[0], preferred_element_type=F32)
            u = jnp.dot(xg, wu_ref[0], preferred_element_type=F32)
            _rows_to_tiles(y_ref.at[r0:r0 + half], _bdot(_silu(a) * u, wd_ref[0]))

    @pl.when(e == 0)
    def _():
        acc_ref[...] = jnp.zeros_like(acc_ref)
        y_b[...] = jnp.zeros_like(y_b)
        gather(idx_cur, xg_a)

    def step(xg_cur, y_cur, xg_nxt, y_prv):
        scatter(idx_prev, gv_prev, y_prv)
        expert(xg_cur, y_cur)
        gather(idx_next, xg_nxt)

    @pl.when(e % 2 == 0)
    def _():
        step(xg_a, y_a, xg_b, y_b)

    @pl.when(e % 2 == 1)
    def _():
        step(xg_b, y_b, xg_a, y_a)

    @pl.when(e == n_e - 1)
    def _():
        scatter(idx_cur, gv_cur, y_b)
        cp = pltpu.make_async_copy(acc_ref, out_hbm.at[pl.ds(pl.program_id(0) * nb, nb)], sem)
        cp.start()
        cp.wait()


def _moe_call(idx, gval, h2, wg, wu, wd, row_block, n_rows, group):
    g_, n_e, _, slots = idx.shape
    assert n_e % 2 == 0 and slots % (2 * ROW_LOOP_UNROLL) == 0
    b_ = h2.shape[0]
    d, f = wg.shape[1], wg.shape[2]
    cap = slots // group
    tile = (ROW_SUBLANES, LANES)
    smem = lambda shift: pl.BlockSpec((1, 1, 1, slots), lambda g, e: (g, jnp.clip(e + shift, 0, n_e - 1), 0, 0),
                                      memory_space=pltpu.SMEM)
    rows_buf = pltpu.VMEM((slots,) + tile, F32)
    return pl.pallas_call(
        functools.partial(_moe_kernel, cap=cap),
        grid=(g_, n_e),
        in_specs=[smem(-1), smem(0), smem(1), smem(-1), smem(0),
                  pl.BlockSpec((group, n_rows) + tile, lambda g, e: (g, row_block, 0, 0),
                               pipeline_mode=pl.Buffered(1)),
                  pl.BlockSpec((1, d, f), lambda g, e: (e, 0, 0)),
                  pl.BlockSpec((1, d, f), lambda g, e: (e, 0, 0)),
                  pl.BlockSpec((1, f, d), lambda g, e: (e, 0, 0))],
        out_specs=pl.BlockSpec(memory_space=pl.ANY),
        out_shape=jax.ShapeDtypeStruct((b_, n_rows) + tile, F32),
        scratch_shapes=[rows_buf, rows_buf, rows_buf, rows_buf,
                        pltpu.VMEM((group, n_rows) + tile, F32), pltpu.SemaphoreType.DMA(())],
        compiler_params=_cparams(("arbitrary", "arbitrary"), vmem=60 * 1024 * 1024),
        name="moe_experts",
    )(idx, idx, idx, gval, gval, h2, wg, wu, wd)


def _route(aff, n_lat, n_ctx, with_ctx):
    b_ = aff.shape[0]

    def pick(a, n):
        cap = max(1, CAPACITY_FACTOR * n // N_EXPERTS)
        gval, idx = lax.top_k(jnp.swapaxes(a, 1, 2), cap)
        return gval, idx.astype(jnp.int32), cap

    gl, il, cap_l = pick(aff[:, :n_lat, :N_EXPERTS], n_lat)
    lat = (il.reshape(b_, N_EXPERTS, 1, cap_l), gl.reshape(b_, N_EXPERTS, 1, cap_l))
    if not with_ctx:
        return lat, None
    gc, ic, cap_c = pick(aff[:, n_lat:, :N_EXPERTS], n_ctx)
    flat = lambda t: jnp.transpose(t, (1, 0, 2)).reshape(1, N_EXPERTS, 1, b_ * cap_c)
    return lat, (flat(ic), flat(gc))


def _ln2_kernel(x1_ref, fl_ref, fc_ref, mod_ref, g_ref, b_ref, o_ref, *, alpha, n_lat_tiles):
    d = D_MODEL
    j = pl.program_id(1)
    f = jnp.where(j < n_lat_tiles, _tiles_to_rows(fl_ref.at[0]), _tiles_to_rows(fc_ref.at[0]))
    g2 = mod_ref[0, 0, :, 5 * d:6 * d]
    o_ref[0] = _layer_norm(alpha * x1_ref[0] + g2 * f, g_ref[...], b_ref[...])


def _ln2_call(x1, f_lat, f_ctx, modv, ln_g, ln_b, alpha, n_lat_tiles, n_tiles):
    b_, s, d = x1.shape
    ts = TOKEN_TILE
    return pl.pallas_call(
        functools.partial(_ln2_kernel, alpha=alpha, n_lat_tiles=n_lat_tiles),
        grid=(b_, n_tiles),
        in_specs=[pl.BlockSpec((1, ts, d), lambda b, j: (b, j, 0)),
                  pl.BlockSpec((1, ts, ROW_SUBLANES, LANES), lambda b, j: (b, jnp.minimum(j, n_lat_tiles - 1), 0, 0)),
                  pl.BlockSpec((1, ts, ROW_SUBLANES, LANES), lambda b, j: (b, jnp.maximum(j - n_lat_tiles, 0), 0, 0)),
                  pl.BlockSpec((1, 1, 1, 6 * d), lambda b, j: (b, j // n_lat_tiles, 0, 0)),
                  pl.BlockSpec((1, d), lambda b, j: (0, 0)),
                  pl.BlockSpec((1, d), lambda b, j: (0, 0))],
        out_specs=pl.BlockSpec((1, ts, d), lambda b, j: (b, j, 0)),
        out_shape=jax.ShapeDtypeStruct((b_, n_tiles * ts, d), F32),
        compiler_params=_cparams(("parallel", "arbitrary")),
        name="ln2",
    )(x1, f_lat, f_ctx, modv, ln_g, ln_b)


def _rope_tables(n_lat, n_ctx, dim):
    nf = dim // 4
    t = jnp.arange(n_lat)
    row = (t // GRID_W).astype(F32)
    col = (t % GRID_W).astype(F32)
    inv = ROPE_THETA ** (-jnp.arange(nf, dtype=F32) / nf)
    ar = row[:, None] * inv
    ac = col[:, None] * inv
    cos = jnp.concatenate([jnp.cos(ar), jnp.cos(ar), jnp.cos(ac), jnp.cos(ac)], axis=1)
    sin = jnp.concatenate([-jnp.sin(ar), jnp.sin(ar), -jnp.sin(ac), jnp.sin(ac)], axis=1)
    reps = BRANCH_W // dim
    cos = jnp.concatenate([jnp.tile(cos, (1, reps)), jnp.ones((n_ctx, BRANCH_W), F32)], axis=0)
    sin = jnp.concatenate([jnp.tile(sin, (1, reps)), jnp.zeros((n_ctx, BRANCH_W), F32)], axis=0)
    return cos, sin


def _reorder_w_in(w_in):
    o_ab = 3 * BRANCH_W
    o_gate = o_ab + 4 * N_HEADS
    o_b = o_gate + BRANCH_W
    o_gates = o_b + 3 * BRANCH_W + 4 * BRANCH_W
    pad = jnp.zeros(w_in.shape[:2] + (LANES - 4 * N_HEADS,), w_in.dtype)
    return jnp.concatenate([w_in[..., o_gates:], w_in[..., :o_ab], w_in[..., o_ab:o_gate], pad,
                            w_in[..., o_gate:o_b], w_in[..., o_b:o_gates]], axis=-1).astype(BF16)


def _lane_vec(p):
    v = jnp.zeros((LANES,), F32)
    v = v.at[0:N_HEADS].set(p[0]).at[8:8 + N_HEADS].set(p[1])
    return v.reshape(1, LANES)


def kernel(x, c, ctx, c_ctx, w_mod, b_mod, w_in, conv_a, a_log, dt_bias, gdn_norm, diff_lambda, diff_norm,
           qk_norm_c, sink_d, w_br, w_out, ln1_g, ln1_b, w_router, w_gate_e, w_up_e, w_down_e, ln2_g, ln2_b):
    b_, n_lat, d = x.shape
    n_ctx = ctx.shape[1]
    depth = w_mod.shape[0]
    ts = TOKEN_TILE
    assert d == D_MODEL and n_lat % ts == 0 and n_ctx == ts and n_lat % GRID_W == 0
    assert n_lat >= ts + 2 * WINDOW or n_lat == ts
    n_lat_tiles = n_lat // ts
    n_tiles = n_lat_tiles + n_ctx // ts
    alpha = (2.0 * depth) ** 0.25

    rows = -(-(b_ + 1) // 8) * 8
    cc = jnp.zeros((rows, d), F32).at[:b_].set(c).at[b_].set(c_ctx)
    mod_all = _mod_call(cc, w_mod, b_mod)

    w_all = _reorder_w_in(w_in)
    w_br_b = w_br.astype(BF16)
    w_out_b = w_out.astype(BF16)
    w_r = jnp.concatenate([w_router, jnp.zeros((depth, d, LANES - N_EXPERTS), F32)], axis=-1).astype(BF16)
    wg_b, wu_b, wd_b = w_gate_e.astype(BF16), w_up_e.astype(BF16), w_down_e.astype(BF16)
    conv_w = jnp.concatenate([conv_a, jnp.zeros((depth, 8 - CONV_K, conv_a.shape[2]), F32)], axis=1)
    tabs = _rope_tables(n_lat, n_ctx, HEAD_DIM) + _rope_tables(n_lat, n_ctx, DH_B)

    stream = (jnp.concatenate([x, ctx], axis=1),)
    for l in range(depth):
        last = l == depth - 1
        lam_init = 0.8 - 0.6 * math.exp(-0.3 * l)
        modv = jnp.stack([mod_all[l, :b_], jnp.broadcast_to(mod_all[l, b_], (b_, 6 * d))], axis=1)
        modv = modv.reshape(b_, 2, 1, 6 * d)
        gq = jnp.tile(qk_norm_c[l, 0], N_HEADS).reshape(1, BRANCH_W)
        gk = jnp.tile(qk_norm_c[l, 1], N_HEADS // 2).reshape(1, BRANCH_W // 2)

        proj = _inproj_call(stream, modv, w_all[l], tabs, gq, gk, n_lat_tiles, alpha)
        (gates, aqkv, aab, agate, qb, kb, vb, qc, kc, vc, qd, kd, vd) = proj[:13]
        x_all = stream[0] if l == 0 else proj[13]

        qn, kn, vn, gcol, grow = _gdn_prep_call(aqkv, aab, conv_w[l], _lane_vec(a_log[l]), _lane_vec(dt_bias[l]),
                                                n_lat_tiles)
        local_out = _gdn_local_call(qn, kn, vn, gcol, grow)
        o_f, o_b = _gdn_scan_call(local_out, n_lat // CHUNK, n_ctx // CHUNK)
        gdn = (o_f, o_b, agate, gdn_norm[l].reshape(1, HEAD_DIM))

        nq = n_lat_tiles if last else n_tiles
        out_b = _diff_attn_call(qb, kb, vb, diff_lambda[l], diff_norm[l].reshape(1, HEAD_DIM), lam_init,
                                n_lat, n_ctx, nq)
        out_c = _gqa_call(qc, kc, vc, n_lat, n_ctx, nq)
        out_d = _window_call(qd, kd, vd, sink_d[l], n_lat, n_ctx, nq)

        x1, h2, aff = _merge_call(gdn, (out_b, out_c, out_d), gates, x_all, modv, w_br_b[l], w_out_b[l],
                                  ln1_g[l].reshape(1, d), ln1_b[l].reshape(1, d), w_r[l], alpha,
                                  n_lat_tiles, nq)

        route_lat, route_ctx = _route(aff, n_lat, n_ctx, not last)
        f_lat = _moe_call(route_lat[0], route_lat[1], h2, wg_b[l], wu_b[l], wd_b[l], 0, n_lat, 1)
        if last:
            return _ln2_call(x1, f_lat, f_lat, modv, ln2_g[l].reshape(1, d), ln2_b[l].reshape(1, d), alpha,
                             n_lat_tiles, nq)
        f_ctx = _moe_call(route_ctx[0], route_ctx[1], h2, wg_b[l], wu_b[l], wd_b[l], n_lat // n_ctx, n_ctx, b_)
        stream = (x1, f_lat, f_ctx, modv, ln2_g[l].reshape(1, d), ln2_b[l].reshape(1, d))
```

```python
import functools
import math

import jax
import jax.numpy as jnp
from jax import lax
from jax.experimental import pallas as pl
from jax.experimental.pallas import tpu as pltpu

F32 = jnp.float32
BF16 = jnp.bfloat16

D_MODEL = 1024
HEAD_DIM = 64
N_HEADS = 4
BRANCH_W = 256
N_BRANCH = 4
GRID_W = 64
ROPE_THETA = 10000.0
EPS = 1e-6
CONV_K = 5
CHUNK = 64
DH_B = 32
WINDOW = 128
N_EXPERTS = 16
CAPACITY_FACTOR = 2
LOG2E = 1.4426950408889634

TOKEN_TILE = 256
LANES = 128
VMEM_LIMIT = 56 * 1024 * 1024

GATE_W = N_BRANCH * D_MODEL
C_GATES = 0
C_AQKV = C_GATES + GATE_W
C_AAB = C_AQKV + 3 * BRANCH_W
C_AGATE = C_AAB + LANES
C_B = C_AGATE + BRANCH_W
C_C = C_B + 3 * BRANCH_W
C_D = C_C + 2 * BRANCH_W
W_ALL = C_D + 2 * BRANCH_W


def _cparams(sem, vmem=VMEM_LIMIT):
    return pltpu.CompilerParams(dimension_semantics=sem, vmem_limit_bytes=vmem)


def _bdot(a, b):
    return jnp.dot(a.astype(BF16), b.astype(BF16), preferred_element_type=F32)


def _bdot_nt(a, b):
    return lax.dot_general(a.astype(BF16), b.astype(BF16), (((1,), (1,)), ((), ())),
                           preferred_element_type=F32)


def _silu(x):
    return x * jax.nn.sigmoid(x)


def _block_diag_ones(width, group):
    r = lax.broadcasted_iota(jnp.int32, (width, width), 0) // group
    c = lax.broadcasted_iota(jnp.int32, (width, width), 1) // group
    return jnp.where(r == c, 1.0, 0.0).astype(BF16)


def _group_sum(x, ones_bd):
    hi = x.astype(BF16)
    lo = (x - hi.astype(F32)).astype(BF16)
    return (jnp.dot(hi, ones_bd, preferred_element_type=F32)
            + jnp.dot(lo, ones_bd, preferred_element_type=F32))


def _pair_swap(x, q):
    w = x.shape[-1]
    lane = lax.broadcasted_iota(jnp.int32, x.shape, x.ndim - 1)
    fwd = pltpu.roll(x, w - q, x.ndim - 1)
    bwd = pltpu.roll(x, q, x.ndim - 1)
    return jnp.where((lane % (2 * q)) < q, fwd, bwd)


def _layer_norm(x, g, b):
    mu = jnp.mean(x, axis=-1, keepdims=True)
    xc = x - mu
    var = jnp.mean(xc * xc, axis=-1, keepdims=True)
    return xc * lax.rsqrt(var + EPS) * g + b


def _mod_kernel(c_ref, w_ref, b_ref, o_ref):
    o_ref[0] = _bdot(_silu(c_ref[...]), w_ref[0]) + b_ref[0]


def _mod_call(cc, w_mod, b_mod):
    depth, d, n6 = w_mod.shape
    rows = cc.shape[0]
    nb = 1536
    return pl.pallas_call(
        _mod_kernel,
        grid=(depth, n6 // nb),
        in_specs=[pl.BlockSpec((rows, d), lambda l, j: (0, 0)),
                  pl.BlockSpec((1, d, nb), lambda l, j: (l, 0, j)),
                  pl.BlockSpec((1, 1, nb), lambda l, j: (l, 0, j))],
        out_specs=pl.BlockSpec((1, rows, nb), lambda l, j: (l, 0, j)),
        out_shape=jax.ShapeDtypeStruct((depth, rows, n6), F32),
        compiler_params=_cparams(("arbitrary", "arbitrary")),
        name="mod_vectors",
    )(cc, w_mod, b_mod.reshape(depth, 1, n6))


def _rope(x, cos, sin, quarter):
    return x * cos + _pair_swap(x, quarter) * sin


def _store_heads(ref, x, n_heads):
    for h in range(n_heads):
        ref[0, h] = x[:, h * HEAD_DIM:(h + 1) * HEAD_DIM].astype(ref.dtype)


def _store_key_heads(ref, x, n_heads):
    xt = jnp.transpose(x)
    for h in range(n_heads):
        ref[0, h] = xt[h * HEAD_DIM:(h + 1) * HEAD_DIM, :].astype(ref.dtype)


def _store_value_heads(ref, x, n_heads):
    lane = lax.broadcasted_iota(jnp.int32, (x.shape[0], HEAD_DIM), 1)
    ones_col = jnp.where(lane == 0, 1.0, 0.0)
    for h in range(n_heads):
        ref[0, h] = jnp.concatenate([x[:, h * HEAD_DIM:(h + 1) * HEAD_DIM], ones_col], axis=1).astype(ref.dtype)


def _inproj_kernel(*refs, prev_ln, alpha, n_lat_tiles):
    d = D_MODEL
    if prev_ln:
        x1_ref, fl_ref, fc_ref, pmod_ref, plg_ref, plb_ref = refs[:6]
        refs, xo_ref = refs[6:-1], refs[-1]
        f = jnp.where(pl.program_id(1) < n_lat_tiles, _tiles_to_rows(fl_ref.at[0]), _tiles_to_rows(fc_ref.at[0]))
        x = _layer_norm(alpha * x1_ref[0] + pmod_ref[0, 0, :, 5 * d:6 * d] * f, plg_ref[...], plb_ref[...])
        xo_ref[0] = x
    else:
        x, refs = refs[0][0], refs[1:]
    (mod_ref, w_ref, cos_hd_ref, sin_hd_ref, cos_b_ref, sin_b_ref, gq_ref, gk_ref,
     gates_ref, aqkv_ref, aab_ref, agate_ref,
     qb_ref, kb_ref, vb_ref, qc_ref, kc_ref, vc_ref, qd_ref, kd_ref, vd_ref) = refs
    sh1 = mod_ref[0, 0, :, 0:d]
    sc1 = mod_ref[0, 0, :, d:2 * d]
    h = (x * (1.0 + sc1) + sh1).astype(BF16)

    def proj(a, width):
        return jnp.dot(h, w_ref[:, a:a + width], preferred_element_type=F32)

    gates_ref[0] = proj(C_GATES, GATE_W).astype(gates_ref.dtype)
    aqkv_ref[0] = proj(C_AQKV, 3 * BRANCH_W)
    aab_ref[0] = proj(C_AAB, LANES)
    agate_ref[0] = proj(C_AGATE, BRANCH_W).astype(agate_ref.dtype)

    cos_hd, sin_hd = cos_hd_ref[...], sin_hd_ref[...]
    cos_b, sin_b = cos_b_ref[...], sin_b_ref[...]
    kv_w = BRANCH_W // 2

    pb = proj(C_B, 3 * BRANCH_W)
    qb = _rope(pb[:, 0:BRANCH_W], cos_b, sin_b, DH_B // 4) * (DH_B ** -0.5 * LOG2E)
    kb = _rope(pb[:, BRANCH_W:2 * BRANCH_W], cos_b, sin_b, DH_B // 4)
    _store_heads(qb_ref, qb, N_HEADS)
    _store_key_heads(kb_ref, kb, N_HEADS)
    _store_value_heads(vb_ref, pb[:, 2 * BRANCH_W:3 * BRANCH_W], N_HEADS)

    pc = proj(C_C, 2 * BRANCH_W)
    ones_q = _block_diag_ones(BRANCH_W, HEAD_DIM)
    ones_k = _block_diag_ones(kv_w, HEAD_DIM)
    qc = pc[:, 0:BRANCH_W]
    kc = pc[:, BRANCH_W:BRANCH_W + kv_w]
    qc = qc * lax.rsqrt(_group_sum(qc * qc, ones_q) * (1.0 / HEAD_DIM) + EPS) * gq_ref[...]
    kc = kc * lax.rsqrt(_group_sum(kc * kc, ones_k) * (1.0 / HEAD_DIM) + EPS) * gk_ref[...]
    qc = _rope(qc, cos_hd, sin_hd, HEAD_DIM // 4) * (HEAD_DIM ** -0.5 * LOG2E)
    kc = _rope(kc, cos_hd[:, 0:kv_w], sin_hd[:, 0:kv_w], HEAD_DIM // 4)
    _store_heads(qc_ref, qc, N_HEADS)
    _store_key_heads(kc_ref, kc, N_HEADS // 2)
    _store_value_heads(vc_ref, pc[:, BRANCH_W + kv_w:2 * BRANCH_W], N_HEADS // 2)

    pd = proj(C_D, 2 * BRANCH_W)
    qd = _rope(pd[:, 0:BRANCH_W], cos_hd, sin_hd, HEAD_DIM // 4) * (HEAD_DIM ** -0.5 * LOG2E)
    kd = _rope(pd[:, BRANCH_W:BRANCH_W + kv_w], cos_hd[:, 0:kv_w], sin_hd[:, 0:kv_w], HEAD_DIM // 4)
    _store_heads(qd_ref, qd, N_HEADS)
    _store_key_heads(kd_ref, kd, N_HEADS // 2)
    _store_value_heads(vd_ref, pd[:, BRANCH_W + kv_w:2 * BRANCH_W], N_HEADS // 2)


def _inproj_call(stream, modv, w_all, tabs, gq, gk, n_lat_tiles, alpha):
    prev_ln = len(stream) > 1
    b_, s, d = stream[0].shape
    ts = TOKEN_TILE
    nt = s // ts
    vec = pl.BlockSpec((1, d), lambda b, j: (0, 0))
    tile_rows = lambda index: pl.BlockSpec((1, ts, ROW_SUBLANES, LANES), index)
    mod_spec = pl.BlockSpec((1, 1, 1, 6 * d), lambda b, j: (b, j // n_lat_tiles, 0, 0))
    tok = lambda w: pl.BlockSpec((1, ts, w), lambda b, j: (b, j, 0))
    tab = pl.BlockSpec((ts, BRANCH_W), lambda b, j: (j, 0))
    head = lambda nh, w=HEAD_DIM: pl.BlockSpec((1, nh, ts, w), lambda b, j: (b, 0, j, 0))
    hshape = lambda nh, w=HEAD_DIM: jax.ShapeDtypeStruct((b_, nh, s, w), BF16)
    khead = lambda nh: pl.BlockSpec((1, nh, HEAD_DIM, ts), lambda b, j: (b, 0, 0, j))
    kshape = lambda nh: jax.ShapeDtypeStruct((b_, nh, HEAD_DIM, s), BF16)
    kvh = N_HEADS // 2
    stream_specs = [tok(d)]
    if prev_ln:
        stream_specs += [tile_rows(lambda b, j: (b, jnp.minimum(j, n_lat_tiles - 1), 0, 0)),
                         tile_rows(lambda b, j: (b, jnp.maximum(j - n_lat_tiles, 0), 0, 0)),
                         mod_spec, vec, vec]
    return pl.pallas_call(
        functools.partial(_inproj_kernel, prev_ln=prev_ln, alpha=alpha, n_lat_tiles=n_lat_tiles),
        grid=(b_, nt),
        in_specs=stream_specs + [
                  mod_spec,
                  pl.BlockSpec((d, W_ALL), lambda b, j: (0, 0), pipeline_mode=pl.Buffered(1)),
                  tab, tab, tab, tab,
                  pl.BlockSpec((1, BRANCH_W), lambda b, j: (0, 0)),
                  pl.BlockSpec((1, BRANCH_W // 2), lambda b, j: (0, 0))],
        out_specs=[tok(GATE_W), tok(3 * BRANCH_W), tok(LANES), tok(BRANCH_W),
                   head(N_HEADS), khead(N_HEADS), head(N_HEADS, LANES),
                   head(N_HEADS), khead(kvh), head(kvh, LANES),
                   head(N_HEADS), khead(kvh), head(kvh, LANES)] + ([tok(d)] if prev_ln else []),
        out_shape=[jax.ShapeDtypeStruct((b_, s, GATE_W), BF16),
                   jax.ShapeDtypeStruct((b_, s, 3 * BRANCH_W), F32),
                   jax.ShapeDtypeStruct((b_, s, LANES), F32),
                   jax.ShapeDtypeStruct((b_, s, BRANCH_W), BF16),
                   hshape(N_HEADS), kshape(N_HEADS), hshape(N_HEADS, LANES),
                   hshape(N_HEADS), kshape(kvh), hshape(kvh, LANES),
                   hshape(N_HEADS), kshape(kvh), hshape(kvh, LANES)]
                  + ([jax.ShapeDtypeStruct((b_, s, d), F32)] if prev_ln else []),
        compiler_params=_cparams(("parallel", "arbitrary")),
        name="inproj",
    )(*stream, modv, w_all, *tabs, gq, gk)


NEG_BIG = -1e30


def _softmax_pv(streams, chunks, mask_fn=None):
    m = [st[3] for st in streams]
    acc = [None] * len(streams)
    for i, (q2, k_ref, v_ref, _) in enumerate(streams):
        for ks, n, tag in chunks:
            s = jnp.dot(q2, k_ref[:, pl.ds(ks, n)], preferred_element_type=F32)
            if mask_fn is not None:
                s = mask_fn(s, ks, n, tag)
            cm = jnp.max(s, axis=1, keepdims=True)
            m_new = cm if m[i] is None else jnp.maximum(m[i], cm)
            pv = jnp.dot(jnp.exp2(s - m_new).astype(BF16), v_ref[pl.ds(ks, n), :], preferred_element_type=F32)
            acc[i] = pv if acc[i] is None else acc[i] * jnp.exp2(m[i] - m_new) + pv
            m[i] = m_new
    out = []
    for i, st in enumerate(streams):
        l = acc[i][:, HEAD_DIM:HEAD_DIM + 1]
        if st[3] is not None:
            l = l + jnp.exp2(st[3] - m[i])
        out.append((acc[i][:, 0:HEAD_DIM], l))
    return out


def _key_chunks(start, total, size):
    out, a = [], 0
    while a < total:
        n = min(size, total - a)
        out.append((start + a, n, a))
        a += n
    return out


KEY_CHUNK = 2048


def _diff_attn_kernel(lam_ref, q_ref, k_ref, v_ref, g_ref, o_ref, *, n_lat, n_ctx, lam_init):
    tq = q_ref.shape[2]
    i = pl.program_id(2)
    lv = lam_ref[...]
    lam = (jnp.exp(jnp.sum(lv[0:1] * lv[1:2], axis=1, keepdims=True))
           - jnp.exp(jnp.sum(lv[2:3] * lv[3:4], axis=1, keepdims=True)) + lam_init)
    lane = lax.broadcasted_iota(jnp.int32, (tq, HEAD_DIM), 1)

    def run(chunks):
        streams = []
        for hh in range(N_HEADS):
            q = q_ref[0, hh]
            zero = jnp.zeros_like(q)
            q2 = jnp.concatenate([jnp.where(lane < DH_B, q, zero), jnp.where(lane < DH_B, zero, q)], axis=0)
            streams.append((q2, k_ref.at[0, hh], v_ref.at[0, hh], None))
        for hh, (acc, l) in enumerate(_softmax_pv(streams, chunks)):
            o = acc / l
            o = o[0:tq] - lam * o[tq:2 * tq]
            o = o * lax.rsqrt(jnp.mean(o * o, axis=1, keepdims=True) + EPS) * g_ref[...] * (1.0 - lam_init)
            o_ref[0, :, hh * HEAD_DIM:(hh + 1) * HEAD_DIM] = o.astype(o_ref.dtype)

    @pl.when(i * tq < n_lat)
    def _():
        run(_key_chunks(0, n_lat + n_ctx, KEY_CHUNK))

    @pl.when(i * tq >= n_lat)
    def _():
        run(_key_chunks(n_lat, n_ctx, KEY_CHUNK))


def _gqa_kernel(q_ref, k_ref, v_ref, o_ref, *, n_lat, n_ctx):
    tq = q_ref.shape[2]
    i = pl.program_id(2)

    def run(chunks):
        streams = [(jnp.concatenate([q_ref[0, 2 * g], q_ref[0, 2 * g + 1]], axis=0), k_ref.at[0, g], v_ref.at[0, g],
                    None) for g in range(N_HEADS // 2)]
        for g, (acc, l) in enumerate(_softmax_pv(streams, chunks)):
            o = acc / l
            for j in range(2):
                c0 = (2 * g + j) * HEAD_DIM
                o_ref[0, :, c0:c0 + HEAD_DIM] = o[j * tq:(j + 1) * tq].astype(o_ref.dtype)

    @pl.when(i * tq < n_lat)
    def _():
        run(_key_chunks(0, n_lat + n_ctx, KEY_CHUNK))

    @pl.when(i * tq >= n_lat)
    def _():
        run(_key_chunks(n_lat, n_ctx, KEY_CHUNK))


def _window_kernel(sink_ref, q_ref, k_ref, v_ref, o_ref, *, n_lat, n_ctx, win_keys):
    tq = q_ref.shape[2]
    i = pl.program_id(2)
    row = lax.broadcasted_iota(jnp.int32, (2 * tq, 1), 0)

    def run(chunks, mask):
        streams = []
        for g in range(N_HEADS // 2):
            sink = jnp.where(row < tq, sink_ref[2 * g], sink_ref[2 * g + 1]) * LOG2E
            q2 = jnp.concatenate([q_ref[0, 2 * g], q_ref[0, 2 * g + 1]], axis=0)
            streams.append((q2, k_ref.at[0, g], v_ref.at[0, g], sink))
        for g, (acc, l) in enumerate(_softmax_pv(streams, chunks, mask_fn=mask)):
            o = acc / l
            for j in range(2):
                c0 = (2 * g + j) * HEAD_DIM
                o_ref[0, :, c0:c0 + HEAD_DIM] = o[j * tq:(j + 1) * tq].astype(o_ref.dtype)

    @pl.when(i * tq < n_lat)
    def _():
        start = jnp.clip(i * tq - WINDOW, 0, n_lat - win_keys)
        start = pl.multiple_of(start, WINDOW)

        def mask(s, ks, n, tag):
            if tag < n_ctx:
                return s
            t = i * tq + lax.broadcasted_iota(jnp.int32, s.shape, 0) % tq
            sp = start + lax.broadcasted_iota(jnp.int32, s.shape, 1)
            return jnp.where(jnp.abs(t - sp) <= WINDOW, s, NEG_BIG)

        run([(n_lat, n_ctx, 0), (start, win_keys, n_ctx)], mask)

    @pl.when(i * tq >= n_lat)
    def _():
        run([(n_lat, n_ctx, 0)], None)


def _attn_specs(s, tq, q_heads, kv_heads):
    q_spec = pl.BlockSpec((1, q_heads, tq, HEAD_DIM), lambda b, g, i: (b, 0, i, 0))
    k_spec = pl.BlockSpec((1, kv_heads, HEAD_DIM, s), lambda b, g, i: (b, 0, 0, 0))
    v_spec = pl.BlockSpec((1, kv_heads, s, LANES), lambda b, g, i: (b, 0, 0, 0))
    o_spec = pl.BlockSpec((1, tq, BRANCH_W), lambda b, g, i: (b, i, 0))
    return q_spec, k_spec, v_spec, o_spec


def _diff_attn_call(q, k, v, lam_vecs, norm_g, lam_init, n_lat, n_ctx, n_q_tiles):
    b_, _, s, _ = q.shape
    tq = TOKEN_TILE
    q_spec, k_spec, v_spec, o_spec = _attn_specs(s, tq, N_HEADS, N_HEADS)
    return pl.pallas_call(
        functools.partial(_diff_attn_kernel, n_lat=n_lat, n_ctx=n_ctx, lam_init=lam_init),
        grid=(b_, 1, n_q_tiles),
        in_specs=[pl.BlockSpec((4, DH_B), lambda b, g, i: (0, 0)), q_spec, k_spec, v_spec,
                  pl.BlockSpec((1, HEAD_DIM), lambda b, g, i: (0, 0))],
        out_specs=o_spec,
        out_shape=jax.ShapeDtypeStruct((b_, n_q_tiles * tq, BRANCH_W), BF16),
        compiler_params=_cparams(("parallel", "arbitrary", "arbitrary")),
        name="diff_attn",
    )(lam_vecs, q, k, v, norm_g)


def _gqa_call(q, k, v, n_lat, n_ctx, n_q_tiles):
    b_, _, s, _ = q.shape
    tq = TOKEN_TILE
    q_spec, k_spec, v_spec, o_spec = _attn_specs(s, tq, N_HEADS, N_HEADS // 2)
    return pl.pallas_call(
        functools.partial(_gqa_kernel, n_lat=n_lat, n_ctx=n_ctx),
        grid=(b_, 1, n_q_tiles),
        in_specs=[q_spec, k_spec, v_spec],
        out_specs=o_spec,
        out_shape=jax.ShapeDtypeStruct((b_, n_q_tiles * tq, BRANCH_W), BF16),
        compiler_params=_cparams(("parallel", "arbitrary", "arbitrary")),
        name="gqa_attn",
    )(q, k, v)


def _window_call(q, k, v, sink, n_lat, n_ctx, n_q_tiles):
    b_, _, s, _ = q.shape
    tq = TOKEN_TILE
    win_keys = min(tq + 2 * WINDOW, n_lat)
    q_spec, k_spec, v_spec, o_spec = _attn_specs(s, tq, N_HEADS, N_HEADS // 2)
    return pl.pallas_call(
        functools.partial(_window_kernel, n_lat=n_lat, n_ctx=n_ctx, win_keys=win_keys),
        grid=(b_, 1, n_q_tiles),
        in_specs=[pl.BlockSpec(memory_space=pltpu.SMEM), q_spec, k_spec, v_spec],
        out_specs=o_spec,
        out_shape=jax.ShapeDtypeStruct((b_, n_q_tiles * tq, BRANCH_W), BF16),
        compiler_params=_cparams(("parallel", "arbitrary", "arbitrary")),
        name="window_attn",
    )(sink, q, k, v)


def _row_iota(shape):
    return lax.broadcasted_iota(jnp.int32, shape, 0)


def _chunk_scan(x, pos, forward):
    n = x.shape[0]
    d = 1
    while d < CHUNK:
        if forward:
            x = x + jnp.where(pos >= d, pltpu.roll(x, d, 0), 0.0)
        else:
            x = x + jnp.where(pos < CHUNK - d, pltpu.roll(x, n - d, 0), 0.0)
        d *= 2
    return x


HALO_ROWS = 8


def _gdn_prep_kernel(prev_ref, cur_ref, next_ref, ab_ref, convw_ref, alog_ref, dtb_ref,
                     q_ref, k_ref, v_ref, gcol_ref, grow_ref, *, n_lat_tiles):
    ts = cur_ref.shape[1]
    j = pl.program_id(1)
    has_prev = jnp.logical_and(j != 0, j != n_lat_tiles)
    has_next = jnp.logical_and(j != n_lat_tiles - 1, j < n_lat_tiles)
    cur = cur_ref[0]
    prev8 = jnp.where(has_prev, prev_ref[0], 0.0)
    next8 = jnp.where(has_next, next_ref[0], 0.0)
    reps = ts // HALO_ROWS
    rows = _row_iota(cur.shape)
    half = CONV_K // 2
    y = cur * convw_ref[half:half + 1, :]
    for d in range(1, half + 1):
        back = jnp.where(rows < d, jnp.concatenate([pltpu.roll(prev8, d, 0)] * reps, axis=0), pltpu.roll(cur, d, 0))
        fwd = jnp.where(rows >= ts - d, jnp.concatenate([pltpu.roll(next8, HALO_ROWS - d, 0)] * reps, axis=0),
                        pltpu.roll(cur, ts - d, 0))
        y = y + back * convw_ref[half - d:half - d + 1, :] + fwd * convw_ref[half + d:half + d + 1, :]
    y = _silu(y)
    ones_bd = _block_diag_ones(BRANCH_W, HEAD_DIM)
    q = y[:, 0:BRANCH_W]
    k = y[:, BRANCH_W:2 * BRANCH_W]
    q = q * lax.rsqrt(_group_sum(q * q, ones_bd) + EPS) * (HEAD_DIM ** -0.5)
    k = k * lax.rsqrt(_group_sum(k * k, ones_bd) + EPS)
    _store_heads(q_ref, q, N_HEADS)
    _store_heads(k_ref, k, N_HEADS)
    _store_heads(v_ref, y[:, 2 * BRANCH_W:3 * BRANCH_W], N_HEADS)

    ab = ab_ref[0]
    lane = lax.broadcasted_iota(jnp.int32, ab.shape, 1)
    is_a = (lane % 8) < N_HEADS
    z = ab + dtb_ref[...]
    softplus = jnp.maximum(z, 0.0) + jnp.log(1.0 + jnp.exp(-jnp.abs(z)))
    lg = jnp.where(is_a, -jnp.exp(alog_ref[...]) * softplus, 0.0)
    pos = _row_iota(ab.shape) % CHUNK
    cum = jnp.where(lane < 8, _chunk_scan(lg, pos, True), _chunk_scan(lg, pos, False))
    gates = jnp.where(is_a, cum, jax.nn.sigmoid(ab))
    gcol_ref[0] = gates
    gt = jnp.transpose(gates)
    for c in range(ts // CHUNK):
        grow_ref[0, c] = gt[0:16, c * CHUNK:(c + 1) * CHUNK]


def _gdn_prep_call(aqkv, aab, conv_w, alog_vec, dtb_vec, n_lat_tiles):
    b_, s, _ = aqkv.shape
    ts = TOKEN_TILE
    nt = s // ts
    w3 = 3 * BRANCH_W
    head = pl.BlockSpec((1, N_HEADS, ts, HEAD_DIM), lambda b, j: (b, 0, j, 0))
    hshape = jax.ShapeDtypeStruct((b_, N_HEADS, s, HEAD_DIM), BF16)
    cpt = ts // CHUNK
    hpt = ts // HALO_ROWS
    return pl.pallas_call(
        functools.partial(_gdn_prep_kernel, n_lat_tiles=n_lat_tiles),
        grid=(b_, nt),
        in_specs=[pl.BlockSpec((1, HALO_ROWS, w3), lambda b, j: (b, jnp.maximum(j * hpt - 1, 0), 0)),
                  pl.BlockSpec((1, ts, w3), lambda b, j: (b, j, 0)),
                  pl.BlockSpec((1, HALO_ROWS, w3), lambda b, j: (b, jnp.minimum((j + 1) * hpt, nt * hpt - 1), 0)),
                  pl.BlockSpec((1, ts, LANES), lambda b, j: (b, j, 0)),
                  pl.BlockSpec((8, w3), lambda b, j: (0, 0)),
                  pl.BlockSpec((1, LANES), lambda b, j: (0, 0)),
                  pl.BlockSpec((1, LANES), lambda b, j: (0, 0))],
        out_specs=[head, head, head,
                   pl.BlockSpec((1, ts, LANES), lambda b, j: (b, j, 0)),
                   pl.BlockSpec((1, cpt, 16, CHUNK), lambda b, j: (b, j, 0, 0))],
        out_shape=[hshape, hshape, hshape,
                   jax.ShapeDtypeStruct((b_, s, LANES), F32),
                   jax.ShapeDtypeStruct((b_, s // CHUNK, 16, CHUNK), F32)],
        compiler_params=_cparams(("parallel", "arbitrary")),
        name="gdn_prep",
    )(aqkv, aqkv, aqkv, aab, conv_w, alog_vec, dtb_vec)


def _gdn_local_kernel(q_ref, k_ref, v_ref, gcol_ref, grow_ref,
                      wq_f, p_f, kdt_f, ug_f, wq_b, p_b, kdt_b, ug_b):
    c = CHUNK
    n = N_HEADS * c
    r = lax.broadcasted_iota(jnp.int32, (n, n), 0)
    s_ = lax.broadcasted_iota(jnp.int32, (n, n), 1)
    same_head = (r // c) == (s_ // c)
    eye = jnp.where(r == s_, 1.0, 0.0)
    eye_c = eye[0:c, 0:c]
    untile = jnp.where(lax.broadcasted_iota(jnp.int32, (n, c), 0) % c == lax.broadcasted_iota(jnp.int32, (n, c), 1),
                       1.0, 0.0)
    outs = ((wq_f, p_f, kdt_f, ug_f), (wq_b, p_b, kdt_b, ug_b))
    stack_cols = lambda a, lane0: jnp.concatenate([a[:, lane0 + h:lane0 + h + 1] for h in range(N_HEADS)], axis=0)
    chains = []
    for cc in range(q_ref.shape[2] // c):
        rows = slice(cc * c, (cc + 1) * c)
        q = q_ref[0, :, rows, :].reshape(n, HEAD_DIM).astype(F32)
        k = k_ref[0, :, rows, :].reshape(n, HEAD_DIM).astype(F32)
        v = v_ref[0, :, rows, :].reshape(n, HEAD_DIM).astype(F32)
        gcol = gcol_ref[0, rows, :]
        grow = grow_ref[0, cc]
        kk = _bdot_nt(k, k)
        qk = _bdot_nt(q, k)
        kt = _bdot_nt(eye_c, k)
        for direction in range(2):
            wq_ref, p_ref, kdt_ref, ug_ref = outs[direction]
            base = 8 * direction
            g_c = stack_cols(gcol, base)
            beta_c = stack_cols(gcol, base + N_HEADS)
            g_r = jnp.concatenate([grow[base + h:base + h + 1, :] for h in range(N_HEADS)], axis=1)
            end = c - 1 if direction == 0 else 0
            g_end_r = jnp.concatenate([jnp.broadcast_to(grow[base + h:base + h + 1, end:end + 1], (1, c))
                                       for h in range(N_HEADS)], axis=1)
            g_end_c = jnp.concatenate([jnp.broadcast_to(gcol[end:end + 1, base + h:base + h + 1], (c, 1))
                                       for h in range(N_HEADS)], axis=0)
            if direction == 0:
                incl, strict = same_head & (r >= s_), same_head & (r > s_)
            else:
                incl, strict = same_head & (r <= s_), same_head & (r < s_)
            decay = jnp.exp(jnp.where(incl, g_c - g_r, -jnp.inf))
            a_mat = jnp.where(strict, beta_c * kk * decay, 0.0)
            eg = jnp.exp(g_c)
            rhs = jnp.concatenate([beta_c * eg * k, beta_c * v], axis=1)
            p_ref[0, cc] = _bdot(qk * decay, untile).astype(BF16)
            kdt = kt * jnp.exp(g_end_r - g_r)
            kdt_ref[0, cc] = kdt.astype(BF16)
            ug_ref[0, cc, 1] = jnp.broadcast_to(jnp.exp(g_end_c), (n, HEAD_DIM))
            chains.append(dict(pw=-a_mat, t_inv=eye - a_mat, rhs=rhs, qg=q * eg, cc=cc, wq_ref=wq_ref, ug_ref=ug_ref))

    for _ in range(5):
        for ch in chains:
            ch["pw"] = _bdot(ch["pw"], ch["pw"])
        for ch in chains:
            ch["t_inv"] = ch["t_inv"] + _bdot(ch["t_inv"], ch["pw"])
    for ch in chains:
        cc, wq_ref, ug_ref = ch["cc"], ch["wq_ref"], ch["ug_ref"]
        sol = _bdot(ch["t_inv"], ch["rhs"])
        wq = jnp.concatenate([sol[:, 0:c], ch["qg"]], axis=0)
        wq_ref[0, cc] = wq.astype(BF16)
        ug_ref[0, cc, 0] = sol[:, c:2 * c]


GDN_CHUNKS_PER_STEP = 4
GDN_SCAN_CHUNKS = 2
GDN_SCAN_BATCH = 4


def _gdn_local_call(qn, kn, vn, gcol, grow):
    b_, nh, s, _ = qn.shape
    nch = s // CHUNK
    cps = GDN_CHUNKS_PER_STEP
    n = nh * CHUNK
    head = pl.BlockSpec((1, nh, cps * CHUNK, HEAD_DIM), lambda b, c: (b, 0, c, 0))
    spec = lambda *shape: pl.BlockSpec((1, cps) + shape, lambda b, c: (b, c) + (0,) * len(shape))
    out_specs = [spec(2 * n, CHUNK), spec(n, CHUNK), spec(CHUNK, n), spec(2, n, HEAD_DIM)]
    out_shape = [jax.ShapeDtypeStruct((b_, nch, 2 * n, CHUNK), BF16), jax.ShapeDtypeStruct((b_, nch, n, CHUNK), BF16),
                 jax.ShapeDtypeStruct((b_, nch, CHUNK, n), BF16), jax.ShapeDtypeStruct((b_, nch, 2, n, HEAD_DIM), F32)]
    return pl.pallas_call(
        _gdn_local_kernel,
        grid=(b_, nch // cps),
        in_specs=[head, head, head,
                  pl.BlockSpec((1, cps * CHUNK, LANES), lambda b, c: (b, c, 0)),
                  pl.BlockSpec((1, cps, 16, CHUNK), lambda b, c: (b, c, 0, 0))],
        out_specs=out_specs * 2,
        out_shape=out_shape * 2,
        compiler_params=_cparams(("parallel", "arbitrary")),
        name="gdn_local",
    )(qn, kn, vn, gcol, grow)


def _gdn_scan_kernel(wq_f, p_f, kdt_f, ug_f, wq_b, p_b, kdt_b, ug_b, of_ref, ob_ref, state_ref):
    n = N_HEADS * CHUNK

    @pl.when(pl.program_id(1) == 0)
    def _():
        state_ref[...] = jnp.zeros_like(state_ref)

    nb, nc = wq_f.shape[0], wq_f.shape[1]
    dirs = ((wq_f, p_f, kdt_f, ug_f, of_ref), (wq_b, p_b, kdt_b, ug_b, ob_ref))
    chains = [(bb, d) for bb in range(nb) for d in range(2)]
    st = {ch: state_ref[ch[0], ch[1]] for ch in chains}
    r2 = lax.broadcasted_iota(jnp.int32, (2 * n, n), 0)
    c2 = lax.broadcasted_iota(jnp.int32, (2 * n, n), 1)
    same_head2 = ((r2 % n) // CHUNK) == (c2 // CHUNK)
    same_head = same_head2[0:n]
    tile = jnp.where(lax.broadcasted_iota(jnp.int32, (CHUNK, n), 0)
                     == lax.broadcasted_iota(jnp.int32, (CHUNK, n), 1) % CHUNK, 1.0, 0.0).astype(BF16)
    lane_tile = lambda x: jnp.dot(x, tile, preferred_element_type=F32)
    for t in range(nc):
        cc = {ch: (t if ch[1] == 0 else nc - 1 - t) for ch in chains}
        wq = {ch: jnp.where(same_head2, lane_tile(dirs[ch[1]][0][ch[0], cc[ch]]), 0.0).astype(BF16) for ch in chains}
        pm = {ch: jnp.where(same_head, lane_tile(dirs[ch[1]][1][ch[0], cc[ch]]), 0.0).astype(BF16) for ch in chains}
        kdt = {ch: jnp.where(same_head, jnp.concatenate([dirs[ch[1]][2][ch[0], cc[ch]]] * N_HEADS, axis=0),
                             jnp.zeros((), BF16)) for ch in chains}
        ws = {ch: jnp.dot(wq[ch], st[ch].astype(BF16), preferred_element_type=F32) for ch in chains}
        u_b = {ch: (dirs[ch[1]][3][ch[0], cc[ch], 0] - ws[ch][0:n]).astype(BF16) for ch in chains}
        for ch in chains:
            bb, d = ch
            o = ws[ch][n:2 * n] + jnp.dot(pm[ch], u_b[ch], preferred_element_type=F32)
            for h in range(N_HEADS):
                dirs[d][4][bb, h, cc[ch] * CHUNK:(cc[ch] + 1) * CHUNK, :] = o[h * CHUNK:(h + 1) * CHUNK]
        st = {ch: dirs[ch[1]][3][ch[0], cc[ch], 1] * st[ch]
              + jnp.dot(kdt[ch], u_b[ch], preferred_element_type=F32) for ch in chains}
    for ch in chains:
        state_ref[ch[0], ch[1]] = st[ch]


def _gdn_scan_call(local_out, n_lat_chunks, n_ctx_chunks):
    wq = local_out[0]
    b_, nch = wq.shape[0], wq.shape[1]
    nh = N_HEADS
    n = nh * CHUNK
    s = nch * CHUNK
    nc = GDN_SCAN_CHUNKS
    assert n_lat_chunks % nc == 0 and n_ctx_chunks % nc == 0
    nlc, ncc = n_lat_chunks // nc, n_ctx_chunks // nc

    def order_f(i):
        return jnp.where(i < ncc, nlc + i, i - ncc)

    def order_b(i):
        return jnp.where(i < ncc, nlc + ncc - 1 - i, nlc - 1 - (i - ncc))

    nb = max(g for g in (GDN_SCAN_BATCH, 2, 1) if b_ % g == 0)

    def specs(order):
        sp = lambda *shape: pl.BlockSpec((nb, nc) + shape, lambda b, i: (b, order(i)) + (0,) * len(shape))
        return [sp(2 * n, CHUNK), sp(n, CHUNK), sp(CHUNK, n), sp(2, n, HEAD_DIM)]

    oo = lambda order: pl.BlockSpec((nb, nh, nc * CHUNK, HEAD_DIM), lambda b, i: (b, 0, order(i), 0))
    oshape = jax.ShapeDtypeStruct((b_, nh, s, HEAD_DIM), F32)
    return pl.pallas_call(
        _gdn_scan_kernel,
        grid=(b_ // nb, nch // nc),
        in_specs=specs(order_f) + specs(order_b),
        out_specs=[oo(order_f), oo(order_b)],
        out_shape=[oshape, oshape],
        scratch_shapes=[pltpu.VMEM((nb, 2, n, HEAD_DIM), F32)],
        compiler_params=_cparams(("parallel", "arbitrary")),
        name="gdn_scan",
    )(*local_out)


def _gdn_output(of_ref, ob_ref, gate_ref, g_ref):
    gate = gate_ref[0].astype(F32)
    heads = []
    for h in range(N_HEADS):
        o = of_ref[0, h] + ob_ref[0, h]
        o = o * lax.rsqrt(jnp.mean(o * o, axis=1, keepdims=True) + EPS) * g_ref[...]
        heads.append(o * _silu(gate[:, h * HEAD_DIM:(h + 1) * HEAD_DIM]))
    return jnp.concatenate(heads, axis=1).astype(BF16)


def _merge_kernel(of_ref, obw_ref, agate_ref, gn_ref, ob_ref, oc_ref, od_ref, gates_ref, x_ref, mod_ref, wbr_ref,
                  wout_ref, lng_ref, lnb_ref, wr_ref, x1_ref, h2_ref, aff_ref, *, alpha):
    d = D_MODEL
    branch_out = (_gdn_output(of_ref, obw_ref, agate_ref, gn_ref), ob_ref[0], oc_ref[0], od_ref[0])
    m = None
    for i, o in enumerate(branch_out):
        gate = jax.nn.sigmoid(gates_ref[0, :, i * d:(i + 1) * d].astype(F32))
        term = gate * jnp.dot(o, wbr_ref[i], preferred_element_type=F32)
        m = term if m is None else m + term
    y = _bdot(m, wout_ref[...])
    g1 = mod_ref[0, 0, :, 2 * d:3 * d]
    sh2 = mod_ref[0, 0, :, 3 * d:4 * d]
    sc2 = mod_ref[0, 0, :, 4 * d:5 * d]
    x1 = _layer_norm(alpha * x_ref[0] + g1 * y, lng_ref[...], lnb_ref[...])
    x1_ref[0] = x1
    h2 = x1 * (1.0 + sc2) + sh2
    _rows_to_tiles(h2_ref.at[0], h2)
    logits = _bdot(h2, wr_ref[...])
    lane = lax.broadcasted_iota(jnp.int32, logits.shape, 1)
    logits = jnp.where(lane < N_EXPERTS, logits, NEG_BIG)
    e = jnp.exp(logits - jnp.max(logits, axis=1, keepdims=True))
    aff_ref[0] = e / jnp.sum(e, axis=1, keepdims=True)


def _merge_call(gdn, outs, gates, x_all, modv, w_br, w_out, ln_g, ln_b, w_router, alpha, n_lat_tiles, n_tiles):
    b_, s, d = x_all.shape
    ts = TOKEN_TILE
    tok = lambda w: pl.BlockSpec((1, ts, w), lambda b, j: (b, j, 0))
    head = pl.BlockSpec((1, N_HEADS, ts, HEAD_DIM), lambda b, j: (b, 0, j, 0))
    const2 = lambda shape: pl.BlockSpec(shape, lambda b, j: (0,) * len(shape))
    return pl.pallas_call(
        functools.partial(_merge_kernel, alpha=alpha),
        grid=(b_, n_tiles),
        in_specs=[head, head, tok(BRANCH_W), const2((1, HEAD_DIM))] + [tok(BRANCH_W)] * 3 + [
            tok(GATE_W), tok(d),
            pl.BlockSpec((1, 1, 1, 6 * d), lambda b, j: (b, j // n_lat_tiles, 0, 0)),
            const2((N_BRANCH, BRANCH_W, d)), const2((d, d)), const2((1, d)), const2((1, d)),
            const2((d, LANES))],
        out_specs=[tok(d), pl.BlockSpec((1, ts, ROW_SUBLANES, LANES), lambda b, j: (b, j, 0, 0)), tok(LANES)],
        out_shape=[jax.ShapeDtypeStruct((b_, n_tiles * ts, d), F32),
                   jax.ShapeDtypeStruct((b_, n_tiles * ts, ROW_SUBLANES, LANES), F32),
                   jax.ShapeDtypeStruct((b_, n_tiles * ts, LANES), F32)],
        compiler_params=_cparams(("parallel", "arbitrary")),
        name="merge_ln1_router",
    )(*gdn, *outs, gates, x_all, modv, w_br, w_out, ln_g, ln_b, w_router)


ROW_LOOP_UNROLL = 8
ROW_SUBLANES = D_MODEL // LANES


def _rows_to_tiles(ref, x):
    for sl in range(ROW_SUBLANES):
        ref[:, sl, :] = x[:, sl * LANES:(sl + 1) * LANES]


def _tiles_to_rows(ref):
    return jnp.concatenate([ref[:, sl, :] for sl in range(ROW_SUBLANES)], axis=1)


def _moe_kernel(idx_prev, idx_cur, idx_next, gv_prev, gv_cur, h_ref, wg_ref, wu_ref, wd_ref, out_hbm,
                xg_a, xg_b, y_a, y_b, acc_ref, sem, *, cap):
    e = pl.program_id(1)
    n_e = pl.num_programs(1)
    nb = h_ref.shape[0]
    slots = xg_a.shape[0]
    unroll = ROW_LOOP_UNROLL

    def gather(idx_ref, xg_ref):
        for j in range(slots):
            xg_ref[j] = h_ref[j // cap, idx_ref[0, 0, 0, j]]

    def scatter(idx_ref, gv_ref, y_ref):
        for j0 in range(0, slots, unroll):
            bb = j0 // cap
            rows = [idx_ref[0, 0, 0, j0 + k] for k in range(unroll)]
            new = [acc_ref[bb, rows[k]] + gv_ref[0, 0, 0, j0 + k] * y_ref[j0 + k] for k in range(unroll)]
            for k in range(unroll):
                acc_ref[bb, rows[k]] = new[k]

    def expert(xg_ref, y_ref):
        half = slots // 2
        for r0 in (0, half):
            xg = _tiles_to_rows(xg_ref.at[r0:r0 + half]).astype(BF16)
            a = jnp.dot(xg, wg_ref[0], preferred_element_type=F32)
            u = jnp.dot(xg, wu_ref[0], preferred_element_type=F32)
            _rows_to_tiles(y_ref.at[r0:r0 + half], _bdot(_silu(a) * u, wd_ref[0]))

    @pl.when(e == 0)
    def _():
        acc_ref[...] = jnp.zeros_like(acc_ref)
        y_b[...] = jnp.zeros_like(y_b)
        gather(idx_cur, xg_a)

    def step(xg_cur, y_cur, xg_nxt, y_prv):
        scatter(idx_prev, gv_prev, y_prv)
        expert(xg_cur, y_cur)
        gather(idx_next, xg_nxt)

    @pl.when(e % 2 == 0)
    def _():
        step(xg_a, y_a, xg_b, y_b)

    @pl.when(e % 2 == 1)
    def _():
        step(xg_b, y_b, xg_a, y_a)

    @pl.when(e == n_e - 1)
    def _():
        scatter(idx_cur, gv_cur, y_b)
        cp = pltpu.make_async_copy(acc_ref, out_hbm.at[pl.ds(pl.program_id(0) * nb, nb)], sem)
        cp.start()
        cp.wait()


def _moe_call(idx, gval, h2, wg, wu, wd, row_block, n_rows, group):
    g_, n_e, _, slots = idx.shape
    assert n_e % 2 == 0 and slots % (2 * ROW_LOOP_UNROLL) == 0
    b_ = h2.shape[0]
    d, f = wg.shape[1], wg.shape[2]
    cap = slots // group
    tile = (ROW_SUBLANES, LANES)
    smem = lambda shift: pl.BlockSpec((1, 1, 1, slots), lambda g, e: (g, jnp.clip(e + shift, 0, n_e - 1), 0, 0),
                                      memory_space=pltpu.SMEM)
    rows_buf = pltpu.VMEM((slots,) + tile, F32)
    return pl.pallas_call(
        functools.partial(_moe_kernel, cap=cap),
        grid=(g_, n_e),
        in_specs=[smem(-1), smem(0), smem(1), smem(-1), smem(0),
                  pl.BlockSpec((group, n_rows) + tile, lambda g, e: (g, row_block, 0, 0),
                               pipeline_mode=pl.Buffered(1)),
                  pl.BlockSpec((1, d, f), lambda g, e: (e, 0, 0)),
                  pl.BlockSpec((1, d, f), lambda g, e: (e, 0, 0)),
                  pl.BlockSpec((1, f, d), lambda g, e: (e, 0, 0))],
        out_specs=pl.BlockSpec(memory_space=pl.ANY),
        out_shape=jax.ShapeDtypeStruct((b_, n_rows) + tile, F32),
        scratch_shapes=[rows_buf, rows_buf, rows_buf, rows_buf,
                        pltpu.VMEM((group, n_rows) + tile, F32), pltpu.SemaphoreType.DMA(())],
        compiler_params=_cparams(("arbitrary", "arbitrary"), vmem=60 * 1024 * 1024),
        name="moe_experts",
    )(idx, idx, idx, gval, gval, h2, wg, wu, wd)


def _route(aff, n_lat, n_ctx, with_ctx):
    b_ = aff.shape[0]

    def pick(a, n):
        cap = max(1, CAPACITY_FACTOR * n // N_EXPERTS)
        gval, idx = lax.top_k(jnp.swapaxes(a, 1, 2), cap)
        return gval, idx.astype(jnp.int32), cap

    gl, il, cap_l = pick(aff[:, :n_lat, :N_EXPERTS], n_lat)
    lat = (il.reshape(b_, N_EXPERTS, 1, cap_l), gl.reshape(b_, N_EXPERTS, 1, cap_l))
    if not with_ctx:
        return lat, None
    gc, ic, cap_c = pick(aff[:, n_lat:, :N_EXPERTS], n_ctx)
    flat = lambda t: jnp.transpose(t, (1, 0, 2)).reshape(1, N_EXPERTS, 1, b_ * cap_c)
    return lat, (flat(ic), flat(gc))


def _ln2_kernel(x1_ref, fl_ref, fc_ref, mod_ref, g_ref, b_ref, o_ref, *, alpha, n_lat_tiles):
    d = D_MODEL
    j = pl.program_id(1)
    f = jnp.where(j < n_lat_tiles, _tiles_to_rows(fl_ref.at[0]), _tiles_to_rows(fc_ref.at[0]))
    g2 = mod_ref[0, 0, :, 5 * d:6 * d]
    o_ref[0] = _layer_norm(alpha * x1_ref[0] + g2 * f, g_ref[...], b_ref[...])


def _ln2_call(x1, f_lat, f_ctx, modv, ln_g, ln_b, alpha, n_lat_tiles, n_tiles):
    b_, s, d = x1.shape
    ts = TOKEN_TILE
    return pl.pallas_call(
        functools.partial(_ln2_kernel, alpha=alpha, n_lat_tiles=n_lat_tiles),
        grid=(b_, n_tiles),
        in_specs=[pl.BlockSpec((1, ts, d), lambda b, j: (b, j, 0)),
                  pl.BlockSpec((1, ts, ROW_SUBLANES, LANES), lambda b, j: (b, jnp.minimum(j, n_lat_tiles - 1), 0, 0)),
                  pl.BlockSpec((1, ts, ROW_SUBLANES, LANES), lambda b, j: (b, jnp.maximum(j - n_lat_tiles, 0), 0, 0)),
                  pl.BlockSpec((1, 1, 1, 6 * d), lambda b, j: (b, j // n_lat_tiles, 0, 0)),
                  pl.BlockSpec((1, d), lambda b, j: (0, 0)),
                  pl.BlockSpec((1, d), lambda b, j: (0, 0))],
        out_specs=pl.BlockSpec((1, ts, d), lambda b, j: (b, j, 0)),
        out_shape=jax.ShapeDtypeStruct((b_, n_tiles * ts, d), F32),
        compiler_params=_cparams(("parallel", "arbitrary")),
        name="ln2",
    )(x1, f_lat, f_ctx, modv, ln_g, ln_b)


def _rope_tables(n_lat, n_ctx, dim):
    nf = dim // 4
    t = jnp.arange(n_lat)
    row = (t // GRID_W).astype(F32)
    col = (t % GRID_W).astype(F32)
    inv = ROPE_THETA ** (-jnp.arange(nf, dtype=F32) / nf)
    ar = row[:, None] * inv
    ac = col[:, None] * inv
    cos = jnp.concatenate([jnp.cos(ar), jnp.cos(ar), jnp.cos(ac), jnp.cos(ac)], axis=1)
    sin = jnp.concatenate([-jnp.sin(ar), jnp.sin(ar), -jnp.sin(ac), jnp.sin(ac)], axis=1)
    reps = BRANCH_W // dim
    cos = jnp.concatenate([jnp.tile(cos, (1, reps)), jnp.ones((n_ctx, BRANCH_W), F32)], axis=0)
    sin = jnp.concatenate([jnp.tile(sin, (1, reps)), jnp.zeros((n_ctx, BRANCH_W), F32)], axis=0)
    return cos, sin


def _reorder_w_in(w_in):
    o_ab = 3 * BRANCH_W
    o_gate = o_ab + 4 * N_HEADS
    o_b = o_gate + BRANCH_W
    o_gates = o_b + 3 * BRANCH_W + 4 * BRANCH_W
    pad = jnp.zeros(w_in.shape[:2] + (LANES - 4 * N_HEADS,), w_in.dtype)
    return jnp.concatenate([w_in[..., o_gates:], w_in[..., :o_ab], w_in[..., o_ab:o_gate], pad,
                            w_in[..., o_gate:o_b], w_in[..., o_b:o_gates]], axis=-1).astype(BF16)


def _lane_vec(p):
    v = jnp.zeros((LANES,), F32)
    v = v.at[0:N_HEADS].set(p[0]).at[8:8 + N_HEADS].set(p[1])
    return v.reshape(1, LANES)


def kernel(x, c, ctx, c_ctx, w_mod, b_mod, w_in, conv_a, a_log, dt_bias, gdn_norm, diff_lambda, diff_norm,
           qk_norm_c, sink_d, w_br, w_out, ln1_g, ln1_b, w_router, w_gate_e, w_up_e, w_down_e, ln2_g, ln2_b):
    b_, n_lat, d = x.shape
    n_ctx = ctx.shape[1]
    depth = w_mod.shape[0]
    ts = TOKEN_TILE
    assert d == D_MODEL and n_lat % ts == 0 and n_ctx == ts and n_lat % GRID_W == 0
    assert n_lat >= ts + 2 * WINDOW or n_lat == ts
    n_lat_tiles = n_lat // ts
    n_tiles = n_lat_tiles + n_ctx // ts
    alpha = (2.0 * depth) ** 0.25

    rows = -(-(b_ + 1) // 8) * 8
    cc = jnp.zeros((rows, d), F32).at[:b_].set(c).at[b_].set(c_ctx)
    mod_all = _mod_call(cc, w_mod, b_mod)

    w_all = _reorder_w_in(w_in)
    w_br_b = w_br.astype(BF16)
    w_out_b = w_out.astype(BF16)
    w_r = jnp.concatenate([w_router, jnp.zeros((depth, d, LANES - N_EXPERTS), F32)], axis=-1).astype(BF16)
    wg_b, wu_b, wd_b = w_gate_e.astype(BF16), w_up_e.astype(BF16), w_down_e.astype(BF16)
    conv_w = jnp.concatenate([conv_a, jnp.zeros((depth, 8 - CONV_K, conv_a.shape[2]), F32)], axis=1)
    tabs = _rope_tables(n_lat, n_ctx, HEAD_DIM) + _rope_tables(n_lat, n_ctx, DH_B)

    stream = (jnp.concatenate([x, ctx], axis=1),)
    for l in range(depth):
        last = l == depth - 1
        lam_init = 0.8 - 0.6 * math.exp(-0.3 * l)
        modv = jnp.stack([mod_all[l, :b_], jnp.broadcast_to(mod_all[l, b_], (b_, 6 * d))], axis=1)
        modv = modv.reshape(b_, 2, 1, 6 * d)
        gq = jnp.tile(qk_norm_c[l, 0], N_HEADS).reshape(1, BRANCH_W)
        gk = jnp.tile(qk_norm_c[l, 1], N_HEADS // 2).reshape(1, BRANCH_W // 2)

        proj = _inproj_call(stream, modv, w_all[l], tabs, gq, gk, n_lat_tiles, alpha)
        (gates, aqkv, aab, agate, qb, kb, vb, qc, kc, vc, qd, kd, vd) = proj[:13]
        x_all = stream[0] if l == 0 else proj[13]

        qn, kn, vn, gcol, grow = _gdn_prep_call(aqkv, aab, conv_w[l], _lane_vec(a_log[l]), _lane_vec(dt_bias[l]),
                                                n_lat_tiles)
        local_out = _gdn_local_call(qn, kn, vn, gcol, grow)
        o_f, o_b = _gdn_scan_call(local_out, n_lat // CHUNK, n_ctx // CHUNK)
        gdn = (o_f, o_b, agate, gdn_norm[l].reshape(1, HEAD_DIM))

        nq = n_lat_tiles if last else n_tiles
        out_b = _diff_attn_call(qb, kb, vb, diff_lambda[l], diff_norm[l].reshape(1, HEAD_DIM), lam_init,
                                n_lat, n_ctx, nq)
        out_c = _gqa_call(qc, kc, vc, n_lat, n_ctx, nq)
        out_d = _window_call(qd, kd, vd, sink_d[l], n_lat, n_ctx, nq)

        x1, h2, aff = _merge_call(gdn, (out_b, out_c, out_d), gates, x_all, modv, w_br_b[l], w_out_b[l],
                                  ln1_g[l].reshape(1, d), ln1_b[l].reshape(1, d), w_r[l], alpha,
                                  n_lat_tiles, nq)

        route_lat, route_ctx = _route(aff, n_lat, n_ctx, not last)
        f_lat = _moe_call(route_lat[0], route_lat[1], h2, wg_b[l], wu_b[l], wd_b[l], 0, n_lat, 1)
        if last:
            return _ln2_call(x1, f_lat, f_lat, modv, ln2_g[l].reshape(1, d), ln2_b[l].reshape(1, d), alpha,
                             n_lat_tiles, nq)
        f_ctx = _moe_call(route_ctx[0], route_ctx[1], h2, wg_b[l], wu_b[l], wd_b[l], n_lat // n_ctx, n_ctx, b_)
        stream = (x1, f_lat, f_ctx, modv, ln2_g[l].reshape(1, d), ln2_b[l].reshape(1, d))
```

```python
import functools
import math

import jax
import jax.numpy as jnp
from jax import lax
from jax.experimental import pallas as pl
from jax.experimental.pallas import tpu as pltpu

F32 = jnp.float32
BF16 = jnp.bfloat16

D_MODEL = 1024
HEAD_DIM = 64
N_HEADS = 4
BRANCH_W = 256
N_BRANCH = 4
GRID_W = 64
ROPE_THETA = 10000.0
EPS = 1e-6
CONV_K = 5
CHUNK = 64
DH_B = 32
WINDOW = 128
N_EXPERTS = 16
CAPACITY_FACTOR = 2
LOG2E = 1.4426950408889634

TOKEN_TILE = 256
LANES = 128
VMEM_LIMIT = 56 * 1024 * 1024

GATE_W = N_BRANCH * D_MODEL
C_GATES = 0
C_AQKV = C_GATES + GATE_W
C_AAB = C_AQKV + 3 * BRANCH_W
C_AGATE = C_AAB + LANES
C_B = C_AGATE + BRANCH_W
C_C = C_B + 3 * BRANCH_W
C_D = C_C + 2 * BRANCH_W
W_ALL = C_D + 2 * BRANCH_W


def _cparams(sem, vmem=VMEM_LIMIT):
    return pltpu.CompilerParams(dimension_semantics=sem, vmem_limit_bytes=vmem)


def _bdot(a, b):
    return jnp.dot(a.astype(BF16), b.astype(BF16), preferred_element_type=F32)


def _bdot_nt(a, b):
    return lax.dot_general(a.astype(BF16), b.astype(BF16), (((1,), (1,)), ((), ())),
                           preferred_element_type=F32)


def _silu(x):
    return x * jax.nn.sigmoid(x)


def _block_diag_ones(width, group):
    r = lax.broadcasted_iota(jnp.int32, (width, width), 0) // group
    c = lax.broadcasted_iota(jnp.int32, (width, width), 1) // group
    return jnp.where(r == c, 1.0, 0.0).astype(BF16)


def _group_sum(x, ones_bd):
    hi = x.astype(BF16)
    lo = (x - hi.astype(F32)).astype(BF16)
    return (jnp.dot(hi, ones_bd, preferred_element_type=F32)
            + jnp.dot(lo, ones_bd, preferred_element_type=F32))


def _pair_swap(x, q):
    w = x.shape[-1]
    lane = lax.broadcasted_iota(jnp.int32, x.shape, x.ndim - 1)
    fwd = pltpu.roll(x, w - q, x.ndim - 1)
    bwd = pltpu.roll(x, q, x.ndim - 1)
    return jnp.where((lane % (2 * q)) < q, fwd, bwd)


def _layer_norm(x, g, b):
    mu = jnp.mean(x, axis=-1, keepdims=True)
    xc = x - mu
    var = jnp.mean(xc * xc, axis=-1, keepdims=True)
    return xc * lax.rsqrt(var + EPS) * g + b


def _mod_kernel(c_ref, w_ref, b_ref, o_ref):
    o_ref[0] = _bdot(_silu(c_ref[...]), w_ref[0]) + b_ref[0]


def _mod_call(cc, w_mod, b_mod):
    depth, d, n6 = w_mod.shape
    rows = cc.shape[0]
    nb = 1536
    return pl.pallas_call(
        _mod_kernel,
        grid=(depth, n6 // nb),
        in_specs=[pl.BlockSpec((rows, d), lambda l, j: (0, 0)),
                  pl.BlockSpec((1, d, nb), lambda l, j: (l, 0, j)),
                  pl.BlockSpec((1, 1, nb), lambda l, j: (l, 0, j))],
        out_specs=pl.BlockSpec((1, rows, nb), lambda l, j: (l, 0, j)),
        out_shape=jax.ShapeDtypeStruct((depth, rows, n6), F32),
        compiler_params=_cparams(("arbitrary", "arbitrary")),
        name="mod_vectors",
    )(cc, w_mod, b_mod.reshape(depth, 1, n6))


def _rope(x, cos, sin, quarter):
    return x * cos + _pair_swap(x, quarter) * sin


def _store_heads(ref, x, n_heads):
    for h in range(n_heads):
        ref[0, h] = x[:, h * HEAD_DIM:(h + 1) * HEAD_DIM].astype(ref.dtype)


def _store_key_heads(ref, x, n_heads):
    xt = jnp.transpose(x)
    for h in range(n_heads):
        ref[0, h] = xt[h * HEAD_DIM:(h + 1) * HEAD_DIM, :].astype(ref.dtype)


def _store_value_heads(ref, x, n_heads):
    lane = lax.broadcasted_iota(jnp.int32, (x.shape[0], HEAD_DIM), 1)
    ones_col = jnp.where(lane == 0, 1.0, 0.0)
    for h in range(n_heads):
        ref[0, h] = jnp.concatenate([x[:, h * HEAD_DIM:(h + 1) * HEAD_DIM], ones_col], axis=1).astype(ref.dtype)


def _inproj_kernel(*refs, prev_ln, alpha, n_lat_tiles):
    d = D_MODEL
    if prev_ln:
        x1_ref, fl_ref, fc_ref, pmod_ref, plg_ref, plb_ref = refs[:6]
        refs, xo_ref = refs[6:-1], refs[-1]
        f = jnp.where(pl.program_id(1) < n_lat_tiles, _tiles_to_rows(fl_ref.at[0]), _tiles_to_rows(fc_ref.at[0]))
        x = _layer_norm(alpha * x1_ref[0] + pmod_ref[0, 0, :, 5 * d:6 * d] * f, plg_ref[...], plb_ref[...])
        xo_ref[0] = x
    else:
        x, refs = refs[0][0], refs[1:]
    (mod_ref, w_ref, cos_hd_ref, sin_hd_ref, cos_b_ref, sin_b_ref, gq_ref, gk_ref,
     gates_ref, aqkv_ref, aab_ref, agate_ref,
     qb_ref, kb_ref, vb_ref, qc_ref, kc_ref, vc_ref, qd_ref, kd_ref, vd_ref) = refs
    sh1 = mod_ref[0, 0, :, 0:d]
    sc1 = mod_ref[0, 0, :, d:2 * d]
    h = (x * (1.0 + sc1) + sh1).astype(BF16)

    def proj(a, width):
        return jnp.dot(h, w_ref[:, a:a + width], preferred_element_type=F32)

    gates_ref[0] = proj(C_GATES, GATE_W).astype(gates_ref.dtype)
    aqkv_ref[0] = proj(C_AQKV, 3 * BRANCH_W)
    aab_ref[0] = proj(C_AAB, LANES)
    agate_ref[0] = proj(C_AGATE, BRANCH_W).astype(agate_ref.dtype)

    cos_hd, sin_hd = cos_hd_ref[...], sin_hd_ref[...]
    cos_b, sin_b = cos_b_ref[...], sin_b_ref[...]
    kv_w = BRANCH_W // 2

    pb = proj(C_B, 3 * BRANCH_W)
    qb = _rope(pb[:, 0:BRANCH_W], cos_b, sin_b, DH_B // 4) * (DH_B ** -0.5 * LOG2E)
    kb = _rope(pb[:, BRANCH_W:2 * BRANCH_W], cos_b, sin_b, DH_B // 4)
    _store_heads(qb_ref, qb, N_HEADS)
    _store_key_heads(kb_ref, kb, N_HEADS)
    _store_value_heads(vb_ref, pb[:, 2 * BRANCH_W:3 * BRANCH_W], N_HEADS)

    pc = proj(C_C, 2 * BRANCH_W)
    ones_q = _block_diag_ones(BRANCH_W, HEAD_DIM)
    ones_k = _block_diag_ones(kv_w, HEAD_DIM)
    qc = pc[:, 0:BRANCH_W]
    kc = pc[:, BRANCH_W:BRANCH_W + kv_w]
    qc = qc * lax.rsqrt(_group_sum(qc * qc, ones_q) * (1.0 / HEAD_DIM) + EPS) * gq_ref[...]
    kc = kc * lax.rsqrt(_group_sum(kc * kc, ones_k) * (1.0 / HEAD_DIM) + EPS) * gk_ref[...]
    qc = _rope(qc, cos_hd, sin_hd, HEAD_DIM // 4) * (HEAD_DIM ** -0.5 * LOG2E)
    kc = _rope(kc, cos_hd[:, 0:kv_w], sin_hd[:, 0:kv_w], HEAD_DIM // 4)
    _store_heads(qc_ref, qc, N_HEADS)
    _store_key_heads(kc_ref, kc, N_HEADS // 2)
    _store_value_heads(vc_ref, pc[:, BRANCH_W + kv_w:2 * BRANCH_W], N_HEADS // 2)

    pd = proj(C_D, 2 * BRANCH_W)
    qd = _rope(pd[:, 0:BRANCH_W], cos_hd, sin_hd, HEAD_DIM // 4) * (HEAD_DIM ** -0.5 * LOG2E)
    kd = _rope(pd[:, BRANCH_W:BRANCH_W + kv_w], cos_hd[:, 0:kv_w], sin_hd[:, 0:kv_w], HEAD_DIM // 4)
    _store_heads(qd_ref, qd, N_HEADS)
    _store_key_heads(kd_ref, kd, N_HEADS // 2)
    _store_value_heads(vd_ref, pd[:, BRANCH_W + kv_w:2 * BRANCH_W], N_HEADS // 2)


def _inproj_call(stream, modv, w_all, tabs, gq, gk, n_lat_tiles, alpha):
    prev_ln = len(stream) > 1
    b_, s, d = stream[0].shape
    ts = TOKEN_TILE
    nt = s // ts
    vec = pl.BlockSpec((1, d), lambda b, j: (0, 0))
    tile_rows = lambda index: pl.BlockSpec((1, ts, ROW_SUBLANES, LANES), index)
    mod_spec = pl.BlockSpec((1, 1, 1, 6 * d), lambda b, j: (b, j // n_lat_tiles, 0, 0))
    tok = lambda w: pl.BlockSpec((1, ts, w), lambda b, j: (b, j, 0))
    tab = pl.BlockSpec((ts, BRANCH_W), lambda b, j: (j, 0))
    head = lambda nh, w=HEAD_DIM: pl.BlockSpec((1, nh, ts, w), lambda b, j: (b, 0, j, 0))
    hshape = lambda nh, w=HEAD_DIM: jax.ShapeDtypeStruct((b_, nh, s, w), BF16)
    khead = lambda nh: pl.BlockSpec((1, nh, HEAD_DIM, ts), lambda b, j: (b, 0, 0, j))
    kshape = lambda nh: jax.ShapeDtypeStruct((b_, nh, HEAD_DIM, s), BF16)
    kvh = N_HEADS // 2
    stream_specs = [tok(d)]
    if prev_ln:
        stream_specs += [tile_rows(lambda b, j: (b, jnp.minimum(j, n_lat_tiles - 1), 0, 0)),
                         tile_rows(lambda b, j: (b, jnp.maximum(j - n_lat_tiles, 0), 0, 0)),
                         mod_spec, vec, vec]
    return pl.pallas_call(
        functools.partial(_inproj_kernel, prev_ln=prev_ln, alpha=alpha, n_lat_tiles=n_lat_tiles),
        grid=(b_, nt),
        in_specs=stream_specs + [
                  mod_spec,
                  pl.BlockSpec((d, W_ALL), lambda b, j: (0, 0), pipeline_mode=pl.Buffered(1)),
                  tab, tab, tab, tab,
                  pl.BlockSpec((1, BRANCH_W), lambda b, j: (0, 0)),
                  pl.BlockSpec((1, BRANCH_W // 2), lambda b, j: (0, 0))],
        out_specs=[tok(GATE_W), tok(3 * BRANCH_W), tok(LANES), tok(BRANCH_W),
                   head(N_HEADS), khead(N_HEADS), head(N_HEADS, LANES),
                   head(N_HEADS), khead(kvh), head(kvh, LANES),
                   head(N_HEADS), khead(kvh), head(kvh, LANES)] + ([tok(d)] if prev_ln else []),
        out_shape=[jax.ShapeDtypeStruct((b_, s, GATE_W), BF16),
                   jax.ShapeDtypeStruct((b_, s, 3 * BRANCH_W), F32),
                   jax.ShapeDtypeStruct((b_, s, LANES), F32),
                   jax.ShapeDtypeStruct((b_, s, BRANCH_W), BF16),
                   hshape(N_HEADS), kshape(N_HEADS), hshape(N_HEADS, LANES),
                   hshape(N_HEADS), kshape(kvh), hshape(kvh, LANES),
                   hshape(N_HEADS), kshape(kvh), hshape(kvh, LANES)]
                  + ([jax.ShapeDtypeStruct((b_, s, d), F32)] if prev_ln else []),
        compiler_params=_cparams(("parallel", "arbitrary")),
        name="inproj",
    )(*stream, modv, w_all, *tabs, gq, gk)


NEG_BIG = -1e30


def _softmax_pv(streams, chunks, mask_fn=None):
    m = [st[3] for st in streams]
    acc = [None] * len(streams)
    for i, (q2, k_ref, v_ref, _) in enumerate(streams):
        for ks, n, tag in chunks:
            s = jnp.dot(q2, k_ref[:, pl.ds(ks, n)], preferred_element_type=F32)
            if mask_fn is not None:
                s = mask_fn(s, ks, n, tag)
            cm = jnp.max(s, axis=1, keepdims=True)
            m_new = cm if m[i] is None else jnp.maximum(m[i], cm)
            pv = jnp.dot(jnp.exp2(s - m_new).astype(BF16), v_ref[pl.ds(ks, n), :], preferred_element_type=F32)
            acc[i] = pv if acc[i] is None else acc[i] * jnp.exp2(m[i] - m_new) + pv
            m[i] = m_new
    out = []
    for i, st in enumerate(streams):
        l = acc[i][:, HEAD_DIM:HEAD_DIM + 1]
        if st[3] is not None:
            l = l + jnp.exp2(st[3] - m[i])
        out.append((acc[i][:, 0:HEAD_DIM], l))
    return out


def _key_chunks(start, total, size):
    out, a = [], 0
    while a < total:
        n = min(size, total - a)
        out.append((start + a, n, a))
        a += n
    return out


KEY_CHUNK = 2048
DIFF_Q_TILE = 512


def _diff_attn_kernel(lam_ref, q_ref, k_ref, v_ref, g_ref, o_ref, *, n_lat, n_ctx, lam_init):
    tq = q_ref.shape[2]
    i = pl.program_id(2)
    lv = lam_ref[...]
    lam = (jnp.exp(jnp.sum(lv[0:1] * lv[1:2], axis=1, keepdims=True))
           - jnp.exp(jnp.sum(lv[2:3] * lv[3:4], axis=1, keepdims=True)) + lam_init)
    key_chunk = KEY_CHUNK * TOKEN_TILE // tq

    def run(chunks, rows):
        lane = lax.broadcasted_iota(jnp.int32, (rows, HEAD_DIM), 1)
        streams = []
        for hh in range(N_HEADS):
            q = q_ref[0, hh, 0:rows, :]
            zero = jnp.zeros_like(q)
            q2 = jnp.concatenate([jnp.where(lane < DH_B, q, zero), jnp.where(lane < DH_B, zero, q)], axis=0)
            streams.append((q2, k_ref.at[0, hh], v_ref.at[0, hh], None))
        for hh, (acc, l) in enumerate(_softmax_pv(streams, chunks)):
            o = acc / l
            o = o[0:rows] - lam * o[rows:2 * rows]
            o = o * lax.rsqrt(jnp.mean(o * o, axis=1, keepdims=True) + EPS) * g_ref[...] * (1.0 - lam_init)
            o_ref[0, 0:rows, hh * HEAD_DIM:(hh + 1) * HEAD_DIM] = o.astype(o_ref.dtype)

    @pl.when(i * tq < n_lat)
    def _():
        run(_key_chunks(0, n_lat + n_ctx, key_chunk), tq)

    @pl.when(i * tq >= n_lat)
    def _():
        run(_key_chunks(n_lat, n_ctx, key_chunk), min(tq, n_ctx))


def _gqa_kernel(q_ref, k_ref, v_ref, o_ref, *, n_lat, n_ctx):
    tq = q_ref.shape[2]
    i = pl.program_id(2)

    def run(chunks):
        streams = [(jnp.concatenate([q_ref[0, 2 * g], q_ref[0, 2 * g + 1]], axis=0), k_ref.at[0, g], v_ref.at[0, g],
                    None) for g in range(N_HEADS // 2)]
        for g, (acc, l) in enumerate(_softmax_pv(streams, chunks)):
            o = acc / l
            for j in range(2):
                c0 = (2 * g + j) * HEAD_DIM
                o_ref[0, :, c0:c0 + HEAD_DIM] = o[j * tq:(j + 1) * tq].astype(o_ref.dtype)

    @pl.when(i * tq < n_lat)
    def _():
        run(_key_chunks(0, n_lat + n_ctx, KEY_CHUNK))

    @pl.when(i * tq >= n_lat)
    def _():
        run(_key_chunks(n_lat, n_ctx, KEY_CHUNK))


def _window_kernel(sink_ref, q_ref, k_ref, v_ref, o_ref, *, n_lat, n_ctx, win_keys):
    tq = q_ref.shape[2]
    i = pl.program_id(2)
    row = lax.broadcasted_iota(jnp.int32, (2 * tq, 1), 0)

    def run(chunks, mask):
        streams = []
        for g in range(N_HEADS // 2):
            sink = jnp.where(row < tq, sink_ref[2 * g], sink_ref[2 * g + 1]) * LOG2E
            q2 = jnp.concatenate([q_ref[0, 2 * g], q_ref[0, 2 * g + 1]], axis=0)
            streams.append((q2, k_ref.at[0, g], v_ref.at[0, g], sink))
        for g, (acc, l) in enumerate(_softmax_pv(streams, chunks, mask_fn=mask)):
            o = acc / l
            for j in range(2):
                c0 = (2 * g + j) * HEAD_DIM
                o_ref[0, :, c0:c0 + HEAD_DIM] = o[j * tq:(j + 1) * tq].astype(o_ref.dtype)

    @pl.when(i * tq < n_lat)
    def _():
        start = jnp.clip(i * tq - WINDOW, 0, n_lat - win_keys)
        start = pl.multiple_of(start, WINDOW)

        def mask(s, ks, n, tag):
            if tag < n_ctx:
                return s
            t = i * tq + lax.broadcasted_iota(jnp.int32, s.shape, 0) % tq
            sp = start + lax.broadcasted_iota(jnp.int32, s.shape, 1)
            return jnp.where(jnp.abs(t - sp) <= WINDOW, s, NEG_BIG)

        run([(n_lat, n_ctx, 0), (start, win_keys, n_ctx)], mask)

    @pl.when(i * tq >= n_lat)
    def _():
        run([(n_lat, n_ctx, 0)], None)


def _attn_specs(s, tq, q_heads, kv_heads):
    q_spec = pl.BlockSpec((1, q_heads, tq, HEAD_DIM), lambda b, g, i: (b, 0, i, 0))
    k_spec = pl.BlockSpec((1, kv_heads, HEAD_DIM, s), lambda b, g, i: (b, 0, 0, 0))
    v_spec = pl.BlockSpec((1, kv_heads, s, LANES), lambda b, g, i: (b, 0, 0, 0))
    o_spec = pl.BlockSpec((1, tq, BRANCH_W), lambda b, g, i: (b, i, 0))
    return q_spec, k_spec, v_spec, o_spec


def _diff_attn_call(q, k, v, lam_vecs, norm_g, lam_init, n_lat, n_ctx, n_q_tiles):
    b_, _, s, _ = q.shape
    tq, n_rows = DIFF_Q_TILE, n_q_tiles * TOKEN_TILE
    q_spec, k_spec, v_spec, o_spec = _attn_specs(s, tq, N_HEADS, N_HEADS)
    return pl.pallas_call(
        functools.partial(_diff_attn_kernel, n_lat=n_lat, n_ctx=n_ctx, lam_init=lam_init),
        grid=(b_, 1, pl.cdiv(n_rows, tq)),
        in_specs=[pl.BlockSpec((4, DH_B), lambda b, g, i: (0, 0)), q_spec, k_spec, v_spec,
                  pl.BlockSpec((1, HEAD_DIM), lambda b, g, i: (0, 0))],
        out_specs=o_spec,
        out_shape=jax.ShapeDtypeStruct((b_, n_rows, BRANCH_W), BF16),
        compiler_params=_cparams(("parallel", "arbitrary", "arbitrary")),
        name="diff_attn",
    )(lam_vecs, q, k, v, norm_g)


def _gqa_call(q, k, v, n_lat, n_ctx, n_q_tiles):
    b_, _, s, _ = q.shape
    tq = TOKEN_TILE
    q_spec, k_spec, v_spec, o_spec = _attn_specs(s, tq, N_HEADS, N_HEADS // 2)
    return pl.pallas_call(
        functools.partial(_gqa_kernel, n_lat=n_lat, n_ctx=n_ctx),
        grid=(b_, 1, n_q_tiles),
        in_specs=[q_spec, k_spec, v_spec],
        out_specs=o_spec,
        out_shape=jax.ShapeDtypeStruct((b_, n_q_tiles * tq, BRANCH_W), BF16),
        compiler_params=_cparams(("parallel", "arbitrary", "arbitrary")),
        name="gqa_attn",
    )(q, k, v)


def _window_call(q, k, v, sink, n_lat, n_ctx, n_q_tiles):
    b_, _, s, _ = q.shape
    tq = TOKEN_TILE
    win_keys = min(tq + 2 * WINDOW, n_lat)
    q_spec, k_spec, v_spec, o_spec = _attn_specs(s, tq, N_HEADS, N_HEADS // 2)
    return pl.pallas_call(
        functools.partial(_window_kernel, n_lat=n_lat, n_ctx=n_ctx, win_keys=win_keys),
        grid=(b_, 1, n_q_tiles),
        in_specs=[pl.BlockSpec(memory_space=pltpu.SMEM), q_spec, k_spec, v_spec],
        out_specs=o_spec,
        out_shape=jax.ShapeDtypeStruct((b_, n_q_tiles * tq, BRANCH_W), BF16),
        compiler_params=_cparams(("parallel", "arbitrary", "arbitrary")),
        name="window_attn",
    )(sink, q, k, v)


def _row_iota(shape):
    return lax.broadcasted_iota(jnp.int32, shape, 0)


def _chunk_scan(x, pos, forward):
    n = x.shape[0]
    d = 1
    while d < CHUNK:
        if forward:
            x = x + jnp.where(pos >= d, pltpu.roll(x, d, 0), 0.0)
        else:
            x = x + jnp.where(pos < CHUNK - d, pltpu.roll(x, n - d, 0), 0.0)
        d *= 2
    return x


HALO_ROWS = 8


def _gdn_prep_kernel(prev_ref, cur_ref, next_ref, ab_ref, convw_ref, alog_ref, dtb_ref,
                     q_ref, k_ref, v_ref, gcol_ref, grow_ref, *, n_lat_tiles):
    ts = cur_ref.shape[1]
    j = pl.program_id(1)
    has_prev = jnp.logical_and(j != 0, j != n_lat_tiles)
    has_next = jnp.logical_and(j != n_lat_tiles - 1, j < n_lat_tiles)
    cur = cur_ref[0]
    prev8 = jnp.where(has_prev, prev_ref[0], 0.0)
    next8 = jnp.where(has_next, next_ref[0], 0.0)
    reps = ts // HALO_ROWS
    rows = _row_iota(cur.shape)
    half = CONV_K // 2
    y = cur * convw_ref[half:half + 1, :]
    for d in range(1, half + 1):
        back = jnp.where(rows < d, jnp.concatenate([pltpu.roll(prev8, d, 0)] * reps, axis=0), pltpu.roll(cur, d, 0))
        fwd = jnp.where(rows >= ts - d, jnp.concatenate([pltpu.roll(next8, HALO_ROWS - d, 0)] * reps, axis=0),
                        pltpu.roll(cur, ts - d, 0))
        y = y + back * convw_ref[half - d:half - d + 1, :] + fwd * convw_ref[half + d:half + d + 1, :]
    y = _silu(y)
    ones_bd = _block_diag_ones(BRANCH_W, HEAD_DIM)
    q = y[:, 0:BRANCH_W]
    k = y[:, BRANCH_W:2 * BRANCH_W]
    q = q * lax.rsqrt(_group_sum(q * q, ones_bd) + EPS) * (HEAD_DIM ** -0.5)
    k = k * lax.rsqrt(_group_sum(k * k, ones_bd) + EPS)
    _store_heads(q_ref, q, N_HEADS)
    _store_heads(k_ref, k, N_HEADS)
    _store_heads(v_ref, y[:, 2 * BRANCH_W:3 * BRANCH_W], N_HEADS)

    ab = ab_ref[0]
    lane = lax.broadcasted_iota(jnp.int32, ab.shape, 1)
    is_a = (lane % 8) < N_HEADS
    z = ab + dtb_ref[...]
    softplus = jnp.maximum(z, 0.0) + jnp.log(1.0 + jnp.exp(-jnp.abs(z)))
    lg = jnp.where(is_a, -jnp.exp(alog_ref[...]) * softplus, 0.0)
    pos = _row_iota(ab.shape) % CHUNK
    cum = jnp.where(lane < 8, _chunk_scan(lg, pos, True), _chunk_scan(lg, pos, False))
    gates = jnp.where(is_a, cum, jax.nn.sigmoid(ab))
    gcol_ref[0] = gates
    gt = jnp.transpose(gates)
    for c in range(ts // CHUNK):
        grow_ref[0, c] = gt[0:16, c * CHUNK:(c + 1) * CHUNK]


def _gdn_prep_call(aqkv, aab, conv_w, alog_vec, dtb_vec, n_lat_tiles):
    b_, s, _ = aqkv.shape
    ts = TOKEN_TILE
    nt = s // ts
    w3 = 3 * BRANCH_W
    head = pl.BlockSpec((1, N_HEADS, ts, HEAD_DIM), lambda b, j: (b, 0, j, 0))
    hshape = jax.ShapeDtypeStruct((b_, N_HEADS, s, HEAD_DIM), F32)
    cpt = ts // CHUNK
    hpt = ts // HALO_ROWS
    return pl.pallas_call(
        functools.partial(_gdn_prep_kernel, n_lat_tiles=n_lat_tiles),
        grid=(b_, nt),
        in_specs=[pl.BlockSpec((1, HALO_ROWS, w3), lambda b, j: (b, jnp.maximum(j * hpt - 1, 0), 0)),
                  pl.BlockSpec((1, ts, w3), lambda b, j: (b, j, 0)),
                  pl.BlockSpec((1, HALO_ROWS, w3), lambda b, j: (b, jnp.minimum((j + 1) * hpt, nt * hpt - 1), 0)),
                  pl.BlockSpec((1, ts, LANES), lambda b, j: (b, j, 0)),
                  pl.BlockSpec((8, w3), lambda b, j: (0, 0)),
                  pl.BlockSpec((1, LANES), lambda b, j: (0, 0)),
                  pl.BlockSpec((1, LANES), lambda b, j: (0, 0))],
        out_specs=[head, head, head,
                   pl.BlockSpec((1, ts, LANES), lambda b, j: (b, j, 0)),
                   pl.BlockSpec((1, cpt, 16, CHUNK), lambda b, j: (b, j, 0, 0))],
        out_shape=[hshape, hshape, hshape,
                   jax.ShapeDtypeStruct((b_, s, LANES), F32),
                   jax.ShapeDtypeStruct((b_, s // CHUNK, 16, CHUNK), F32)],
        compiler_params=_cparams(("parallel", "arbitrary")),
        name="gdn_prep",
    )(aqkv, aqkv, aqkv, aab, conv_w, alog_vec, dtb_vec)


def _gdn_local_kernel(q_ref, k_ref, v_ref, gcol_ref, grow_ref,
                      wq_f, p_f, kdt_f, ug_f, wq_b, p_b, kdt_b, ug_b):
    c = CHUNK
    n = N_HEADS * c
    r = lax.broadcasted_iota(jnp.int32, (n, n), 0)
    s_ = lax.broadcasted_iota(jnp.int32, (n, n), 1)
    same_head = (r // c) == (s_ // c)
    eye = jnp.where(r == s_, 1.0, 0.0)
    eye_c = eye[0:c, 0:c]
    untile = jnp.where(lax.broadcasted_iota(jnp.int32, (n, c), 0) % c == lax.broadcasted_iota(jnp.int32, (n, c), 1),
                       1.0, 0.0)
    outs = ((wq_f, p_f, kdt_f, ug_f), (wq_b, p_b, kdt_b, ug_b))
    stack_cols = lambda a, lane0: jnp.concatenate([a[:, lane0 + h:lane0 + h + 1] for h in range(N_HEADS)], axis=0)
    chains = []
    for cc in range(q_ref.shape[2] // c):
        rows = slice(cc * c, (cc + 1) * c)
        q = q_ref[0, :, rows, :].reshape(n, HEAD_DIM)
        k = k_ref[0, :, rows, :].reshape(n, HEAD_DIM)
        v = v_ref[0, :, rows, :].reshape(n, HEAD_DIM)
        gcol = gcol_ref[0, rows, :]
        grow = grow_ref[0, cc]
        kk = _bdot_nt(k, k)
        qk = _bdot_nt(q, k)
        kt = _bdot_nt(eye_c, k)
        for direction in range(2):
            wq_ref, p_ref, kdt_ref, ug_ref = outs[direction]
            base = 8 * direction
            g_c = stack_cols(gcol, base)
            beta_c = stack_cols(gcol, base + N_HEADS)
            g_r = jnp.concatenate([grow[base + h:base + h + 1, :] for h in range(N_HEADS)], axis=1)
            end = c - 1 if direction == 0 else 0
            g_end_r = jnp.concatenate([jnp.broadcast_to(grow[base + h:base + h + 1, end:end + 1], (1, c))
                                       for h in range(N_HEADS)], axis=1)
            g_end_c = jnp.concatenate([jnp.broadcast_to(gcol[end:end + 1, base + h:base + h + 1], (c, 1))
                                       for h in range(N_HEADS)], axis=0)
            if direction == 0:
                incl, strict = same_head & (r >= s_), same_head & (r > s_)
            else:
                incl, strict = same_head & (r <= s_), same_head & (r < s_)
            decay = jnp.exp(jnp.where(incl, g_c - g_r, -jnp.inf))
            a_mat = jnp.where(strict, beta_c * kk * decay, 0.0)
            eg = jnp.exp(g_c)
            rhs = jnp.concatenate([beta_c * eg * k, beta_c * v], axis=1)
            p_ref[0, cc] = _bdot(qk * decay, untile).astype(BF16)
            kdt = kt * jnp.exp(g_end_r - g_r)
            kdt_ref[0, cc] = kdt.astype(BF16)
            ug_ref[0, cc, 1] = jnp.broadcast_to(jnp.exp(g_end_c), (n, HEAD_DIM))
            chains.append(dict(pw=-a_mat, t_inv=eye - a_mat, rhs=rhs, qg=q * eg, cc=cc, wq_ref=wq_ref, ug_ref=ug_ref))

    for _ in range(5):
        for ch in chains:
            ch["pw"] = _bdot(ch["pw"], ch["pw"])
        for ch in chains:
            ch["t_inv"] = ch["t_inv"] + _bdot(ch["t_inv"], ch["pw"])
    for ch in chains:
        cc, wq_ref, ug_ref = ch["cc"], ch["wq_ref"], ch["ug_ref"]
        sol = _bdot(ch["t_inv"], ch["rhs"])
        wq = jnp.concatenate([sol[:, 0:c], ch["qg"]], axis=0)
        wq_ref[0, cc] = wq.astype(BF16)
        ug_ref[0, cc, 0] = sol[:, c:2 * c]


GDN_CHUNKS_PER_STEP = 4
GDN_SCAN_CHUNKS = 2
GDN_SCAN_BATCH = 4


def _gdn_local_call(qn, kn, vn, gcol, grow):
    b_, nh, s, _ = qn.shape
    nch = s // CHUNK
    cps = GDN_CHUNKS_PER_STEP
    n = nh * CHUNK
    head = pl.BlockSpec((1, nh, cps * CHUNK, HEAD_DIM), lambda b, c: (b, 0, c, 0))
    spec = lambda *shape: pl.BlockSpec((1, cps) + shape, lambda b, c: (b, c) + (0,) * len(shape))
    out_specs = [spec(2 * n, CHUNK), spec(n, CHUNK), spec(CHUNK, n), spec(2, n, HEAD_DIM)]
    out_shape = [jax.ShapeDtypeStruct((b_, nch, 2 * n, CHUNK), BF16), jax.ShapeDtypeStruct((b_, nch, n, CHUNK), BF16),
                 jax.ShapeDtypeStruct((b_, nch, CHUNK, n), BF16), jax.ShapeDtypeStruct((b_, nch, 2, n, HEAD_DIM), F32)]
    return pl.pallas_call(
        _gdn_local_kernel,
        grid=(b_, nch // cps),
        in_specs=[head, head, head,
                  pl.BlockSpec((1, cps * CHUNK, LANES), lambda b, c: (b, c, 0)),
                  pl.BlockSpec((1, cps, 16, CHUNK), lambda b, c: (b, c, 0, 0))],
        out_specs=out_specs * 2,
        out_shape=out_shape * 2,
        compiler_params=_cparams(("parallel", "arbitrary")),
        name="gdn_local",
    )(qn, kn, vn, gcol, grow)


def _gdn_scan_kernel(wq_f, p_f, kdt_f, ug_f, wq_b, p_b, kdt_b, ug_b, of_ref, ob_ref, state_ref):
    n = N_HEADS * CHUNK

    @pl.when(pl.program_id(1) == 0)
    def _():
        state_ref[...] = jnp.zeros_like(state_ref)

    nb, nc = wq_f.shape[0], wq_f.shape[1]
    dirs = ((wq_f, p_f, kdt_f, ug_f, of_ref), (wq_b, p_b, kdt_b, ug_b, ob_ref))
    chains = [(bb, d) for bb in range(nb) for d in range(2)]
    st = {ch: state_ref[ch[0], ch[1]] for ch in chains}
    r2 = lax.broadcasted_iota(jnp.int32, (2 * n, n), 0)
    c2 = lax.broadcasted_iota(jnp.int32, (2 * n, n), 1)
    same_head2 = ((r2 % n) // CHUNK) == (c2 // CHUNK)
    same_head = same_head2[0:n]
    tile = jnp.where(lax.broadcasted_iota(jnp.int32, (CHUNK, n), 0)
                     == lax.broadcasted_iota(jnp.int32, (CHUNK, n), 1) % CHUNK, 1.0, 0.0).astype(BF16)
    lane_tile = lambda x: jnp.dot(x, tile, preferred_element_type=F32)
    for t in range(nc):
        cc = {ch: (t if ch[1] == 0 else nc - 1 - t) for ch in chains}
        wq = {ch: jnp.where(same_head2, lane_tile(dirs[ch[1]][0][ch[0], cc[ch]]), 0.0).astype(BF16) for ch in chains}
        pm = {ch: jnp.where(same_head, lane_tile(dirs[ch[1]][1][ch[0], cc[ch]]), 0.0).astype(BF16) for ch in chains}
        kdt = {ch: jnp.where(same_head, jnp.concatenate([dirs[ch[1]][2][ch[0], cc[ch]]] * N_HEADS, axis=0),
                             jnp.zeros((), BF16)) for ch in chains}
        ws = {ch: jnp.dot(wq[ch], st[ch].astype(BF16), preferred_element_type=F32) for ch in chains}
        u_b = {ch: (dirs[ch[1]][3][ch[0], cc[ch], 0] - ws[ch][0:n]).astype(BF16) for ch in chains}
        for ch in chains:
            bb, d = ch
            o = ws[ch][n:2 * n] + jnp.dot(pm[ch], u_b[ch], preferred_element_type=F32)
            for h in range(N_HEADS):
                dirs[d][4][bb, h, cc[ch] * CHUNK:(cc[ch] + 1) * CHUNK, :] = o[h * CHUNK:(h + 1) * CHUNK]
        st = {ch: dirs[ch[1]][3][ch[0], cc[ch], 1] * st[ch]
              + jnp.dot(kdt[ch], u_b[ch], preferred_element_type=F32) for ch in chains}
    for ch in chains:
        state_ref[ch[0], ch[1]] = st[ch]


def _gdn_scan_call(local_out, n_lat_chunks, n_ctx_chunks):
    wq = local_out[0]
    b_, nch = wq.shape[0], wq.shape[1]
    nh = N_HEADS
    n = nh * CHUNK
    s = nch * CHUNK
    nc = GDN_SCAN_CHUNKS
    assert n_lat_chunks % nc == 0 and n_ctx_chunks % nc == 0
    nlc, ncc = n_lat_chunks // nc, n_ctx_chunks // nc

    def order_f(i):
        return jnp.where(i < ncc, nlc + i, i - ncc)

    def order_b(i):
        return jnp.where(i < ncc, nlc + ncc - 1 - i, nlc - 1 - (i - ncc))

    nb = max(g for g in (GDN_SCAN_BATCH, 2, 1) if b_ % g == 0)

    def specs(order):
        sp = lambda *shape: pl.BlockSpec((nb, nc) + shape, lambda b, i: (b, order(i)) + (0,) * len(shape))
        return [sp(2 * n, CHUNK), sp(n, CHUNK), sp(CHUNK, n), sp(2, n, HEAD_DIM)]

    oo = lambda order: pl.BlockSpec((nb, nh, nc * CHUNK, HEAD_DIM), lambda b, i: (b, 0, order(i), 0))
    oshape = jax.ShapeDtypeStruct((b_, nh, s, HEAD_DIM), F32)
    return pl.pallas_call(
        _gdn_scan_kernel,
        grid=(b_ // nb, nch // nc),
        in_specs=specs(order_f) + specs(order_b),
        out_specs=[oo(order_f), oo(order_b)],
        out_shape=[oshape, oshape],
        scratch_shapes=[pltpu.VMEM((nb, 2, n, HEAD_DIM), F32)],
        compiler_params=_cparams(("parallel", "arbitrary")),
        name="gdn_scan",
    )(*local_out)


def _gdn_output(of_ref, ob_ref, gate_ref, g_ref):
    gate = gate_ref[0].astype(F32)
    heads = []
    for h in range(N_HEADS):
        o = of_ref[0, h] + ob_ref[0, h]
        o = o * lax.rsqrt(jnp.mean(o * o, axis=1, keepdims=True) + EPS) * g_ref[...]
        heads.append(o * _silu(gate[:, h * HEAD_DIM:(h + 1) * HEAD_DIM]))
    return jnp.concatenate(heads, axis=1).astype(BF16)


def _merge_kernel(of_ref, obw_ref, agate_ref, gn_ref, ob_ref, oc_ref, od_ref, gates_ref, x_ref, mod_ref, wbr_ref,
                  wout_ref, lng_ref, lnb_ref, wr_ref, x1_ref, h2_ref, aff_ref, *, alpha):
    d = D_MODEL
    branch_out = (_gdn_output(of_ref, obw_ref, agate_ref, gn_ref), ob_ref[0], oc_ref[0], od_ref[0])
    m = None
    for i, o in enumerate(branch_out):
        gate = jax.nn.sigmoid(gates_ref[0, :, i * d:(i + 1) * d].astype(F32))
        term = gate * jnp.dot(o, wbr_ref[i], preferred_element_type=F32)
        m = term if m is None else m + term
    y = _bdot(m, wout_ref[...])
    g1 = mod_ref[0, 0, :, 2 * d:3 * d]
    sh2 = mod_ref[0, 0, :, 3 * d:4 * d]
    sc2 = mod_ref[0, 0, :, 4 * d:5 * d]
    x1 = _layer_norm(alpha * x_ref[0] + g1 * y, lng_ref[...], lnb_ref[...])
    x1_ref[0] = x1
    h2 = x1 * (1.0 + sc2) + sh2
    _rows_to_tiles(h2_ref.at[0], h2)
    logits = _bdot(h2, wr_ref[...])
    lane = lax.broadcasted_iota(jnp.int32, logits.shape, 1)
    logits = jnp.where(lane < N_EXPERTS, logits, NEG_BIG)
    e = jnp.exp(logits - jnp.max(logits, axis=1, keepdims=True))
    aff_ref[0] = e / jnp.sum(e, axis=1, keepdims=True)


def _merge_call(gdn, outs, gates, x_all, modv, w_br, w_out, ln_g, ln_b, w_router, alpha, n_lat_tiles, n_tiles):
    b_, s, d = x_all.shape
    ts = TOKEN_TILE
    tok = lambda w: pl.BlockSpec((1, ts, w), lambda b, j: (b, j, 0))
    head = pl.BlockSpec((1, N_HEADS, ts, HEAD_DIM), lambda b, j: (b, 0, j, 0))
    const2 = lambda shape: pl.BlockSpec(shape, lambda b, j: (0,) * len(shape))
    return pl.pallas_call(
        functools.partial(_merge_kernel, alpha=alpha),
        grid=(b_, n_tiles),
        in_specs=[head, head, tok(BRANCH_W), const2((1, HEAD_DIM))] + [tok(BRANCH_W)] * 3 + [
            tok(GATE_W), tok(d),
            pl.BlockSpec((1, 1, 1, 6 * d), lambda b, j: (b, j // n_lat_tiles, 0, 0)),
            const2((N_BRANCH, BRANCH_W, d)), const2((d, d)), const2((1, d)), const2((1, d)),
            const2((d, LANES))],
        out_specs=[tok(d), pl.BlockSpec((1, ts, ROW_SUBLANES, LANES), lambda b, j: (b, j, 0, 0)), tok(LANES)],
        out_shape=[jax.ShapeDtypeStruct((b_, n_tiles * ts, d), F32),
                   jax.ShapeDtypeStruct((b_, n_tiles * ts, ROW_SUBLANES, LANES), F32),
                   jax.ShapeDtypeStruct((b_, n_tiles * ts, LANES), F32)],
        compiler_params=_cparams(("parallel", "arbitrary")),
        name="merge_ln1_router",
    )(*gdn, *outs, gates, x_all, modv, w_br, w_out, ln_g, ln_b, w_router)


ROW_LOOP_UNROLL = 8
ROW_SUBLANES = D_MODEL // LANES


def _rows_to_tiles(ref, x):
    for sl in range(ROW_SUBLANES):
        ref[:, sl, :] = x[:, sl * LANES:(sl + 1) * LANES]


def _tiles_to_rows(ref):
    return jnp.concatenate([ref[:, sl, :] for sl in range(ROW_SUBLANES)], axis=1)


def _moe_kernel(idx_prev, idx_cur, idx_next, gv_prev, gv_cur, h_ref, wg_ref, wu_ref, wd_ref, out_hbm,
                xg_a, xg_b, y_a, y_b, acc_ref, sem, *, cap):
    e = pl.program_id(1)
    n_e = pl.num_programs(1)
    nb = h_ref.shape[0]
    slots = xg_a.shape[0]
    unroll = ROW_LOOP_UNROLL

    def gather(idx_ref, xg_ref):
        for j in range(slots):
            xg_ref[j] = h_ref[j // cap, idx_ref[0, 0, 0, j]]

    def scatter(idx_ref, gv_ref, y_ref):
        for j0 in range(0, slots, unroll):
            bb = j0 // cap
            rows = [idx_ref[0, 0, 0, j0 + k] for k in range(unroll)]
            new = [acc_ref[bb, rows[k]] + gv_ref[0, 0, 0, j0 + k] * y_ref[j0 + k] for k in range(unroll)]
            for k in range(unroll):
                acc_ref[bb, rows[k]] = new[k]

    def expert(xg_ref, y_ref):
        half = slots // 2
        for r0 in (0, half):
            xg = _tiles_to_rows(xg_ref.at[r0:r0 + half]).astype(BF16)
            a = jnp.dot(xg, wg_ref[0], preferred_element_type=F32)
            u = jnp.dot(xg, wu_ref[0], preferred_element_type=F32)
            _rows_to_tiles(y_ref.at[r0:r0 + half], _bdot(_silu(a) * u, wd_ref[0]))

    @pl.when(e == 0)
    def _():
        acc_ref[...] = jnp.zeros_like(acc_ref)
        y_b[...] = jnp.zeros_like(y_b)
        gather(idx_cur, xg_a)

    def step(xg_cur, y_cur, xg_nxt, y_prv):
        scatter(idx_prev, gv_prev, y_prv)
        expert(xg_cur, y_cur)
        gather(idx_next, xg_nxt)

    @pl.when(e % 2 == 0)
    def _():
        step(xg_a, y_a, xg_b, y_b)

    @pl.when(e % 2 == 1)
    def _():
        step(xg_b, y_b, xg_a, y_a)

    @pl.when(e == n_e - 1)
    def _():
        scatter(idx_cur, gv_cur, y_b)
        cp = pltpu.make_async_copy(acc_ref, out_hbm.at[pl.ds(pl.program_id(0) * nb, nb)], sem)
        cp.start()
        cp.wait()


def _moe_call(idx, gval, h2, wg, wu, wd, row_block, n_rows, group):
    g_, n_e, _, slots = idx.shape
    assert n_e % 2 == 0 and slots % (2 * ROW_LOOP_UNROLL) == 0
    b_ = h2.shape[0]
    d, f = wg.shape[1], wg.shape[2]
    cap = slots // group
    tile = (ROW_SUBLANES, LANES)
    smem = lambda shift: pl.BlockSpec((1, 1, 1, slots), lambda g, e: (g, jnp.clip(e + shift, 0, n_e - 1), 0, 0),
                                      memory_space=pltpu.SMEM)
    rows_buf = pltpu.VMEM((slots,) + tile, F32)
    return pl.pallas_call(
        functools.partial(_moe_kernel, cap=cap),
        grid=(g_, n_e),
        in_specs=[smem(-1), smem(0), smem(1), smem(-1), smem(0),
                  pl.BlockSpec((group, n_rows) + tile, lambda g, e: (g, row_block, 0, 0),
                               pipeline_mode=pl.Buffered(1)),
                  pl.BlockSpec((1, d, f), lambda g, e: (e, 0, 0)),
                  pl.BlockSpec((1, d, f), lambda g, e: (e, 0, 0)),
                  pl.BlockSpec((1, f, d), lambda g, e: (e, 0, 0))],
        out_specs=pl.BlockSpec(memory_space=pl.ANY),
        out_shape=jax.ShapeDtypeStruct((b_, n_rows) + tile, F32),
        scratch_shapes=[rows_buf, rows_buf, rows_buf, rows_buf,
                        pltpu.VMEM((group, n_rows) + tile, F32), pltpu.SemaphoreType.DMA(())],
        compiler_params=_cparams(("arbitrary", "arbitrary"), vmem=60 * 1024 * 1024),
        name="moe_experts",
    )(idx, idx, idx, gval, gval, h2, wg, wu, wd)


def _route(aff, n_lat, n_ctx, with_ctx):
    b_ = aff.shape[0]

    def pick(a, n):
        cap = max(1, CAPACITY_FACTOR * n // N_EXPERTS)
        gval, idx = lax.top_k(jnp.swapaxes(a, 1, 2), cap)
        return gval, idx.astype(jnp.int32), cap

    gl, il, cap_l = pick(aff[:, :n_lat, :N_EXPERTS], n_lat)
    lat = (il.reshape(b_, N_EXPERTS, 1, cap_l), gl.reshape(b_, N_EXPERTS, 1, cap_l))
    if not with_ctx:
        return lat, None
    gc, ic, cap_c = pick(aff[:, n_lat:, :N_EXPERTS], n_ctx)
    flat = lambda t: jnp.transpose(t, (1, 0, 2)).reshape(1, N_EXPERTS, 1, b_ * cap_c)
    return lat, (flat(ic), flat(gc))


def _ln2_kernel(x1_ref, fl_ref, fc_ref, mod_ref, g_ref, b_ref, o_ref, *, alpha, n_lat_tiles):
    d = D_MODEL
    j = pl.program_id(1)
    f = jnp.where(j < n_lat_tiles, _tiles_to_rows(fl_ref.at[0]), _tiles_to_rows(fc_ref.at[0]))
    g2 = mod_ref[0, 0, :, 5 * d:6 * d]
    o_ref[0] = _layer_norm(alpha * x1_ref[0] + g2 * f, g_ref[...], b_ref[...])


def _ln2_call(x1, f_lat, f_ctx, modv, ln_g, ln_b, alpha, n_lat_tiles, n_tiles):
    b_, s, d = x1.shape
    ts = TOKEN_TILE
    return pl.pallas_call(
        functools.partial(_ln2_kernel, alpha=alpha, n_lat_tiles=n_lat_tiles),
        grid=(b_, n_tiles),
        in_specs=[pl.BlockSpec((1, ts, d), lambda b, j: (b, j, 0)),
                  pl.BlockSpec((1, ts, ROW_SUBLANES, LANES), lambda b, j: (b, jnp.minimum(j, n_lat_tiles - 1), 0, 0)),
                  pl.BlockSpec((1, ts, ROW_SUBLANES, LANES), lambda b, j: (b, jnp.maximum(j - n_lat_tiles, 0), 0, 0)),
                  pl.BlockSpec((1, 1, 1, 6 * d), lambda b, j: (b, j // n_lat_tiles, 0, 0)),
                  pl.BlockSpec((1, d), lambda b, j: (0, 0)),
                  pl.BlockSpec((1, d), lambda b, j: (0, 0))],
        out_specs=pl.BlockSpec((1, ts, d), lambda b, j: (b, j, 0)),
        out_shape=jax.ShapeDtypeStruct((b_, n_tiles * ts, d), F32),
        compiler_params=_cparams(("parallel", "arbitrary")),
        name="ln2",
    )(x1, f_lat, f_ctx, modv, ln_g, ln_b)


def _rope_tables(n_lat, n_ctx, dim):
    nf = dim // 4
    t = jnp.arange(n_lat)
    row = (t // GRID_W).astype(F32)
    col = (t % GRID_W).astype(F32)
    inv = ROPE_THETA ** (-jnp.arange(nf, dtype=F32) / nf)
    ar = row[:, None] * inv
    ac = col[:, None] * inv
    cos = jnp.concatenate([jnp.cos(ar), jnp.cos(ar), jnp.cos(ac), jnp.cos(ac)], axis=1)
    sin = jnp.concatenate([-jnp.sin(ar), jnp.sin(ar), -jnp.sin(ac), jnp.sin(ac)], axis=1)
    reps = BRANCH_W // dim
    cos = jnp.concatenate([jnp.tile(cos, (1, reps)), jnp.ones((n_ctx, BRANCH_W), F32)], axis=0)
    sin = jnp.concatenate([jnp.tile(sin, (1, reps)), jnp.zeros((n_ctx, BRANCH_W), F32)], axis=0)
    return cos, sin


def _reorder_w_in(w_in):
    o_ab = 3 * BRANCH_W
    o_gate = o_ab + 4 * N_HEADS
    o_b = o_gate + BRANCH_W
    o_gates = o_b + 3 * BRANCH_W + 4 * BRANCH_W
    pad = jnp.zeros(w_in.shape[:2] + (LANES - 4 * N_HEADS,), w_in.dtype)
    return jnp.concatenate([w_in[..., o_gates:], w_in[..., :o_ab], w_in[..., o_ab:o_gate], pad,
                            w_in[..., o_gate:o_b], w_in[..., o_b:o_gates]], axis=-1).astype(BF16)


def _lane_vec(p):
    v = jnp.zeros((LANES,), F32)
    v = v.at[0:N_HEADS].set(p[0]).at[8:8 + N_HEADS].set(p[1])
    return v.reshape(1, LANES)


def kernel(x, c, ctx, c_ctx, w_mod, b_mod, w_in, conv_a, a_log, dt_bias, gdn_norm, diff_lambda, diff_norm,
           qk_norm_c, sink_d, w_br, w_out, ln1_g, ln1_b, w_router, w_gate_e, w_up_e, w_down_e, ln2_g, ln2_b):
    b_, n_lat, d = x.shape
    n_ctx = ctx.shape[1]
    depth = w_mod.shape[0]
    ts = TOKEN_TILE
    assert d == D_MODEL and n_lat % ts == 0 and n_ctx == ts and n_lat % GRID_W == 0
    assert n_lat >= ts + 2 * WINDOW or n_lat == ts
    n_lat_tiles = n_lat // ts
    n_tiles = n_lat_tiles + n_ctx // ts
    alpha = (2.0 * depth) ** 0.25

    rows = -(-(b_ + 1) // 8) * 8
    cc = jnp.zeros((rows, d), F32).at[:b_].set(c).at[b_].set(c_ctx)
    mod_all = _mod_call(cc, w_mod, b_mod)

    w_all = _reorder_w_in(w_in)
    w_br_b = w_br.astype(BF16)
    w_out_b = w_out.astype(BF16)
    w_r = jnp.concatenate([w_router, jnp.zeros((depth, d, LANES - N_EXPERTS), F32)], axis=-1).astype(BF16)
    wg_b, wu_b, wd_b = w_gate_e.astype(BF16), w_up_e.astype(BF16), w_down_e.astype(BF16)
    conv_w = jnp.concatenate([conv_a, jnp.zeros((depth, 8 - CONV_K, conv_a.shape[2]), F32)], axis=1)
    tabs = _rope_tables(n_lat, n_ctx, HEAD_DIM) + _rope_tables(n_lat, n_ctx, DH_B)

    stream = (jnp.concatenate([x, ctx], axis=1),)
    for l in range(depth):
        last = l == depth - 1
        lam_init = 0.8 - 0.6 * math.exp(-0.3 * l)
        modv = jnp.stack([mod_all[l, :b_], jnp.broadcast_to(mod_all[l, b_], (b_, 6 * d))], axis=1)
        modv = modv.reshape(b_, 2, 1, 6 * d)
        gq = jnp.tile(qk_norm_c[l, 0], N_HEADS).reshape(1, BRANCH_W)
        gk = jnp.tile(qk_norm_c[l, 1], N_HEADS // 2).reshape(1, BRANCH_W // 2)

        proj = _inproj_call(stream, modv, w_all[l], tabs, gq, gk, n_lat_tiles, alpha)
        (gates, aqkv, aab, agate, qb, kb, vb, qc, kc, vc, qd, kd, vd) = proj[:13]
        x_all = stream[0] if l == 0 else proj[13]

        qn, kn, vn, gcol, grow = _gdn_prep_call(aqkv, aab, conv_w[l], _lane_vec(a_log[l]), _lane_vec(dt_bias[l]),
                                                n_lat_tiles)
        local_out = _gdn_local_call(qn, kn, vn, gcol, grow)
        o_f, o_b = _gdn_scan_call(local_out, n_lat // CHUNK, n_ctx // CHUNK)
        gdn = (o_f, o_b, agate, gdn_norm[l].reshape(1, HEAD_DIM))

        nq = n_lat_tiles if last else n_tiles
        out_b = _diff_attn_call(qb, kb, vb, diff_lambda[l], diff_norm[l].reshape(1, HEAD_DIM), lam_init,
                                n_lat, n_ctx, nq)
        out_c = _gqa_call(qc, kc, vc, n_lat, n_ctx, nq)
        out_d = _window_call(qd, kd, vd, sink_d[l], n_lat, n_ctx, nq)

        x1, h2, aff = _merge_call(gdn, (out_b, out_c, out_d), gates, x_all, modv, w_br_b[l], w_out_b[l],
                                  ln1_g[l].reshape(1, d), ln1_b[l].reshape(1, d), w_r[l], alpha,
                                  n_lat_tiles, nq)

        route_lat, route_ctx = _route(aff, n_lat, n_ctx, not last)
        f_lat = _moe_call(route_lat[0], route_lat[1], h2, wg_b[l], wu_b[l], wd_b[l], 0, n_lat, 1)
        if last:
            return _ln2_call(x1, f_lat, f_lat, modv, ln2_g[l].reshape(1, d), ln2_b[l].reshape(1, d), alpha,
                             n_lat_tiles, nq)
        f_ctx = _moe_call(route_ctx[0], route_ctx[1], h2, wg_b[l], wu_b[l], wd_b[l], n_lat // n_ctx, n_ctx, b_)
        stream = (x1, f_lat, f_ctx, modv, ln2_g[l].reshape(1, d), ln2_b[l].reshape(1, d))
```
